```python
import math
import jax
import jax.numpy as jnp
from jax import lax
import numpy as np

D_MODEL = 1024
BATCH = 8
SEQ = 2048
DEPTH = 4
DEC_BATCH = 128
DEC_SEQ = 4
PAST_LEN = 2048
PAGE_SIZE = 128

N_MIXERS = 3
N_A = (DEPTH + 2) // N_MIXERS
N_B = (DEPTH + 1) // N_MIXERS
N_C = DEPTH // N_MIXERS
CONV_A = 3
GDN_HEADS = 8
GDN_DK = 128
GDN_DV = 128
GDN_CONV = 4
GDN_CHUNK = 64
GDN_QK = GDN_HEADS * GDN_DK
GDN_V = GDN_HEADS * GDN_DV
GDN_CONV_CH = 2 * GDN_QK + GDN_V
GDN_PROJ = GDN_CONV_CH + GDN_V + 2 * GDN_HEADS
MOBA_HEADS = 8
MOBA_DH = 128
MOBA_W = MOBA_HEADS * MOBA_DH
MOBA_BLOCK = 256
MOBA_TOPK = 3
MOBA_QBLOCK = 32
MEM_LEN = 256
XA_HEADS = 4
XA_DH = D_MODEL // XA_HEADS
D_FF = -(-8 * D_MODEL // (3 * 256)) * 256
EPS = 1e-6
NEG_INF = -1e30

kernel_name = 'hybrid_conv_gdn_moba_step'


def rmsnorm(x, g):
    xf = x.astype(jnp.float32)
    y = xf * lax.rsqrt(jnp.mean(xf * xf, axis=-1, keepdims=True) + EPS)
    return (y * g.astype(jnp.float32)).astype(x.dtype)


def l2norm(x):
    xf = x.astype(jnp.float32)
    return xf * lax.rsqrt(jnp.sum(xf * xf, axis=-1, keepdims=True) + EPS)


def causal_depthwise_conv(x_pad, w, length):
    return sum(w[j] * x_pad[:, j:j + length] for j in range(w.shape[0]))


def shortconv_mixer(h, buf, w_in, w_conv, w_out):
    length = h.shape[1]
    gate_b, gate_c, u = jnp.split(h @ w_in, 3, axis=-1)
    z = jnp.concatenate([buf.astype(h.dtype), gate_c * u], axis=1)
    y = gate_b * causal_depthwise_conv(z, w_conv, length)
    return y @ w_out, z[:, -(CONV_A - 1):]


def gated_delta_rule(q, k, v, beta, g, s0):
    bsz, length, nh, dk = q.shape
    dv = v.shape[-1]
    c = math.gcd(length, GDN_CHUNK)
    n = length // c

    def blk(t):
        t = t.reshape((bsz, n, c, nh) + t.shape[3:])
        return jnp.moveaxis(t, (1, 3), (0, 2))

    q, k, v, beta, g = blk(q), blk(k), blk(v), blk(beta), blk(g)
    gam = jnp.cumsum(g, axis=-1)
    idx = jnp.arange(c)
    incl = idx[:, None] >= idx[None, :]
    strict = idx[:, None] > idx[None, :]
    diff = gam[..., :, None] - gam[..., None, :]
    decay = jnp.where(incl, jnp.exp(jnp.where(incl, diff, 0.0)), 0.0)
    kk = jnp.einsum('nbhid,nbhjd->nbhij', k, k)
    a_mat = jnp.where(strict, beta[..., :, None] * kk * decay, 0.0) + jnp.eye(c, dtype=jnp.float32)
    rhs = jnp.concatenate([beta[..., None] * v, (beta * jnp.exp(gam))[..., None] * k], axis=-1)
    sol = lax.linalg.triangular_solve(a_mat, rhs, left_side=True, lower=True, unit_diagonal=True)
    u_c, kd_c = sol[..., :dv], sol[..., dv:]
    aqk = jnp.einsum('nbhid,nbhjd->nbhij', q, k) * decay
    q_g = q * jnp.exp(gam)[..., None]
    g_last = gam[..., -1]
    k_dec = k * jnp.exp(g_last[..., None] - gam)[..., None]

    def step(s, xs):
        u_i, kd_i, aqk_i, qg_i, kdec_i, gl_i = xs
        w_i = u_i - jnp.einsum('bhcd,bhde->bhce', kd_i, s)
        o_i = jnp.einsum('bhcd,bhde->bhce', qg_i, s) + jnp.einsum('bhij,bhje->bhie', aqk_i, w_i)
        s = jnp.exp(gl_i)[..., None, None] * s + jnp.einsum('bhcd,bhce->bhde', kdec_i, w_i)
        return s, o_i

    s_fin, o = lax.scan(step, s0, (u_c, kd_c, aqk, q_g, k_dec, g_last))
    o = jnp.moveaxis(o, (0, 2), (1, 3)).reshape(bsz, length, nh, dv)
    return o, s_fin


def gdn_mixer(h, buf, s0, w_in, w_conv, a_log, dt_bias, g_norm, w_out):
    bsz, length, _ = h.shape
    proj = h @ w_in
    qkv, z, b_logit, a_logit = jnp.split(
        proj, [GDN_CONV_CH, GDN_CONV_CH + GDN_V, GDN_CONV_CH + GDN_V + GDN_HEADS], axis=-1)
    qkv_pad = jnp.concatenate([buf.astype(h.dtype), qkv], axis=1)
    qkv_c = jax.nn.silu(causal_depthwise_conv(qkv_pad, w_conv, length))
    q, k, v = jnp.split(qkv_c, [GDN_QK, 2 * GDN_QK], axis=-1)
    q = l2norm(q.reshape(bsz, length, GDN_HEADS, GDN_DK)) * GDN_DK ** -0.5
    k = l2norm(k.reshape(bsz, length, GDN_HEADS, GDN_DK))
    v = v.reshape(bsz, length, GDN_HEADS, GDN_DV).astype(jnp.float32)
    beta = jax.nn.sigmoid(b_logit.astype(jnp.float32))
    g = -jnp.exp(a_log.astype(jnp.float32)) * jax.nn.softplus(
        a_logit.astype(jnp.float32) + dt_bias.astype(jnp.float32))
    o, s_fin = gated_delta_rule(q, k, v, beta, g, s0.astype(jnp.float32))
    o = rmsnorm(o, g_norm) * jax.nn.silu(z.reshape(bsz, length, GDN_HEADS, GDN_DV).astype(jnp.float32))
    out = o.reshape(bsz, length, GDN_V).astype(h.dtype) @ w_out
    return out, qkv_pad[:, -(GDN_CONV - 1):], s_fin.astype(s0.dtype)


def moba_core(q, pos, k_sel, v_sel, valid, k_own, v_own, pos_own):
    scale = MOBA_DH ** -0.5
    s_own = jnp.einsum('qhd,rhd->qhr', q, k_own).astype(jnp.float32) * scale
    s_own = jnp.where(pos_own[None, None, :] <= pos[:, None, None], s_own, NEG_INF)
    if k_sel is None:
        p = jax.nn.softmax(s_own, axis=-1).astype(v_own.dtype)
        return jnp.einsum('qhr,rhd->qhd', p, v_own)
    s_sel = jnp.einsum('qhd,qhrd->qhr', q, k_sel).astype(jnp.float32) * scale
    if valid is not None:
        s_sel = jnp.where(valid, s_sel, NEG_INF)
    p = jax.nn.softmax(jnp.concatenate([s_sel, s_own], axis=-1), axis=-1).astype(v_own.dtype)
    r = k_sel.shape[2]
    return (jnp.einsum('qhr,qhrd->qhd', p[..., :r], v_sel)
            + jnp.einsum('qhr,rhd->qhd', p[..., r:], v_own))


def moba_prompt(h, w_qkv, w_out):
    bsz, length, _ = h.shape
    q, k, v = [t.reshape(bsz, length, MOBA_HEADS, MOBA_DH) for t in jnp.split(h @ w_qkv, 3, axis=-1)]
    nb = -(-length // MOBA_BLOCK)
    pad = ((0, 0), (0, nb * MOBA_BLOCK - length), (0, 0), (0, 0))
    kblk = jnp.pad(k, pad).reshape(bsz, nb, MOBA_BLOCK, MOBA_HEADS, MOBA_DH)
    vblk = jnp.pad(v, pad).reshape(bsz, nb, MOBA_BLOCK, MOBA_HEADS, MOBA_DH)
    kmean = kblk.astype(jnp.float32).mean(axis=2)
    n_sel = min(MOBA_TOPK, nb - 1)
    qb_len = math.gcd(length, MOBA_QBLOCK)
    nq = length // qb_len
    q_chunks = q.reshape(bsz * nq, qb_len, MOBA_HEADS, MOBA_DH)
    b_idx = jnp.repeat(jnp.arange(bsz), nq)
    q_start = jnp.tile(jnp.arange(nq) * qb_len, bsz)
    h_idx = jnp.arange(MOBA_HEADS)[None, :, None]

    def one(xs):
        qc, b, s0 = xs
        pos = s0 + jnp.arange(qb_len)
        own = s0 // MOBA_BLOCK
        kb, vb = kblk[b], vblk[b]
        k_own = lax.dynamic_index_in_dim(kb, own, axis=0, keepdims=False)
        v_own = lax.dynamic_index_in_dim(vb, own, axis=0, keepdims=False)
        pos_own = own * MOBA_BLOCK + jnp.arange(MOBA_BLOCK)
        if n_sel == 0:
            return moba_core(qc, pos, None, None, None, k_own, v_own, pos_own)
        s_blk = jnp.einsum('qhd,nhd->qhn', qc.astype(jnp.float32), kmean[b])
        s_blk = jnp.where(jnp.arange(nb) < own, s_blk, NEG_INF)
        vals, sel = lax.top_k(s_blk, n_sel)
        valid = jnp.repeat(vals > NEG_INF / 2, MOBA_BLOCK, axis=-1)
        k_sel = kb[sel, :, h_idx, :].reshape(qb_len, MOBA_HEADS, n_sel * MOBA_BLOCK, MOBA_DH)
        v_sel = vb[sel, :, h_idx, :].reshape(qb_len, MOBA_HEADS, n_sel * MOBA_BLOCK, MOBA_DH)
        return moba_core(qc, pos, k_sel, v_sel, valid, k_own, v_own, pos_own)

    o = lax.map(one, (q_chunks, b_idx, q_start))
    return o.reshape(bsz, length, MOBA_W) @ w_out, k, v


def moba_sample(h, pool_k, pool_v, page_table, w_qkv, w_out):
    bsz, length, _ = h.shape
    q, k, v = [t.reshape(bsz, length, MOBA_HEADS, MOBA_DH) for t in jnp.split(h @ w_qkv, 3, axis=-1)]
    n_pages = page_table.shape[1]
    past_len = n_pages * PAGE_SIZE
    bp = MOBA_BLOCK // PAGE_SIZE
    nb_past = past_len // MOBA_BLOCK
    own = nb_past
    tail0 = own * bp
    n_sel = min(MOBA_TOPK, nb_past)
    h_idx = jnp.arange(MOBA_HEADS)[None, :, None, None]

    def one(xs):
        qc, kn, vn, tb = xs
        pos = past_len + jnp.arange(length)
        k_own = jnp.concatenate([pool_k[tb[tail0:]].reshape(-1, MOBA_HEADS, MOBA_DH), kn], axis=0)
        v_own = jnp.concatenate([pool_v[tb[tail0:]].reshape(-1, MOBA_HEADS, MOBA_DH), vn], axis=0)
        pos_own = own * MOBA_BLOCK + jnp.arange(k_own.shape[0])
        if n_sel == 0:
            return moba_core(qc, pos, None, None, None, k_own, v_own, pos_own)
        kmean = pool_k[tb[:tail0]].astype(jnp.float32).reshape(
            nb_past, MOBA_BLOCK, MOBA_HEADS, MOBA_DH).mean(axis=1)
        s_blk = jnp.einsum('qhd,nhd->qhn', qc.astype(jnp.float32), kmean)
        _, sel = lax.top_k(s_blk, n_sel)
        phys = tb[sel[..., None] * bp + jnp.arange(bp)]
        k_sel = pool_k[phys, :, h_idx, :].reshape(length, MOBA_HEADS, n_sel * MOBA_BLOCK, MOBA_DH)
        v_sel = pool_v[phys, :, h_idx, :].reshape(length, MOBA_HEADS, n_sel * MOBA_BLOCK, MOBA_DH)
        return moba_core(qc, pos, k_sel, v_sel, None, k_own, v_own, pos_own)

    o = lax.map(one, (q, k, v, page_table))
    return o.reshape(bsz, length, MOBA_W) @ w_out, k, v


def memory_kv(mem, g, w_kv):
    bsz, m_len, _ = mem.shape
    mk, mv = jnp.split(rmsnorm(mem, g) @ w_kv, 2, axis=-1)
    return (mk.reshape(bsz, m_len, XA_HEADS, XA_DH), mv.reshape(bsz, m_len, XA_HEADS, XA_DH))


def cross_attention(h, mk, mv, w_q, w_o):
    bsz, length, _ = h.shape
    q = (h @ w_q).reshape(bsz, length, XA_HEADS, XA_DH)
    s = jnp.einsum('blhd,bmhd->bhlm', q, mk).astype(jnp.float32) * XA_DH ** -0.5
    p = jax.nn.softmax(s, axis=-1).astype(mv.dtype)
    o = jnp.einsum('bhlm,bmhd->blhd', p, mv).reshape(bsz, length, XA_HEADS * XA_DH)
    return o @ w_o


def swiglu(h, w_up, w_down):
    gate, up = jnp.split(h @ w_up, 2, axis=-1)
    return (jax.nn.silu(gate) * up) @ w_down


def setup_inputs(seed: int = 0) -> dict:
    key = jax.random.key(seed)
    ks = jax.random.split(key, 40)
    ctr = [0]
    f32 = jnp.float32

    def nk():
        ctr[0] += 1
        return ks[ctr[0] - 1]

    def rnd(shape, scale):
        return jax.random.normal(nk(), shape, f32) * scale

    def gain(shape):
        return 1.0 + 0.02 * jax.random.normal(nk(), shape, f32)

    n_pages = PAST_LEN // PAGE_SIZE
    n_used = DEC_BATCH * n_pages
    n_pool = (5 * n_used + 3) // 4
    page_table = jax.random.permutation(nk(), n_pool)[:n_used].reshape(DEC_BATCH, n_pages).astype(jnp.int32)
    dt = jnp.exp(jax.random.uniform(nk(), (N_B, GDN_HEADS), f32, math.log(1e-3), math.log(1e-1)))
    return {
        'x_prompt': rnd((BATCH, SEQ, D_MODEL), 1.0),
        'x_sample': rnd((DEC_BATCH, DEC_SEQ, D_MODEL), 1.0),
        'state_a_conv': rnd((N_A, DEC_BATCH, CONV_A - 1, D_MODEL), 1.0),
        'state_b_conv': rnd((N_B, DEC_BATCH, GDN_CONV - 1, GDN_CONV_CH), 1.0),
        'state_b_rec': rnd((N_B, DEC_BATCH, GDN_HEADS, GDN_DK, GDN_DV), GDN_DK ** -0.5),
        'cache_c_k': rnd((N_C, n_pool, PAGE_SIZE, MOBA_HEADS, MOBA_DH), 1.0),
        'cache_c_v': rnd((N_C, n_pool, PAGE_SIZE, MOBA_HEADS, MOBA_DH), 1.0),
        'cache_mem_k': rnd((DEPTH, DEC_BATCH, MEM_LEN, XA_HEADS, XA_DH), 1.0),
        'cache_mem_v': rnd((DEPTH, DEC_BATCH, MEM_LEN, XA_HEADS, XA_DH), 1.0),
        'page_table': page_table,
        'mem_prompt': rnd((BATCH, MEM_LEN, D_MODEL), 1.0),
        'norm_mix': gain((DEPTH, D_MODEL)),
        'norm_mem': gain((DEPTH, D_MODEL)),
        'norm_xattn': gain((DEPTH, D_MODEL)),
        'norm_ffn': gain((DEPTH, D_MODEL)),
        'norm_final': gain((D_MODEL,)),
        'a_w_in': rnd((N_A, D_MODEL, 3 * D_MODEL), D_MODEL ** -0.5),
        'a_w_conv': rnd((N_A, CONV_A, D_MODEL), CONV_A ** -0.5),
        'a_w_out': rnd((N_A, D_MODEL, D_MODEL), D_MODEL ** -0.5),
        'b_w_in': rnd((N_B, D_MODEL, GDN_PROJ), D_MODEL ** -0.5),
        'b_w_conv': rnd((N_B, GDN_CONV, GDN_CONV_CH), GDN_CONV ** -0.5),
        'b_a_log': jnp.log(jax.random.uniform(nk(), (N_B, GDN_HEADS), f32, 1.0, 16.0)),
        'b_dt_bias': dt + jnp.log(-jnp.expm1(-dt)),
        'b_norm': gain((N_B, GDN_DV)),
        'b_w_out': rnd((N_B, GDN_V, D_MODEL), GDN_V ** -0.5),
        'c_w_qkv': rnd((N_C, D_MODEL, 3 * MOBA_W), D_MODEL ** -0.5),
        'c_w_out': rnd((N_C, MOBA_W, D_MODEL), MOBA_W ** -0.5),
        'x_w_q': rnd((DEPTH, D_MODEL, XA_HEADS * XA_DH), D_MODEL ** -0.5),
        'x_w_kv': rnd((DEPTH, D_MODEL, 2 * XA_HEADS * XA_DH), D_MODEL ** -0.5),
        'x_w_o': rnd((DEPTH, XA_HEADS * XA_DH, D_MODEL), (XA_HEADS * XA_DH) ** -0.5),
        'f_w_up': rnd((DEPTH, D_MODEL, 2 * D_FF), D_MODEL ** -0.5),
        'f_w_down': rnd((DEPTH, D_FF, D_MODEL), D_FF ** -0.5),
    }


def reference(x_prompt, x_sample, state_a_conv, state_b_conv, state_b_rec, cache_c_k, cache_c_v,
              cache_mem_k, cache_mem_v, page_table, mem_prompt,
              norm_mix, norm_mem, norm_xattn, norm_ffn, norm_final,
              a_w_in, a_w_conv, a_w_out,
              b_w_in, b_w_conv, b_a_log, b_dt_bias, b_norm, b_w_out,
              c_w_qkv, c_w_out, x_w_q, x_w_kv, x_w_o, f_w_up, f_w_down):
    xp, xs = x_prompt, x_sample
    bp_, bs_ = x_prompt.shape[0], x_sample.shape[0]
    a_p, a_s, bc_p, bc_s, br_p, br_s = [], [], [], [], [], []
    ck_p, cv_p, ck_s, cv_s, mk_p, mv_p = [], [], [], [], [], []
    for i in range(DEPTH):
        kind, j = i % N_MIXERS, i // N_MIXERS
        hp = rmsnorm(xp, norm_mix[i])
        hs = rmsnorm(xs, norm_mix[i])
        if kind == 0:
            zero_buf = jnp.zeros((bp_, CONV_A - 1, D_MODEL), xp.dtype)
            yp, nbp = shortconv_mixer(hp, zero_buf, a_w_in[j], a_w_conv[j], a_w_out[j])
            ys, nbs = shortconv_mixer(hs, state_a_conv[j], a_w_in[j], a_w_conv[j], a_w_out[j])
            a_p.append(nbp)
            a_s.append(nbs)
        elif kind == 1:
            zero_buf = jnp.zeros((bp_, GDN_CONV - 1, GDN_CONV_CH), xp.dtype)
            zero_s = jnp.zeros((bp_, GDN_HEADS, GDN_DK, GDN_DV), xp.dtype)
            yp, nbp, nsp = gdn_mixer(hp, zero_buf, zero_s, b_w_in[j], b_w_conv[j], b_a_log[j],
                                     b_dt_bias[j], b_norm[j], b_w_out[j])
            ys, nbs, nss = gdn_mixer(hs, state_b_conv[j], state_b_rec[j], b_w_in[j], b_w_conv[j],
                                     b_a_log[j], b_dt_bias[j], b_norm[j], b_w_out[j])
            bc_p.append(nbp)
            bc_s.append(nbs)
            br_p.append(nsp)
            br_s.append(nss)
        else:
            yp, kp, vp = moba_prompt(hp, c_w_qkv[j], c_w_out[j])
            ys, kn, vn = moba_sample(hs, cache_c_k[j], cache_c_v[j], page_table, c_w_qkv[j], c_w_out[j])
            ck_p.append(kp)
            cv_p.append(vp)
            ck_s.append(kn)
            cv_s.append(vn)
        xp = xp + yp
        xs = xs + ys
        mkp, mvp = memory_kv(mem_prompt, norm_mem[i], x_w_kv[i])
        mk_p.append(mkp)
        mv_p.append(mvp)
        xp = xp + cross_attention(rmsnorm(xp, norm_xattn[i]), mkp, mvp, x_w_q[i], x_w_o[i])
        xs = xs + cross_attention(rmsnorm(xs, norm_xattn[i]), cache_mem_k[i], cache_mem_v[i], x_w_q[i], x_w_o[i])
        xp = xp + swiglu(rmsnorm(xp, norm_ffn[i]), f_w_up[i], f_w_down[i])
        xs = xs + swiglu(rmsnorm(xs, norm_ffn[i]), f_w_up[i], f_w_down[i])
    y_prompt = rmsnorm(xp, norm_final)
    y_sample = rmsnorm(xs, norm_final)
    return (y_prompt, y_sample,
            jnp.stack(a_p), jnp.stack(a_s),
            jnp.stack(bc_p), jnp.stack(bc_s),
            jnp.stack(br_p), jnp.stack(br_s),
            jnp.stack(ck_p), jnp.stack(cv_p), jnp.stack(ck_s), jnp.stack(cv_s),
            jnp.stack(mk_p), jnp.stack(mv_p))
```

```python
import functools
import math

import numpy as np
import jax
import jax.numpy as jnp
from jax import lax
from jax.experimental import pallas as pl
from jax.experimental.pallas import tpu as pltpu

F32 = jnp.float32
BF = jnp.bfloat16
SDS = jax.ShapeDtypeStruct

EPS = 1e-6
NEG_INF = -1e30
LANES = 128
SUBLANES = 8
VMEM_LIMIT = 56 * 1024 * 1024

GDN_HEADS = 8
GDN_DK = 128
GDN_CHUNK = 64
MOBA_HEADS = 8
MOBA_DH = 128
MOBA_BLOCK = 256
MOBA_TOPK = 3
PAGE_SIZE = 128
XA_HEADS = 4


def _params(n_axes):
    return pltpu.CompilerParams(dimension_semantics=("arbitrary",) * n_axes,
                                vmem_limit_bytes=VMEM_LIMIT)


def _const_spec(shape):
    nd = len(shape)
    return pl.BlockSpec(shape, lambda *_: (0,) * nd, pipeline_mode=pl.Buffered(1))


def _whole_spec(shape):
    nd = len(shape)
    return pl.BlockSpec(shape, lambda *_: (0,) * nd)


def _dot(a, b):
    return jnp.dot(a, b, preferred_element_type=F32)


def _dot_nt(a, b):
    return lax.dot_general(a, b, (((1,), (1,)), ((), ())), preferred_element_type=F32)


def _dot_tn(a, b):
    return lax.dot_general(a, b, (((0,), (0,)), ((), ())), preferred_element_type=F32)


def _split3(a):
    a0 = a.astype(BF)
    r = a - a0.astype(F32)
    a1 = r.astype(BF)
    a2 = (r - a1.astype(F32)).astype(BF)
    return a0, a1, a2


def _dot_exact_rhs(a, b_bf, dot=_dot):
    a0, a1, a2 = _split3(a)
    return dot(a0, b_bf) + dot(a1, b_bf) + dot(a2, b_bf)


def _dot_hi(a, b, dot=_dot):
    a0, a1, _ = _split3(a)
    b0, b1, _ = _split3(b)
    return dot(a0, b0) + (dot(a0, b1) + dot(a1, b0))


def _rms(x, g):
    ms = jnp.mean(x * x, axis=-1, keepdims=True)
    return x * lax.rsqrt(ms + EPS) * g


def _silu(x):
    return x * jax.nn.sigmoid(x)


def _softplus(x):
    return jnp.maximum(x, 0.0) + jnp.log1p(jnp.exp(-jnp.abs(x)))


def _norm_mm_body(x_ref, g_ref, w_ref, *o_refs):
    h = _rms(x_ref[...], g_ref[...]).astype(BF)
    n = o_refs[0].shape[-1]
    for j, o_ref in enumerate(o_refs):
        o_ref[...] = _dot(h, w_ref[:, j * n:(j + 1) * n])


def norm_mm(x, g, w, n_out, tm, name):
    t, k = x.shape
    n = w.shape[1] // n_out
    return pl.pallas_call(
        _norm_mm_body,
        grid=(t // tm,),
        in_specs=[pl.BlockSpec((tm, k), lambda i: (i, 0)), _const_spec((1, k)), _const_spec(w.shape)],
        out_specs=[pl.BlockSpec((tm, n), lambda i: (i, 0))] * n_out,
        out_shape=[SDS((t, n), F32)] * n_out,
        compiler_params=_params(1),
        name=name,
    )(x, g.reshape(1, k), w)


def _mm_res_body(a_ref, w_ref, r_ref, o_ref):
    o_ref[...] = r_ref[...] + _dot(a_ref[...].astype(BF), w_ref[...])


def mm_res(a, w, res, tm, name):
    t, k = a.shape
    n = w.shape[1]
    return pl.pallas_call(
        _mm_res_body,
        grid=(t // tm,),
        in_specs=[pl.BlockSpec((tm, k), lambda i: (i, 0)), _const_spec(w.shape),
                  pl.BlockSpec((tm, n), lambda i: (i, 0))],
        out_specs=pl.BlockSpec((tm, n), lambda i: (i, 0)),
        out_shape=SDS((t, n), F32),
        compiler_params=_params(1),
        name=name,
    )(a, w, res)


def _memkv_body(x_ref, g_ref, w_ref, k32_ref, v32_ref, kbf_ref, vbf_ref):
    h = _rms(x_ref[...], g_ref[0]).astype(BF)
    d = k32_ref.shape[-1]
    k = _dot(h, w_ref[0, :, :d])
    v = _dot(h, w_ref[0, :, d:])
    k32_ref[0] = k
    v32_ref[0] = v
    kbf_ref[0] = k.astype(BF)
    vbf_ref[0] = v.astype(BF)


def memory_kv_all(mem2d, norm_mem, w_kv, tm):
    depth, d, _ = w_kv.shape
    t = mem2d.shape[0]
    out_spec = pl.BlockSpec((1, tm, d), lambda l, i: (l, i, 0))
    return pl.pallas_call(
        _memkv_body,
        grid=(depth, t // tm),
        in_specs=[pl.BlockSpec((tm, d), lambda l, i: (i, 0)),
                  pl.BlockSpec((1, 1, d), lambda l, i: (l, 0, 0)),
                  pl.BlockSpec((1, d, 2 * d), lambda l, i: (l, 0, 0))],
        out_specs=[out_spec] * 4,
        out_shape=[SDS((depth, t, d), F32)] * 2 + [SDS((depth, t, d), BF)] * 2,
        compiler_params=_params(2),
        name="memory_kv",
    )(mem2d, norm_mem.reshape(depth, 1, d), w_kv)


def _ffn_body(x_ref, g_ref, wu_ref, wd_ref, gf_ref, o_ref, *, ck, final):
    x = x_ref[...]
    h = _rms(x, g_ref[...]).astype(BF)
    dff = wd_ref.shape[0]
    acc = x
    for c0 in range(0, dff, ck):
        gate = _dot(h, wu_ref[:, c0:c0 + ck])
        up = _dot(h, wu_ref[:, dff + c0:dff + c0 + ck])
        a = (_silu(gate) * up).astype(BF)
        acc = acc + _dot(a, wd_ref[c0:c0 + ck, :])
    if final:
        acc = _rms(acc, gf_ref[...])
    o_ref[...] = acc


def ffn(x, g, w_up, w_down, g_final, final, tm, name):
    t, d = x.shape
    body = functools.partial(_ffn_body, ck=2 * LANES, final=final)
    return pl.pallas_call(
        body,
        grid=(t // tm,),
        in_specs=[pl.BlockSpec((tm, d), lambda i: (i, 0)), _const_spec((1, d)),
                  _const_spec(w_up.shape), _const_spec(w_down.shape), _const_spec((1, d))],
        out_specs=pl.BlockSpec((tm, d), lambda i: (i, 0)),
        out_shape=SDS((t, d), F32),
        compiler_params=_params(1),
        name=name,
    )(x, g.reshape(1, d), w_up, w_down, g_final.reshape(1, d))


def _xattn_p_body(x_ref, g_ref, wq_ref, k_ref, v_ref, wo_ref, o_ref, *, nh):
    x = x_ref[...]
    d = x.shape[-1]
    dh = d // nh
    h = _rms(x, g_ref[...]).astype(BF)
    q = (_dot(h, wq_ref[...]) * dh ** -0.5).astype(BF)
    outs = []
    for hh in range(nh):
        cs = slice(hh * dh, (hh + 1) * dh)
        s = _dot_nt(q[:, cs], k_ref[0, :, cs])
        p = jnp.exp(s - jnp.max(s, axis=-1, keepdims=True))
        l = jnp.sum(p, axis=-1, keepdims=True)
        outs.append((_dot(p.astype(BF), v_ref[0, :, cs]) / l).astype(BF))
    o_ref[...] = x + _dot(jnp.concatenate(outs, axis=-1), wo_ref[...])


def xattn_prompt(x, g, w_q, k_bf, v_bf, w_o, bsz, tm, name):
    t, d = x.shape
    nt = t // bsz // tm
    m_len = k_bf.shape[0] // bsz
    kv = lambda a: a.reshape(bsz, m_len, d)
    row = lambda b, i: (b * nt + i, 0)
    return pl.pallas_call(
        functools.partial(_xattn_p_body, nh=XA_HEADS),
        grid=(bsz, nt),
        in_specs=[pl.BlockSpec((tm, d), row), _const_spec((1, d)), _const_spec(w_q.shape),
                  pl.BlockSpec((1, m_len, d), lambda b, i: (b, 0, 0)),
                  pl.BlockSpec((1, m_len, d), lambda b, i: (b, 0, 0)),
                  _const_spec(w_o.shape)],
        out_specs=pl.BlockSpec((tm, d), row),
        out_shape=SDS((t, d), F32),
        compiler_params=_params(2),
        name=name,
    )(x, g.reshape(1, d), w_q, kv(k_bf), kv(v_bf), w_o)


def _xattn_s_body(qt_ref, k_ref, v_ref, o_ref, *, nb, nh):
    m_len, _, dh = k_ref.shape[2:]
    rows = m_len * nh
    ncol = qt_ref.shape[-1]
    rh = lax.broadcasted_iota(jnp.int32, (rows, ncol), 0) % nh
    ch = lax.broadcasted_iota(jnp.int32, (rows, ncol), 1) % nh
    same_head = rh == ch
    for j in range(nb):
        k2 = k_ref[0, j].reshape(rows, dh).astype(BF)
        v2 = v_ref[0, j].reshape(rows, dh).astype(BF)
        s = _dot(k2, qt_ref[j].astype(BF)) * dh ** -0.5
        s = jnp.where(same_head, s, NEG_INF)
        p = jnp.exp(s - jnp.max(s, axis=0, keepdims=True))
        p = p / jnp.sum(p, axis=0, keepdims=True)
        o_ref[j] = _dot_tn(p.astype(BF), v2)


def xattn_sample_core(qt, cache_k, cache_v, layer, nb, name):
    bsz, dh, ncol = qt.shape
    m_len, nh = cache_k.shape[2:4]
    kv_spec = pl.BlockSpec((1, nb, m_len, nh, dh), lambda i: (layer, i, 0, 0, 0))
    return pl.pallas_call(
        functools.partial(_xattn_s_body, nb=nb, nh=nh),
        grid=(bsz // nb,),
        in_specs=[pl.BlockSpec((nb, dh, ncol), lambda i: (i, 0, 0)), kv_spec, kv_spec],
        out_specs=pl.BlockSpec((nb, ncol, dh), lambda i: (i, 0, 0)),
        out_shape=SDS((bsz, ncol, dh), F32),
        compiler_params=_params(1),
        name=name,
    )(qt, cache_k, cache_v)


def _mixa_p_body(x_ref, g_ref, win_ref, wc_ref, wout_ref, o_ref, st_ref, zbuf, *, nt):
    i = pl.program_id(1)
    tm, d = x_ref.shape
    width = wc_ref.shape[0]

    @pl.when(i == 0)
    def _():
        zbuf[0:SUBLANES, :] = jnp.zeros((SUBLANES, d), F32)

    x = x_ref[...]
    h = _rms(x, g_ref[...]).astype(BF)
    gate_b = _dot(h, win_ref[:, 0:d])
    zn = _dot(h, win_ref[:, d:2 * d]) * _dot(h, win_ref[:, 2 * d:3 * d])
    zbuf[SUBLANES:SUBLANES + tm, :] = zn
    conv = wc_ref[width - 1:width, :] * zn
    for j in range(width - 1):
        off = SUBLANES - (width - 1) + j
        conv = conv + wc_ref[j:j + 1, :] * zbuf[off:off + tm, :]
    o_ref[...] = x + _dot((gate_b * conv).astype(BF), wout_ref[...])
    zbuf[0:SUBLANES, :] = zbuf[tm:tm + SUBLANES, :]

    @pl.when(i == nt - 1)
    def _():
        st_ref[0] = zbuf[SUBLANES - (width - 1):SUBLANES, :]


def mixer_a_prompt(x, g, w_in, w_conv, w_out, bsz, tm, name):
    t, d = x.shape
    nt = t // bsz // tm
    width = w_conv.shape[0]
    row = lambda b, i: (b * nt + i, 0)
    return pl.pallas_call(
        functools.partial(_mixa_p_body, nt=nt),
        grid=(bsz, nt),
        in_specs=[pl.BlockSpec((tm, d), row), _const_spec((1, d)), _const_spec(w_in.shape),
                  _const_spec(w_conv.shape), _const_spec(w_out.shape)],
        out_specs=[pl.BlockSpec((tm, d), row), pl.BlockSpec((1, width - 1, d), lambda b, i: (b, 0, 0))],
        out_shape=[SDS((t, d), F32), SDS((bsz, width - 1, d), F32)],
        scratch_shapes=[pltpu.VMEM((tm + SUBLANES, d), F32)],
        compiler_params=_params(2),
        name=name,
    )(x, g.reshape(1, d), w_in, w_conv, w_out)


def _posmajor_conv(state, z, wc_ref, bsz):
    width = wc_ref.shape[0]
    n = z.shape[0]
    zpad = jnp.concatenate([state, z], axis=0)
    conv = wc_ref[0:1, :] * zpad[0:n]
    for j in range(1, width):
        conv = conv + wc_ref[j:j + 1, :] * zpad[j * bsz:j * bsz + n]
    return conv, zpad[n:]


def _mixa_s_body(x_ref, st_ref, g_ref, win_ref, wc_ref, wout_ref, o_ref, nst_ref, *, bsz):
    x = x_ref[...]
    d = x.shape[-1]
    h = _rms(x, g_ref[...]).astype(BF)
    gate_b = _dot(h, win_ref[:, 0:d])
    zn = _dot(h, win_ref[:, d:2 * d]) * _dot(h, win_ref[:, 2 * d:3 * d])
    conv, new_state = _posmajor_conv(st_ref[...], zn, wc_ref, bsz)
    o_ref[...] = x + _dot((gate_b * conv).astype(BF), wout_ref[...])
    nst_ref[...] = new_state


def mixer_a_sample(x, state_pm, g, w_in, w_conv, w_out, bsz, name):
    t, d = x.shape
    return pl.pallas_call(
        functools.partial(_mixa_s_body, bsz=bsz),
        grid=(1,),
        in_specs=[_const_spec(x.shape), _const_spec(state_pm.shape), _const_spec((1, d)),
                  _const_spec(w_in.shape), _const_spec(w_conv.shape), _const_spec(w_out.shape)],
        out_specs=[_whole_spec(x.shape), _whole_spec(state_pm.shape)],
        out_shape=[SDS(x.shape, F32), SDS(state_pm.shape, F32)],
        compiler_params=_params(1),
        name=name,
    )(x, state_pm, g.reshape(1, d), w_in, w_conv, w_out)


def _gdn_activations(conv, ba, alog_ref, dtb_ref, q_ref, k_ref, v_ref, bg_ref):
    nh, dk = GDN_HEADS, GDN_DK
    qkv = _silu(conv)
    for hh in range(nh):
        for (ref, base, scale) in ((q_ref, 0, dk ** -0.5), (k_ref, nh * dk, 1.0)):
            t = qkv[:, base + hh * dk:base + (hh + 1) * dk]
            t = t * lax.rsqrt(jnp.sum(t * t, axis=-1, keepdims=True) + EPS)
            ref[:, hh * dk:(hh + 1) * dk] = t * scale if scale != 1.0 else t
    v_ref[...] = qkv[:, 2 * nh * dk:]
    lane = lax.broadcasted_iota(jnp.int32, ba.shape, 1)
    beta = jax.nn.sigmoid(ba)
    g = -jnp.exp(alog_ref[...]) * _softplus(ba + dtb_ref[...])
    bg_ref[...] = jnp.where(lane < nh, beta, jnp.where(lane < 2 * nh, g, 0.0))


def _gdn_proj_p_body(x_ref, g_ref, w_ref, wba_ref, wc_ref, alog_ref, dtb_ref,
                     q_ref, k_ref, v_ref, z_ref, bg_ref, st_ref, cbuf, *, nt):
    i = pl.program_id(1)
    tm = x_ref.shape[0]
    width, cc = wc_ref.shape

    @pl.when(i == 0)
    def _():
        cbuf[0:SUBLANES, :] = jnp.zeros((SUBLANES, cc), F32)

    h = _rms(x_ref[...], g_ref[...]).astype(BF)
    qkv = _dot(h, w_ref[:, :cc])
    z_ref[...] = _dot(h, w_ref[:, cc:])
    ba = _dot(h, wba_ref[...])
    cbuf[SUBLANES:SUBLANES + tm, :] = qkv
    conv = wc_ref[width - 1:width, :] * qkv
    for j in range(width - 1):
        off = SUBLANES - (width - 1) + j
        conv = conv + wc_ref[j:j + 1, :] * cbuf[off:off + tm, :]
    _gdn_activations(conv, ba, alog_ref, dtb_ref, q_ref, k_ref, v_ref, bg_ref)
    cbuf[0:SUBLANES, :] = cbuf[tm:tm + SUBLANES, :]

    @pl.when(i == nt - 1)
    def _():
        st_ref[0] = cbuf[SUBLANES - (width - 1):SUBLANES, :]


def gdn_proj_prompt(x, g, w_qkvz, w_ba, w_conv, alog, dtb, bsz, tm, name):
    t, d = x.shape
    nt = t // bsz // tm
    width, cc = w_conv.shape
    hd = w_qkvz.shape[1] - cc
    row = lambda b, i: (b * nt + i, 0)
    tile = lambda n: pl.BlockSpec((tm, n), row)
    return pl.pallas_call(
        functools.partial(_gdn_proj_p_body, nt=nt),
        grid=(bsz, nt),
        in_specs=[tile(d), _const_spec((1, d)), _const_spec(w_qkvz.shape), _const_spec(w_ba.shape),
                  _const_spec(w_conv.shape), _const_spec(alog.shape), _const_spec(dtb.shape)],
        out_specs=[tile(hd), tile(hd), tile(hd), tile(hd), tile(LANES),
                   pl.BlockSpec((1, width - 1, cc), lambda b, i: (b, 0, 0))],
        out_shape=[SDS((t, hd), F32)] * 4 + [SDS((t, LANES), F32), SDS((bsz, width - 1, cc), F32)],
        scratch_shapes=[pltpu.VMEM((tm + SUBLANES, cc), F32)],
        compiler_params=_params(2),
        name=name,
    )(x, g.reshape(1, d), w_qkvz, w_ba, w_conv, alog, dtb)


def _gdn_proj_s_body(x_ref, st_ref, g_ref, w_ref, wba_ref, wc_ref, alog_ref, dtb_ref,
                     q_ref, k_ref, v_ref, z_ref, bg_ref, nst_ref, *, bsz):
    cc = wc_ref.shape[1]
    h = _rms(x_ref[...], g_ref[...]).astype(BF)
    qkv = _dot(h, w_ref[:, :cc])
    z_ref[...] = _dot(h, w_ref[:, cc:])
    ba = _dot(h, wba_ref[...])
    conv, new_state = _posmajor_conv(st_ref[...], qkv, wc_ref, bsz)
    _gdn_activations(conv, ba, alog_ref, dtb_ref, q_ref, k_ref, v_ref, bg_ref)
    nst_ref[...] = new_state


def gdn_proj_sample(x, state_pm, g, w_qkvz, w_ba, w_conv, alog, dtb, bsz, name):
    t, d = x.shape
    cc = w_conv.shape[1]
    hd = w_qkvz.shape[1] - cc
    ins = (x, state_pm, g.reshape(1, d), w_qkvz, w_ba, w_conv, alog, dtb)
    outs = [SDS((t, hd), F32)] * 4 + [SDS((t, LANES), F32), SDS(state_pm.shape, F32)]
    return pl.pallas_call(
        functools.partial(_gdn_proj_s_body, bsz=bsz),
        grid=(1,),
        in_specs=[_const_spec(a.shape) for a in ins],
        out_specs=[_whole_spec(o.shape) for o in outs],
        out_shape=outs,
        compiler_params=_params(1),
        name=name,
    )(*ins)


def _pad_transpose(a):
    c = a.shape[0]
    if c < LANES:
        a = jnp.concatenate([a, jnp.zeros((LANES - c, a.shape[1]), a.dtype)], axis=0)
    return a.T


def _gdn_scan_body(*refs, c, per_chunk_state, fuse_out):
    if fuse_out:
        x_ref, refs = refs[0], refs[1:]
    (q_ref, k_ref, v_ref, z_ref, bg_ref, tri_ref, e_ref, s0_ref, gn_ref) = refs[:9]
    refs = refs[9:]
    if fuse_out:
        wout_ref, refs = refs[0], refs[1:]
    o_ref, s_ref, qg_sc, kdec_sc, rk_sc, rv_sc, gam_sc, beta_sc, egl_sc, o_sc = refs

    nh, dk = GDN_HEADS, GDN_DK
    rows, hd = q_ref.shape
    n_chunks = rows // c
    n_double = int(math.log2(c)) - 1

    if per_chunk_state:
        s_ref[...] = s0_ref[...]
    else:
        @pl.when(pl.program_id(1) == 0)
        def _():
            s_ref[...] = s0_ref[...]

    bg = bg_ref[...]
    lane = lax.broadcasted_iota(jnp.int32, bg.shape, 1)
    gam = _dot_exact_rhs(bg, tri_ref[...], dot=lambda a, b: _dot(b, a))
    y = jnp.where(lane < nh, bg, gam)
    yb = _dot_exact_rhs(y, e_ref[...])
    beta_b = yb[:, :hd]
    gam_b = yb[:, hd:]
    gl_b = jnp.broadcast_to(gam_b.reshape(n_chunks, c, hd)[:, c - 1:c, :], (n_chunks, c, hd)).reshape(rows, hd)
    eg = jnp.exp(gam_b)
    k = k_ref[...]
    qg_sc[...] = q_ref[...] * eg
    rk_sc[...] = beta_b * eg * k
    rv_sc[...] = beta_b * v_ref[...]
    kdec_sc[...] = k * jnp.exp(gl_b - gam_b)
    gam_sc[...] = gam_b
    beta_sc[...] = beta_b
    egl_sc[...] = jnp.exp(gl_b)

    ii = lax.broadcasted_iota(jnp.int32, (c, c), 0)
    jj = lax.broadcasted_iota(jnp.int32, (c, c), 1)
    incl = ii >= jj
    strict = ii > jj

    def chunk(ci, carry):
        r0 = pl.multiple_of(ci * c, c)
        rs = pl.ds(r0, c)
        si = ci if per_chunk_state else 0
        for hh in range(nh):
            hc = slice(hh * dk, (hh + 1) * dk)
            kc = k_ref[rs, hc]
            kq = _dot_nt(jnp.concatenate([kc, q_ref[rs, hc]], axis=0).astype(BF), kc.astype(BF))
            gcol = gam_sc[rs, hc]
            diff = gcol[:, :c] - _pad_transpose(gcol)[:c, :c]
            decay = jnp.where(incl, jnp.exp(jnp.where(incl, diff, 0.0)), 0.0)
            m = jnp.where(strict, -(beta_sc[rs, hc][:, :c] * kq[:c] * decay), 0.0)
            pm = m
            mk = m
            for _ in range(n_double):
                mkb = mk.astype(BF)
                mk = _dot(mkb, mkb)
                pm = pm + mk + _dot(pm.astype(BF), mk.astype(BF))
            rhs = jnp.concatenate([rv_sc[rs, hc], rk_sc[rs, hc]], axis=1)
            sol = rhs + _dot(pm.astype(BF), rhs.astype(BF))
            u = sol[:, :dk]
            kd = sol[:, dk:]
            aqk = kq[c:] * decay
            s = s_ref[si, hh]
            t1 = _dot(jnp.concatenate([kd, qg_sc[rs, hc]], axis=0).astype(BF), s.astype(BF))
            w = u - t1[:c]
            kdt = _pad_transpose(kdec_sc[rs, hc])[:, :c]
            t2 = _dot(jnp.concatenate([aqk, kdt], axis=0).astype(BF), w.astype(BF))
            o_sc[rs, hc] = t1[c:] + t2[:c]
            egl = jnp.broadcast_to(egl_sc[pl.ds(r0, 1), hc], (dk, dk))
            s_ref[si, hh] = egl * s + t2[c:]
        return carry

    lax.fori_loop(0, n_chunks, chunk, 0)

    outs = []
    for hh in range(nh):
        hc = slice(hh * dk, (hh + 1) * dk)
        outs.append(_rms(o_sc[:, hc], gn_ref[...]) * _silu(z_ref[:, hc]))
    yv = jnp.concatenate(outs, axis=-1)
    if fuse_out:
        o_ref[...] = x_ref[...] + _dot(yv.astype(BF), wout_ref[...])
    else:
        o_ref[...] = yv


def _gdn_consts(rows, c, hd):
    r = np.arange(rows)
    tri = ((r[:, None] >= r[None, :]) & (r[:, None] // c == r[None, :] // c)).astype(np.float32)
    lane = np.arange(LANES)[:, None]
    col = np.arange(2 * hd)[None, :]
    head = (col % hd) // GDN_DK
    e = np.where(col < hd, lane == head, lane == GDN_HEADS + head).astype(np.float32)
    return jnp.asarray(tri, BF), jnp.asarray(e, BF)


def gdn_scan(x, q, k, v, z, bg, s0, g_norm, w_out, *, c, rows, n_seq, per_chunk_state, name):
    t, hd = q.shape
    nt = t // n_seq // rows
    fuse_out = w_out is not None
    tri, e = _gdn_consts(rows, c, hd)
    ns = rows // c if per_chunk_state else 1
    row = lambda b, i: (b * nt + i, 0)
    tile = lambda n: pl.BlockSpec((rows, n), row)
    st_spec = pl.BlockSpec((ns,) + s0.shape[1:], lambda b, i: (b, 0, 0, 0))
    ins, specs = [], []
    if fuse_out:
        ins.append(x)
        specs.append(tile(x.shape[1]))
    ins += [q, k, v, z, bg, tri, e, s0, g_norm.reshape(1, GDN_DK)]
    specs += [tile(hd)] * 4 + [tile(LANES), _const_spec(tri.shape), _const_spec(e.shape), st_spec,
                               _const_spec((1, GDN_DK))]
    d_out = hd
    if fuse_out:
        ins.append(w_out)
        specs.append(_const_spec(w_out.shape))
        d_out = w_out.shape[1]
    return pl.pallas_call(
        functools.partial(_gdn_scan_body, c=c, per_chunk_state=per_chunk_state, fuse_out=fuse_out),
        grid=(n_seq, nt),
        in_specs=specs,
        out_specs=[tile(d_out), st_spec],
        out_shape=[SDS((t, d_out), F32), SDS(s0.shape, F32)],
        scratch_shapes=[pltpu.VMEM((rows, hd), F32)] * 8,
        compiler_params=_params(2),
        name=name,
    )(*ins)


def _topk_mask(sb, valid, axis):
    n = sb.shape[axis]
    idx = lax.broadcasted_iota(jnp.int32, sb.shape, axis)
    cnt = jnp.zeros(sb.shape, F32)
    for m in range(n):
        sm = lax.slice_in_dim(sb, m, m + 1, axis=axis)
        ahead = (sm > sb) | ((sm == sb) & (m < idx))
        cnt = cnt + jnp.where(ahead, 1.0, 0.0)
    return jnp.where((cnt < MOBA_TOPK) & valid, 1.0, 0.0)


def _moba_p_body(q_ref, k_ref, v_ref, o_ref, km_sc):
    i = pl.program_id(2)
    blk, dh = q_ref.shape
    nb = k_ref.shape[0] // blk
    scale = dh ** -0.5

    @pl.when(i == 0)
    def _():
        km_sc[...] = jnp.sum(k_ref[...].reshape(nb, blk, dh), axis=1) * (1.0 / blk)

    q = q_ref[...]
    qb = q.astype(BF)
    sb = _dot_hi(q, km_sc[...], dot=_dot_nt)
    bidx = lax.broadcasted_iota(jnp.int32, sb.shape, 1)
    sb = jnp.where(bidx < i, sb, NEG_INF)
    sel = _topk_mask(sb, sb > NEG_INF / 2, axis=1)

    r0 = pl.multiple_of(i * blk, blk)
    s = _dot_nt(qb, k_ref[pl.ds(r0, blk), :].astype(BF)) * scale
    rr = lax.broadcasted_iota(jnp.int32, (blk, blk), 0)
    cc = lax.broadcasted_iota(jnp.int32, (blk, blk), 1)
    s = jnp.where(cc <= rr, s, NEG_INF)
    m0 = jnp.max(s, axis=-1, keepdims=True)
    p = jnp.exp(s - m0)
    l0 = jnp.sum(p, axis=-1, keepdims=True)
    acc0 = _dot(p.astype(BF), v_ref[pl.ds(r0, blk), :].astype(BF))

    def past(n, carry):
        m_run, l_run, acc = carry
        c0 = pl.multiple_of(n * blk, blk)
        s = _dot_nt(qb, k_ref[pl.ds(c0, blk), :].astype(BF)) * scale
        sel_n = jnp.sum(jnp.where(bidx == n, sel, 0.0), axis=-1, keepdims=True)
        s = jnp.where(sel_n > 0.5, s, NEG_INF)
        m_new = jnp.maximum(m_run, jnp.max(s, axis=-1, keepdims=True))
        alpha = jnp.exp(m_run - m_new)
        p = jnp.exp(s - m_new)
        l_new = alpha * l_run + jnp.sum(p, axis=-1, keepdims=True)
        acc = alpha * acc + _dot(p.astype(BF), v_ref[pl.ds(c0, blk), :].astype(BF))
        return m_new, l_new, acc

    _, l_fin, acc = lax.fori_loop(0, i, past, (m0, l0, acc0))
    o_ref[...] = acc / l_fin


def moba_prompt_core(q, k, v, bsz, name):
    t, w = q.shape
    dh = MOBA_DH
    nh = w // dh
    length = t // bsz
    nq = length // MOBA_BLOCK
    return pl.pallas_call(
        _moba_p_body,
        grid=(bsz, nh, nq),
        in_specs=[pl.BlockSpec((MOBA_BLOCK, dh), lambda b, h, i: (b * nq + i, h)),
                  pl.BlockSpec((length, dh), lambda b, h, i: (b, h)),
                  pl.BlockSpec((length, dh), lambda b, h, i: (b, h))],
        out_specs=pl.BlockSpec((MOBA_BLOCK, dh), lambda b, h, i: (b * nq + i, h)),
        out_shape=SDS((t, w), F32),
        scratch_shapes=[pltpu.VMEM((nq, dh), F32)],
        compiler_params=_params(3),
        name=name,
    )(q, k, v)


def _moba_s_body(pt_ref, qt_ref, kn_ref, vn_ref, *refs, n_pages):
    del pt_ref
    kp = refs[:n_pages]
    vp = refs[n_pages:2 * n_pages]
    o_ref, s_sc = refs[2 * n_pages:]
    page, nh, dh = kp[0].shape[1:]
    bp = MOBA_BLOCK // page
    nb = n_pages // bp
    prow = page * nh
    scale = dh ** -0.5
    qt = qt_ref[0]
    ncol = qt.shape[-1]
    qtb = qt.astype(BF)

    km = []
    for n in range(nb):
        tot = kp[n * bp][0].sum(axis=0)
        for j in range(1, bp):
            tot = tot + kp[n * bp + j][0].sum(axis=0)
        km.append(tot * (1.0 / MOBA_BLOCK))
    km2 = jnp.concatenate(km, axis=0)
    sbm = _dot_hi(km2, qt)
    rh = lax.broadcasted_iota(jnp.int32, sbm.shape, 0) % nh
    ch = lax.broadcasted_iota(jnp.int32, sbm.shape, 1) % nh
    sb = jnp.sum(jnp.where(rh == ch, sbm, 0.0).reshape(nb, nh, ncol), axis=1)
    sel = _topk_mask(sb, jnp.full(sb.shape, True), axis=0)

    rh = lax.broadcasted_iota(jnp.int32, (prow, ncol), 0) % nh
    ch = lax.broadcasted_iota(jnp.int32, (prow, ncol), 1) % nh
    same_head = rh == ch

    n_own = kn_ref.shape[1]
    ro = lax.broadcasted_iota(jnp.int32, (n_own, ncol), 0)
    co = lax.broadcasted_iota(jnp.int32, (n_own, ncol), 1)
    own_ok = (ro % nh == co % nh) & (ro // nh <= co // nh)
    s_own = jnp.where(own_ok, _dot(kn_ref[0].astype(BF), qtb) * scale, NEG_INF)
    mx = jnp.max(s_own, axis=0, keepdims=True)

    for p in range(n_pages):
        k2 = kp[p][0].reshape(prow, dh).astype(BF)
        s = _dot(k2, qtb) * scale
        ok = same_head & (sel[p // bp:p // bp + 1, :] > 0.5)
        s = jnp.where(ok, s, NEG_INF)
        s_sc[p * prow:(p + 1) * prow, :] = s
        mx = jnp.maximum(mx, jnp.max(s, axis=0, keepdims=True))

    pe = jnp.exp(s_own - mx)
    l = jnp.sum(pe, axis=0, keepdims=True)
    acc = _dot_tn(pe.astype(BF), vn_ref[0].astype(BF))
    for p in range(n_pages):
        pe = jnp.exp(s_sc[p * prow:(p + 1) * prow, :] - mx)
        l = l + jnp.sum(pe, axis=0, keepdims=True)
        acc = acc + _dot_tn(pe.astype(BF), vp[p][0].reshape(prow, dh).astype(BF))
    l_pad = jnp.concatenate([l, jnp.ones((1, LANES - ncol), F32)], axis=1)
    l_col = jnp.broadcast_to(l_pad, (LANES, LANES)).T[:ncol, :1]
    o_ref[0] = acc / l_col


def moba_sample_core(qt, kn, vn, pool_k, pool_v, page_table, name):
    bsz, dh, ncol = qt.shape
    n_pages = page_table.shape[1]
    page, nh = pool_k.shape[1:3]
    n_own = kn.shape[1]

    def page_spec(p):
        return pl.BlockSpec((1, page, nh, dh), lambda b, pt: (pt[b, p], 0, 0, 0))

    grid_spec = pltpu.PrefetchScalarGridSpec(
        num_scalar_prefetch=1,
        grid=(bsz,),
        in_specs=[pl.BlockSpec((1, dh, ncol), lambda b, pt: (b, 0, 0)),
                  pl.BlockSpec((1, n_own, dh), lambda b, pt: (b, 0, 0)),
                  pl.BlockSpec((1, n_own, dh), lambda b, pt: (b, 0, 0))]
                 + [page_spec(p) for p in range(n_pages)] * 2,
        out_specs=pl.BlockSpec((1, ncol, dh), lambda b, pt: (b, 0, 0)),
        scratch_shapes=[pltpu.VMEM((n_pages * page * nh, ncol), F32)],
    )
    return pl.pallas_call(
        functools.partial(_moba_s_body, n_pages=n_pages),
        grid_spec=grid_spec,
        out_shape=SDS((bsz, ncol, dh), F32),
        compiler_params=_params(1),
        name=name,
    )(page_table, qt, kn, vn, *([pool_k] * n_pages), *([pool_v] * n_pages))


def _to_pm(a):
    b, l, c = a.shape
    return jnp.transpose(a, (1, 0, 2)).reshape(l * b, c)


def _from_pm(a, bsz):
    n, c = a.shape
    return jnp.transpose(a.reshape(n // bsz, bsz, c), (1, 0, 2))


def _pad_lanes(vec, offset):
    out = jnp.zeros((1, LANES), F32)
    return lax.dynamic_update_slice(out, vec.reshape(1, -1).astype(F32), (0, offset))


def kernel(x_prompt, x_sample, state_a_conv, state_b_conv, state_b_rec, cache_c_k, cache_c_v, cache_mem_k, cache_mem_v, page_table, mem_prompt, norm_mix, norm_mem, norm_xattn, norm_ffn, norm_final, a_w_in, a_w_conv, a_w_out, b_w_in, b_w_conv, b_a_log, b_dt_bias, b_norm, b_w_out, c_w_qkv, c_w_out, x_w_q, x_w_kv, x_w_o, f_w_up, f_w_down):
    bp, seq, d = x_prompt.shape
    bs, dseq, _ = x_sample.shape
    depth = norm_mix.shape[0]
    n_mixers = 3
    tm = 512
    ts = bs * dseq

    bf = lambda w: w.astype(BF)
    a_w_in, a_w_out, b_w_out, c_w_qkv, c_w_out = map(bf, (a_w_in, a_w_out, b_w_out, c_w_qkv, c_w_out))
    x_w_q, x_w_kv, x_w_o, f_w_up, f_w_down = map(bf, (x_w_q, x_w_kv, x_w_o, f_w_up, f_w_down))
    cc = b_w_conv.shape[-1]
    hd = GDN_HEADS * GDN_DK
    b_w_qkvz = bf(b_w_in[:, :, :cc + hd])
    b_w_ba = bf(jnp.pad(b_w_in[:, :, cc + hd:], ((0, 0), (0, 0), (0, LANES - 2 * GDN_HEADS))))

    xp = x_prompt.reshape(bp * seq, d)
    xs = _to_pm(x_sample)

    mem_len = mem_prompt.shape[1]
    mk32, mv32, mkbf, mvbf = memory_kv_all(mem_prompt.reshape(bp * mem_len, d), norm_mem, x_w_kv, tm)

    a_p, a_s, bc_p, bc_s, br_p, br_s = [], [], [], [], [], []
    ck_p, cv_p, ck_s, cv_s = [], [], [], []
    for i in range(depth):
        kind, j = i % n_mixers, i // n_mixers
        tag = f"l{i}"
        if kind == 0:
            xp, st = mixer_a_prompt(xp, norm_mix[i], a_w_in[j], a_w_conv[j], a_w_out[j], bp, tm, tag + "_mixa_p")
            a_p.append(st)
            st_pm = _to_pm(state_a_conv[j])
            xs, nst = mixer_a_sample(xs, st_pm, norm_mix[i], a_w_in[j], a_w_conv[j], a_w_out[j], bs, tag + "_mixa_s")
            a_s.append(_from_pm(nst, bs))
        elif kind == 1:
            alog = _pad_lanes(b_a_log[j], GDN_HEADS)
            dtb = _pad_lanes(b_dt_bias[j], GDN_HEADS)
            q, k, v, z, bg, st = gdn_proj_prompt(xp, norm_mix[i], b_w_qkvz[j], b_w_ba[j], b_w_conv[j], alog, dtb,
                                                 bp, 256, tag + "_gdnproj_p")
            bc_p.append(st)
            s0 = jnp.zeros((bp, GDN_HEADS, GDN_DK, GDN_DK), F32)
            xp, s_fin = gdn_scan(xp, q, k, v, z, bg, s0, b_norm[j], b_w_out[j], c=math.gcd(seq, GDN_CHUNK),
                                 rows=256, n_seq=bp, per_chunk_state=False, name=tag + "_gdnscan_p")
            br_p.append(s_fin)
            st_pm = _to_pm(state_b_conv[j])
            q, k, v, z, bg, nst = gdn_proj_sample(xs, st_pm, norm_mix[i], b_w_qkvz[j], b_w_ba[j], b_w_conv[j],
                                                  alog, dtb, bs, tag + "_gdnproj_s")
            bc_s.append(_from_pm(nst, bs))
            cpad = SUBLANES
            padded = lambda a: jnp.pad(_from_pm(a, bs), ((0, 0), (0, cpad - dseq), (0, 0))).reshape(bs * cpad, -1)
            nbatch = 8
            y, s_fin = gdn_scan(None, padded(q), padded(k), padded(v), padded(z), padded(bg), state_b_rec[j],
                                b_norm[j], None, c=cpad, rows=nbatch * cpad, n_seq=bs // nbatch,
                                per_chunk_state=True, name=tag + "_gdnscan_s")
            br_s.append(s_fin)
            y = _to_pm(y.reshape(bs, cpad, hd)[:, :dseq])
            xs = mm_res(y, b_w_out[j], xs, ts, tag + "_gdnout_s")
        else:
            q, k, v = norm_mm(xp, norm_mix[i], c_w_qkv[j], 3, tm, tag + "_mobaqkv_p")
            ck_p.append(k.reshape(bp, seq, MOBA_HEADS, MOBA_DH))
            cv_p.append(v.reshape(bp, seq, MOBA_HEADS, MOBA_DH))
            o = moba_prompt_core(q, k, v, bp, tag + "_moba_p")
            xp = mm_res(o, c_w_out[j], xp, tm, tag + "_mobaout_p")
            q, k, v = norm_mm(xs, norm_mix[i], c_w_qkv[j], 3, ts, tag + "_mobaqkv_s")
            kn = _from_pm(k, bs)
            vn = _from_pm(v, bs)
            ck_s.append(kn.reshape(bs, dseq, MOBA_HEADS, MOBA_DH))
            cv_s.append(vn.reshape(bs, dseq, MOBA_HEADS, MOBA_DH))
            qt = jnp.transpose(_from_pm(q, bs).reshape(bs, dseq * MOBA_HEADS, MOBA_DH), (0, 2, 1))
            rows_kv = lambda a: a.reshape(bs, dseq * MOBA_HEADS, MOBA_DH)
            o = moba_sample_core(qt, rows_kv(kn), rows_kv(vn), cache_c_k[j], cache_c_v[j], page_table, tag + "_moba_s")
            xs = mm_res(_to_pm(o.reshape(bs, dseq, MOBA_HEADS * MOBA_DH)), c_w_out[j], xs, ts, tag + "_mobaout_s")

        xp = xattn_prompt(xp, norm_xattn[i], x_w_q[i], mkbf[i], mvbf[i], x_w_o[i], bp, tm, tag + "_xattn_p")
        (qs,) = norm_mm(xs, norm_xattn[i], x_w_q[i], 1, ts, tag + "_xattnq_s")
        dh = d // XA_HEADS
        qt = jnp.transpose(_from_pm(qs, bs).reshape(bs, dseq * XA_HEADS, dh), (0, 2, 1))
        o = xattn_sample_core(qt, cache_mem_k, cache_mem_v, i, 4, tag + "_xattn_s")
        xs = mm_res(_to_pm(o.reshape(bs, dseq, d)), x_w_o[i], xs, ts, tag + "_xattnout_s")

        last = i == depth - 1
        xp = ffn(xp, norm_ffn[i], f_w_up[i], f_w_down[i], norm_final, last, tm, tag + "_ffn_p")
        xs = ffn(xs, norm_ffn[i], f_w_up[i], f_w_down[i], norm_final, last, ts, tag + "_ffn_s")

    mem_shape = (depth, bp, mem_len, XA_HEADS, d // XA_HEADS)
    return (xp.reshape(bp, seq, d), _from_pm(xs, bs),
            jnp.stack(a_p), jnp.stack(a_s),
            jnp.stack(bc_p), jnp.stack(bc_s),
            jnp.stack(br_p), jnp.stack(br_s),
            jnp.stack(ck_p), jnp.stack(cv_p), jnp.stack(ck_s), jnp.stack(cv_s),
            mk32.reshape(mem_shape), mv32.reshape(mem_shape))
```

```python
import functools
import math

import numpy as np
import jax
import jax.numpy as jnp
from jax import lax
from jax.experimental import pallas as pl
from jax.experimental.pallas import tpu as pltpu

F32 = jnp.float32
BF = jnp.bfloat16
SDS = jax.ShapeDtypeStruct

EPS = 1e-6
NEG_INF = -1e30
LANES = 128
SUBLANES = 8
VMEM_LIMIT = 56 * 1024 * 1024

GDN_HEADS = 8
GDN_DK = 128
GDN_CHUNK = 64
MOBA_HEADS = 8
MOBA_DH = 128
MOBA_BLOCK = 256
MOBA_TOPK = 3
PAGE_SIZE = 128
XA_HEADS = 4


def _params(n_axes):
    return pltpu.CompilerParams(dimension_semantics=("arbitrary",) * n_axes,
                                vmem_limit_bytes=VMEM_LIMIT)


def _const_spec(shape):
    nd = len(shape)
    return pl.BlockSpec(shape, lambda *_: (0,) * nd, pipeline_mode=pl.Buffered(1))


def _whole_spec(shape):
    nd = len(shape)
    return pl.BlockSpec(shape, lambda *_: (0,) * nd)


def _dot(a, b):
    return jnp.dot(a, b, preferred_element_type=F32)


def _dot_nt(a, b):
    return lax.dot_general(a, b, (((1,), (1,)), ((), ())), preferred_element_type=F32)


def _dot_tn(a, b):
    return lax.dot_general(a, b, (((0,), (0,)), ((), ())), preferred_element_type=F32)


def _split3(a):
    a0 = a.astype(BF)
    r = a - a0.astype(F32)
    a1 = r.astype(BF)
    a2 = (r - a1.astype(F32)).astype(BF)
    return a0, a1, a2


def _dot_exact_rhs(a, b_bf, dot=_dot):
    a0, a1, a2 = _split3(a)
    return dot(a0, b_bf) + dot(a1, b_bf) + dot(a2, b_bf)


def _dot_hi(a, b, dot=_dot):
    a0, a1, _ = _split3(a)
    b0, b1, _ = _split3(b)
    return dot(a0, b0) + (dot(a0, b1) + dot(a1, b0))


def _rms(x, g):
    ms = jnp.mean(x * x, axis=-1, keepdims=True)
    return x * lax.rsqrt(ms + EPS) * g


def _silu(x):
    return x * jax.nn.sigmoid(x)


def _softplus(x):
    return jnp.maximum(x, 0.0) + jnp.log1p(jnp.exp(-jnp.abs(x)))


def _norm_mm_body(x_ref, g_ref, w_ref, *o_refs):
    h = _rms(x_ref[...], g_ref[...]).astype(BF)
    n = o_refs[0].shape[-1]
    for j, o_ref in enumerate(o_refs):
        o_ref[...] = _dot(h, w_ref[:, j * n:(j + 1) * n])


def norm_mm(x, g, w, n_out, tm, name):
    t, k = x.shape
    n = w.shape[1] // n_out
    return pl.pallas_call(
        _norm_mm_body,
        grid=(t // tm,),
        in_specs=[pl.BlockSpec((tm, k), lambda i: (i, 0)), _const_spec((1, k)), _const_spec(w.shape)],
        out_specs=[pl.BlockSpec((tm, n), lambda i: (i, 0))] * n_out,
        out_shape=[SDS((t, n), F32)] * n_out,
        compiler_params=_params(1),
        name=name,
    )(x, g.reshape(1, k), w)


def _mm_res_body(a_ref, w_ref, r_ref, o_ref):
    o_ref[...] = r_ref[...] + _dot(a_ref[...].astype(BF), w_ref[...])


def mm_res(a, w, res, tm, name):
    t, k = a.shape
    n = w.shape[1]
    return pl.pallas_call(
        _mm_res_body,
        grid=(t // tm,),
        in_specs=[pl.BlockSpec((tm, k), lambda i: (i, 0)), _const_spec(w.shape),
                  pl.BlockSpec((tm, n), lambda i: (i, 0))],
        out_specs=pl.BlockSpec((tm, n), lambda i: (i, 0)),
        out_shape=SDS((t, n), F32),
        compiler_params=_params(1),
        name=name,
    )(a, w, res)


def _memkv_body(x_ref, g_ref, w_ref, k32_ref, v32_ref, kbf_ref, vbf_ref):
    h = _rms(x_ref[...], g_ref[0]).astype(BF)
    d = k32_ref.shape[-1]
    k = _dot(h, w_ref[0, :, :d])
    v = _dot(h, w_ref[0, :, d:])
    k32_ref[0] = k
    v32_ref[0] = v
    kbf_ref[0] = k.astype(BF)
    vbf_ref[0] = v.astype(BF)


def memory_kv_all(mem2d, norm_mem, w_kv, tm):
    depth, d, _ = w_kv.shape
    t = mem2d.shape[0]
    out_spec = pl.BlockSpec((1, tm, d), lambda l, i: (l, i, 0))
    return pl.pallas_call(
        _memkv_body,
        grid=(depth, t // tm),
        in_specs=[pl.BlockSpec((tm, d), lambda l, i: (i, 0)),
                  pl.BlockSpec((1, 1, d), lambda l, i: (l, 0, 0)),
                  pl.BlockSpec((1, d, 2 * d), lambda l, i: (l, 0, 0))],
        out_specs=[out_spec] * 4,
        out_shape=[SDS((depth, t, d), F32)] * 2 + [SDS((depth, t, d), BF)] * 2,
        compiler_params=_params(2),
        name="memory_kv",
    )(mem2d, norm_mem.reshape(depth, 1, d), w_kv)


def _ffn_body(x_ref, g_ref, wu_ref, wd_ref, gf_ref, o_ref, *, ck, final):
    x = x_ref[...]
    h = _rms(x, g_ref[...]).astype(BF)
    dff = wd_ref.shape[0]
    acc = x
    for c0 in range(0, dff, ck):
        gate = _dot(h, wu_ref[:, c0:c0 + ck])
        up = _dot(h, wu_ref[:, dff + c0:dff + c0 + ck])
        a = (_silu(gate) * up).astype(BF)
        acc = acc + _dot(a, wd_ref[c0:c0 + ck, :])
    if final:
        acc = _rms(acc, gf_ref[...])
    o_ref[...] = acc


def ffn(x, g, w_up, w_down, g_final, final, tm, name):
    t, d = x.shape
    body = functools.partial(_ffn_body, ck=2 * LANES, final=final)
    return pl.pallas_call(
        body,
        grid=(t // tm,),
        in_specs=[pl.BlockSpec((tm, d), lambda i: (i, 0)), _const_spec((1, d)),
                  _const_spec(w_up.shape), _const_spec(w_down.shape), _const_spec((1, d))],
        out_specs=pl.BlockSpec((tm, d), lambda i: (i, 0)),
        out_shape=SDS((t, d), F32),
        compiler_params=_params(1),
        name=name,
    )(x, g.reshape(1, d), w_up, w_down, g_final.reshape(1, d))


def _xattn_p_body(x_ref, g_ref, wq_ref, k_ref, v_ref, wo_ref, o_ref, *, nh):
    x = x_ref[...]
    d = x.shape[-1]
    dh = d // nh
    h = _rms(x, g_ref[...]).astype(BF)
    q = (_dot(h, wq_ref[...]) * dh ** -0.5).astype(BF)
    outs = []
    for hh in range(nh):
        cs = slice(hh * dh, (hh + 1) * dh)
        s = _dot_nt(q[:, cs], k_ref[0, :, cs])
        p = jnp.exp(s - jnp.max(s, axis=-1, keepdims=True))
        l = jnp.sum(p, axis=-1, keepdims=True)
        outs.append((_dot(p.astype(BF), v_ref[0, :, cs]) / l).astype(BF))
    o_ref[...] = x + _dot(jnp.concatenate(outs, axis=-1), wo_ref[...])


def xattn_prompt(x, g, w_q, k_bf, v_bf, w_o, bsz, tm, name):
    t, d = x.shape
    nt = t // bsz // tm
    m_len = k_bf.shape[0] // bsz
    kv = lambda a: a.reshape(bsz, m_len, d)
    row = lambda b, i: (b * nt + i, 0)
    return pl.pallas_call(
        functools.partial(_xattn_p_body, nh=XA_HEADS),
        grid=(bsz, nt),
        in_specs=[pl.BlockSpec((tm, d), row), _const_spec((1, d)), _const_spec(w_q.shape),
                  pl.BlockSpec((1, m_len, d), lambda b, i: (b, 0, 0)),
                  pl.BlockSpec((1, m_len, d), lambda b, i: (b, 0, 0)),
                  _const_spec(w_o.shape)],
        out_specs=pl.BlockSpec((tm, d), row),
        out_shape=SDS((t, d), F32),
        compiler_params=_params(2),
        name=name,
    )(x, g.reshape(1, d), w_q, kv(k_bf), kv(v_bf), w_o)


def _xattn_s_body(qt_ref, k_ref, v_ref, o_ref, *, nb, nh):
    m_len, _, dh = k_ref.shape[2:]
    rows = m_len * nh
    ncol = qt_ref.shape[-1]
    rh = lax.broadcasted_iota(jnp.int32, (rows, ncol), 0) % nh
    ch = lax.broadcasted_iota(jnp.int32, (rows, ncol), 1) % nh
    same_head = rh == ch
    for j in range(nb):
        k2 = k_ref[0, j].reshape(rows, dh).astype(BF)
        v2 = v_ref[0, j].reshape(rows, dh).astype(BF)
        s = _dot(k2, qt_ref[j].astype(BF)) * dh ** -0.5
        s = jnp.where(same_head, s, NEG_INF)
        p = jnp.exp(s - jnp.max(s, axis=0, keepdims=True))
        p = p / jnp.sum(p, axis=0, keepdims=True)
        o_ref[j] = _dot_tn(p.astype(BF), v2)


def xattn_sample_core(qt, cache_k, cache_v, layer, nb, name):
    bsz, dh, ncol = qt.shape
    m_len, nh = cache_k.shape[2:4]
    kv_spec = pl.BlockSpec((1, nb, m_len, nh, dh), lambda i: (layer, i, 0, 0, 0))
    return pl.pallas_call(
        functools.partial(_xattn_s_body, nb=nb, nh=nh),
        grid=(bsz // nb,),
        in_specs=[pl.BlockSpec((nb, dh, ncol), lambda i: (i, 0, 0)), kv_spec, kv_spec],
        out_specs=pl.BlockSpec((nb, ncol, dh), lambda i: (i, 0, 0)),
        out_shape=SDS((bsz, ncol, dh), F32),
        compiler_params=_params(1),
        name=name,
    )(qt, cache_k, cache_v)


def _mixa_p_body(x_ref, g_ref, win_ref, wc_ref, wout_ref, o_ref, st_ref, zbuf, *, nt):
    i = pl.program_id(1)
    tm, d = x_ref.shape
    width = wc_ref.shape[0]

    @pl.when(i == 0)
    def _():
        zbuf[0:SUBLANES, :] = jnp.zeros((SUBLANES, d), F32)

    x = x_ref[...]
    h = _rms(x, g_ref[...]).astype(BF)
    gate_b = _dot(h, win_ref[:, 0:d])
    zn = _dot(h, win_ref[:, d:2 * d]) * _dot(h, win_ref[:, 2 * d:3 * d])
    zbuf[SUBLANES:SUBLANES + tm, :] = zn
    conv = wc_ref[width - 1:width, :] * zn
    for j in range(width - 1):
        off = SUBLANES - (width - 1) + j
        conv = conv + wc_ref[j:j + 1, :] * zbuf[off:off + tm, :]
    o_ref[...] = x + _dot((gate_b * conv).astype(BF), wout_ref[...])
    zbuf[0:SUBLANES, :] = zbuf[tm:tm + SUBLANES, :]

    @pl.when(i == nt - 1)
    def _():
        st_ref[0] = zbuf[SUBLANES - (width - 1):SUBLANES, :]


def mixer_a_prompt(x, g, w_in, w_conv, w_out, bsz, tm, name):
    t, d = x.shape
    nt = t // bsz // tm
    width = w_conv.shape[0]
    row = lambda b, i: (b * nt + i, 0)
    return pl.pallas_call(
        functools.partial(_mixa_p_body, nt=nt),
        grid=(bsz, nt),
        in_specs=[pl.BlockSpec((tm, d), row), _const_spec((1, d)), _const_spec(w_in.shape),
                  _const_spec(w_conv.shape), _const_spec(w_out.shape)],
        out_specs=[pl.BlockSpec((tm, d), row), pl.BlockSpec((1, width - 1, d), lambda b, i: (b, 0, 0))],
        out_shape=[SDS((t, d), F32), SDS((bsz, width - 1, d), F32)],
        scratch_shapes=[pltpu.VMEM((tm + SUBLANES, d), F32)],
        compiler_params=_params(2),
        name=name,
    )(x, g.reshape(1, d), w_in, w_conv, w_out)


def _posmajor_conv(state, z, wc_ref, bsz):
    width = wc_ref.shape[0]
    n = z.shape[0]
    zpad = jnp.concatenate([state, z], axis=0)
    conv = wc_ref[0:1, :] * zpad[0:n]
    for j in range(1, width):
        conv = conv + wc_ref[j:j + 1, :] * zpad[j * bsz:j * bsz + n]
    return conv, zpad[n:]


def _mixa_s_body(x_ref, st_ref, g_ref, win_ref, wc_ref, wout_ref, o_ref, nst_ref, *, bsz):
    x = x_ref[...]
    d = x.shape[-1]
    h = _rms(x, g_ref[...]).astype(BF)
    gate_b = _dot(h, win_ref[:, 0:d])
    zn = _dot(h, win_ref[:, d:2 * d]) * _dot(h, win_ref[:, 2 * d:3 * d])
    conv, new_state = _posmajor_conv(st_ref[...], zn, wc_ref, bsz)
    o_ref[...] = x + _dot((gate_b * conv).astype(BF), wout_ref[...])
    nst_ref[...] = new_state


def mixer_a_sample(x, state_pm, g, w_in, w_conv, w_out, bsz, name):
    t, d = x.shape
    return pl.pallas_call(
        functools.partial(_mixa_s_body, bsz=bsz),
        grid=(1,),
        in_specs=[_const_spec(x.shape), _const_spec(state_pm.shape), _const_spec((1, d)),
                  _const_spec(w_in.shape), _const_spec(w_conv.shape), _const_spec(w_out.shape)],
        out_specs=[_whole_spec(x.shape), _whole_spec(state_pm.shape)],
        out_shape=[SDS(x.shape, F32), SDS(state_pm.shape, F32)],
        compiler_params=_params(1),
        name=name,
    )(x, state_pm, g.reshape(1, d), w_in, w_conv, w_out)


def _gdn_activations(conv, ba, alog_ref, dtb_ref, q_ref, k_ref, v_ref, bg_ref):
    nh, dk = GDN_HEADS, GDN_DK
    qkv = _silu(conv)
    for hh in range(nh):
        for (ref, base, scale) in ((q_ref, 0, dk ** -0.5), (k_ref, nh * dk, 1.0)):
            t = qkv[:, base + hh * dk:base + (hh + 1) * dk]
            t = t * lax.rsqrt(jnp.sum(t * t, axis=-1, keepdims=True) + EPS)
            ref[:, hh * dk:(hh + 1) * dk] = t * scale if scale != 1.0 else t
    v_ref[...] = qkv[:, 2 * nh * dk:]
    lane = lax.broadcasted_iota(jnp.int32, ba.shape, 1)
    beta = jax.nn.sigmoid(ba)
    g = -jnp.exp(alog_ref[...]) * _softplus(ba + dtb_ref[...])
    bg_ref[...] = jnp.where(lane < nh, beta, jnp.where(lane < 2 * nh, g, 0.0))


def _gdn_proj_p_body(x_ref, g_ref, w_ref, wba_ref, wc_ref, alog_ref, dtb_ref,
                     q_ref, k_ref, v_ref, z_ref, bg_ref, st_ref, cbuf, *, nt):
    i = pl.program_id(1)
    tm = x_ref.shape[0]
    width, cc = wc_ref.shape

    @pl.when(i == 0)
    def _():
        cbuf[0:SUBLANES, :] = jnp.zeros((SUBLANES, cc), F32)

    h = _rms(x_ref[...], g_ref[...]).astype(BF)
    qkv = _dot(h, w_ref[:, :cc])
    z_ref[...] = _dot(h, w_ref[:, cc:])
    ba = _dot(h, wba_ref[...])
    cbuf[SUBLANES:SUBLANES + tm, :] = qkv
    conv = wc_ref[width - 1:width, :] * qkv
    for j in range(width - 1):
        off = SUBLANES - (width - 1) + j
        conv = conv + wc_ref[j:j + 1, :] * cbuf[off:off + tm, :]
    _gdn_activations(conv, ba, alog_ref, dtb_ref, q_ref, k_ref, v_ref, bg_ref)
    cbuf[0:SUBLANES, :] = cbuf[tm:tm + SUBLANES, :]

    @pl.when(i == nt - 1)
    def _():
        st_ref[0] = cbuf[SUBLANES - (width - 1):SUBLANES, :]


def gdn_proj_prompt(x, g, w_qkvz, w_ba, w_conv, alog, dtb, bsz, tm, name):
    t, d = x.shape
    nt = t // bsz // tm
    width, cc = w_conv.shape
    hd = w_qkvz.shape[1] - cc
    row = lambda b, i: (b * nt + i, 0)
    tile = lambda n: pl.BlockSpec((tm, n), row)
    return pl.pallas_call(
        functools.partial(_gdn_proj_p_body, nt=nt),
        grid=(bsz, nt),
        in_specs=[tile(d), _const_spec((1, d)), _const_spec(w_qkvz.shape), _const_spec(w_ba.shape),
                  _const_spec(w_conv.shape), _const_spec(alog.shape), _const_spec(dtb.shape)],
        out_specs=[tile(hd), tile(hd), tile(hd), tile(hd), tile(LANES),
                   pl.BlockSpec((1, width - 1, cc), lambda b, i: (b, 0, 0))],
        out_shape=[SDS((t, hd), F32)] * 4 + [SDS((t, LANES), F32), SDS((bsz, width - 1, cc), F32)],
        scratch_shapes=[pltpu.VMEM((tm + SUBLANES, cc), F32)],
        compiler_params=_params(2),
        name=name,
    )(x, g.reshape(1, d), w_qkvz, w_ba, w_conv, alog, dtb)


def _gdn_proj_s_body(x_ref, st_ref, g_ref, w_ref, wba_ref, wc_ref, alog_ref, dtb_ref,
                     q_ref, k_ref, v_ref, z_ref, bg_ref, nst_ref, *, bsz):
    cc = wc_ref.shape[1]
    h = _rms(x_ref[...], g_ref[...]).astype(BF)
    qkv = _dot(h, w_ref[:, :cc])
    z_ref[...] = _dot(h, w_ref[:, cc:])
    ba = _dot(h, wba_ref[...])
    conv, new_state = _posmajor_conv(st_ref[...], qkv, wc_ref, bsz)
    _gdn_activations(conv, ba, alog_ref, dtb_ref, q_ref, k_ref, v_ref, bg_ref)
    nst_ref[...] = new_state


def gdn_proj_sample(x, state_pm, g, w_qkvz, w_ba, w_conv, alog, dtb, bsz, name):
    t, d = x.shape
    cc = w_conv.shape[1]
    hd = w_qkvz.shape[1] - cc
    ins = (x, state_pm, g.reshape(1, d), w_qkvz, w_ba, w_conv, alog, dtb)
    outs = [SDS((t, hd), F32)] * 4 + [SDS((t, LANES), F32), SDS(state_pm.shape, F32)]
    return pl.pallas_call(
        functools.partial(_gdn_proj_s_body, bsz=bsz),
        grid=(1,),
        in_specs=[_const_spec(a.shape) for a in ins],
        out_specs=[_whole_spec(o.shape) for o in outs],
        out_shape=outs,
        compiler_params=_params(1),
        name=name,
    )(*ins)


def _pad_transpose(a):
    c = a.shape[0]
    if c < LANES:
        a = jnp.concatenate([a, jnp.zeros((LANES - c, a.shape[1]), a.dtype)], axis=0)
    return a.T


def _gdn_scan_body(*refs, c, per_chunk_state, fuse_out):
    if fuse_out:
        x_ref, refs = refs[0], refs[1:]
    (q_ref, k_ref, v_ref, z_ref, bg_ref, tri_ref, e_ref, s0_ref, gn_ref) = refs[:9]
    refs = refs[9:]
    if fuse_out:
        wout_ref, refs = refs[0], refs[1:]
    o_ref, s_ref, qg_sc, kdec_sc, rk_sc, rv_sc, gam_sc, beta_sc, egl_sc, o_sc = refs

    nh, dk = GDN_HEADS, GDN_DK
    rows, hd = q_ref.shape
    n_chunks = rows // c
    n_double = int(math.log2(c)) - 1

    if not per_chunk_state:
        @pl.when(pl.program_id(1) == 0)
        def _():
            s_ref[...] = s0_ref[...]

    bg = bg_ref[...]
    lane = lax.broadcasted_iota(jnp.int32, bg.shape, 1)
    gam = _dot_exact_rhs(bg, tri_ref[...], dot=lambda a, b: _dot(b, a))
    y = jnp.where(lane < nh, bg, gam)
    yb = _dot_exact_rhs(y, e_ref[...])
    beta_b = yb[:, :hd]
    gam_b = yb[:, hd:]
    gl_b = jnp.broadcast_to(gam_b.reshape(n_chunks, c, hd)[:, c - 1:c, :], (n_chunks, c, hd)).reshape(rows, hd)
    eg = jnp.exp(gam_b)
    k = k_ref[...]
    qg_sc[...] = q_ref[...] * eg
    rk_sc[...] = beta_b * eg * k
    rv_sc[...] = beta_b * v_ref[...]
    kdec_sc[...] = k * jnp.exp(gl_b - gam_b)
    gam_sc[...] = gam_b
    beta_sc[...] = beta_b
    egl_sc[...] = jnp.exp(gl_b)

    ii = lax.broadcasted_iota(jnp.int32, (c, c), 0)
    jj = lax.broadcasted_iota(jnp.int32, (c, c), 1)
    incl = ii >= jj
    strict = ii > jj

    chains = [(slice(ci * c, (ci + 1) * c), slice(hh * dk, (hh + 1) * dk))
              for ci in range(n_chunks) for hh in range(nh)]
    kq, decay, pm = [], [], []
    for rs, hc in chains:
        kc = k_ref[rs, hc]
        kq.append(_dot_nt(jnp.concatenate([kc, q_ref[rs, hc]], axis=0).astype(BF), kc.astype(BF)))
    for (rs, hc), kq_i in zip(chains, kq):
        gcol = gam_sc[rs, hc]
        diff = gcol[:, :c] - _pad_transpose(gcol)[:c, :c]
        dec = jnp.where(incl, jnp.exp(jnp.where(incl, diff, 0.0)), 0.0)
        decay.append(dec)
        pm.append(jnp.where(strict, -(beta_sc[rs, hc][:, :c] * kq_i[:c] * dec), 0.0))
    mk = pm
    for _ in range(n_double):
        mk = [_dot(a.astype(BF), a.astype(BF)) for a in mk]
        pm = [p + a + _dot(p.astype(BF), a.astype(BF)) for p, a in zip(pm, mk)]
    sol, lhs2 = [], []
    for (rs, hc), p, kq_i, dec in zip(chains, pm, kq, decay):
        rhs = jnp.concatenate([rv_sc[rs, hc], rk_sc[rs, hc]], axis=1)
        sol.append(rhs + _dot(p.astype(BF), rhs.astype(BF)))
        kdt = _pad_transpose(kdec_sc[rs, hc])[:, :c]
        lhs2.append(jnp.concatenate([kq_i[c:] * dec, kdt], axis=0).astype(BF))

    def advance(idxs, states):
        t1 = []
        for idx, s in zip(idxs, states):
            rs, hc = chains[idx]
            t1.append(_dot(jnp.concatenate([sol[idx][:, dk:], qg_sc[rs, hc]], axis=0).astype(BF), s.astype(BF)))
        t2 = [_dot(lhs2[idx], (sol[idx][:, :dk] - t[:c]).astype(BF)) for idx, t in zip(idxs, t1)]
        new_states = []
        for idx, s, ta, tb in zip(idxs, states, t1, t2):
            rs, hc = chains[idx]
            o_sc[rs, hc] = ta[c:] + tb[:c]
            egl = jnp.broadcast_to(egl_sc[rs.start:rs.start + 1, hc], (dk, dk))
            new_states.append(egl * s + tb[c:])
        return new_states

    if per_chunk_state:
        idxs = list(range(len(chains)))
        new_states = advance(idxs, [s0_ref[idx // nh, idx % nh] for idx in idxs])
        for idx, s_new in zip(idxs, new_states):
            s_ref[idx // nh, idx % nh] = s_new
    else:
        state = [s_ref[0, hh] for hh in range(nh)]
        for ci in range(n_chunks):
            state = advance([ci * nh + hh for hh in range(nh)], state)
        for hh in range(nh):
            s_ref[0, hh] = state[hh]

    outs = []
    for hh in range(nh):
        hc = slice(hh * dk, (hh + 1) * dk)
        outs.append(_rms(o_sc[:, hc], gn_ref[...]) * _silu(z_ref[:, hc]))
    yv = jnp.concatenate(outs, axis=-1)
    if fuse_out:
        o_ref[...] = x_ref[...] + _dot(yv.astype(BF), wout_ref[...])
    else:
        o_ref[...] = yv


def _gdn_consts(rows, c, hd):
    r = np.arange(rows)
    tri = ((r[:, None] >= r[None, :]) & (r[:, None] // c == r[None, :] // c)).astype(np.float32)
    lane = np.arange(LANES)[:, None]
    col = np.arange(2 * hd)[None, :]
    head = (col % hd) // GDN_DK
    e = np.where(col < hd, lane == head, lane == GDN_HEADS + head).astype(np.float32)
    return jnp.asarray(tri, BF), jnp.asarray(e, BF)


def gdn_scan(x, q, k, v, z, bg, s0, g_norm, w_out, *, c, rows, n_seq, per_chunk_state, name):
    t, hd = q.shape
    nt = t // n_seq // rows
    fuse_out = w_out is not None
    tri, e = _gdn_consts(rows, c, hd)
    ns = rows // c if per_chunk_state else 1
    row = lambda b, i: (b * nt + i, 0)
    tile = lambda n: pl.BlockSpec((rows, n), row)
    st_spec = pl.BlockSpec((ns,) + s0.shape[1:], lambda b, i: (b, 0, 0, 0))
    ins, specs = [], []
    if fuse_out:
        ins.append(x)
        specs.append(tile(x.shape[1]))
    ins += [q, k, v, z, bg, tri, e, s0, g_norm.reshape(1, GDN_DK)]
    specs += [tile(hd)] * 4 + [tile(LANES), _const_spec(tri.shape), _const_spec(e.shape), st_spec,
                               _const_spec((1, GDN_DK))]
    d_out = hd
    if fuse_out:
        ins.append(w_out)
        specs.append(_const_spec(w_out.shape))
        d_out = w_out.shape[1]
    return pl.pallas_call(
        functools.partial(_gdn_scan_body, c=c, per_chunk_state=per_chunk_state, fuse_out=fuse_out),
        grid=(n_seq, nt),
        in_specs=specs,
        out_specs=[tile(d_out), st_spec],
        out_shape=[SDS((t, d_out), F32), SDS(s0.shape, F32)],
        scratch_shapes=[pltpu.VMEM((rows, hd), F32)] * 8,
        compiler_params=_params(2),
        name=name,
    )(*ins)


def _topk_mask(sb, valid, axis):
    n = sb.shape[axis]
    idx = lax.broadcasted_iota(jnp.int32, sb.shape, axis)
    cnt = jnp.zeros(sb.shape, F32)
    for m in range(n):
        sm = lax.slice_in_dim(sb, m, m + 1, axis=axis)
        ahead = (sm > sb) | ((sm == sb) & (m < idx))
        cnt = cnt + jnp.where(ahead, 1.0, 0.0)
    return jnp.where((cnt < MOBA_TOPK) & valid, 1.0, 0.0)


def _moba_p_body(q_ref, k_ref, v_ref, ex_ref, o_ref):
    length, dh = k_ref.shape
    blk = MOBA_BLOCK
    nb = length // blk
    scale = dh ** -0.5
    k = k_ref[...]
    kb = k.astype(BF)
    vb = v_ref[...].astype(BF)
    km = jnp.sum(k.reshape(nb, blk, dh), axis=1) * (1.0 / blk)
    brow = lax.broadcasted_iota(jnp.int32, (nb, blk), 0)
    rr = lax.broadcasted_iota(jnp.int32, (blk, blk), 0)
    cc = lax.broadcasted_iota(jnp.int32, (blk, blk), 1)
    causal = cc <= rr
    for i in range(nb):
        q = q_ref[i * blk:(i + 1) * blk, :]
        n_keys = (i + 1) * blk
        s = _dot_nt(q.astype(BF), kb[:n_keys]) * scale
        s_own = jnp.where(causal, s[:, i * blk:], NEG_INF)
        if i > 0:
            sbt = jnp.where(brow < i, _dot_hi(km, q, dot=_dot_nt), NEG_INF)
            selt = _topk_mask(sbt, sbt > NEG_INF / 2, axis=0)
            keep = _dot_tn(selt.astype(BF), ex_ref[:, :i * blk]) > 0.5
            s = jnp.concatenate([jnp.where(keep, s[:, :i * blk], NEG_INF), s_own], axis=-1)
        else:
            s = s_own
        pe = jnp.exp(s - jnp.max(s, axis=-1, keepdims=True))
        l = jnp.sum(pe, axis=-1, keepdims=True)
        o_ref[i * blk:(i + 1) * blk, :] = _dot(pe.astype(BF), vb[:n_keys]) / l


def moba_prompt_core(q, k, v, bsz, name):
    t, w = q.shape
    dh = MOBA_DH
    nh = w // dh
    length = t // bsz
    nb = length // MOBA_BLOCK
    expand = jnp.asarray(np.arange(nb)[:, None] == np.arange(length)[None, :] // MOBA_BLOCK, BF)
    spec = pl.BlockSpec((length, dh), lambda b, h: (b, h))
    return pl.pallas_call(
        _moba_p_body,
        grid=(bsz, nh),
        in_specs=[spec, spec, spec, _const_spec(expand.shape)],
        out_specs=spec,
        out_shape=SDS((t, w), F32),
        compiler_params=_params(2),
        name=name,
    )(q, k, v, expand)


def _lane_group_reduce(row, op, ncol):
    a = jnp.broadcast_to(row, (SUBLANES, LANES))
    shift = ncol
    while shift < LANES:
        a = op(a, pltpu.roll(a, shift, 1))
        shift *= 2
    return a[:1]


def _moba_s_body(pt_ref, qt_ref, hb_ref, kn_ref, vn_ref, *refs, n_pages, ncol):
    del pt_ref
    kp = refs[:n_pages]
    vp = refs[n_pages:2 * n_pages]
    o_ref, s_sc = refs[2 * n_pages:]
    page, nh, dh = kp[0].shape[1:]
    bp = MOBA_BLOCK // page
    nb = n_pages // bp
    prow = page * nh
    pack = LANES // ncol
    qt = qt_ref[0]
    lane = lax.broadcasted_iota(jnp.int32, (1, LANES), 1)
    group = lane // ncol
    qs = qt * dh ** -0.5
    rhs = jnp.concatenate([jnp.where(group == g, qs, 0.0).astype(BF) for g in range(pack)], axis=0)

    km = []
    for n in range(nb):
        tot = kp[n * bp][0].sum(axis=0)
        for j in range(1, bp):
            tot = tot + kp[n * bp + j][0].sum(axis=0)
        km.append(tot * (1.0 / MOBA_BLOCK))
    km2 = jnp.concatenate(km, axis=0)
    sbm = _dot_hi(km2, qt)
    rh = lax.broadcasted_iota(jnp.int32, sbm.shape, 0) % nh
    ch = lax.broadcasted_iota(jnp.int32, sbm.shape, 1) % ncol % nh
    sb = jnp.sum(jnp.where(rh == ch, sbm, 0.0).reshape(nb, nh, LANES), axis=1)
    sel = _topk_mask(sb, jnp.full(sb.shape, True), axis=0)
    sel_bias = jnp.where(sel > 0.5, 0.0, NEG_INF)

    n_own = kn_ref.shape[1]
    ro = lax.broadcasted_iota(jnp.int32, (n_own, LANES), 0)
    co = lax.broadcasted_iota(jnp.int32, (n_own, LANES), 1)
    own_ok = (ro % nh == co % nh) & (ro // nh <= co // nh) & (co < ncol)
    s_own = jnp.where(own_ok, _dot(kn_ref[0].astype(BF), qs.astype(BF)), NEG_INF)
    mx = jnp.max(s_own, axis=0, keepdims=True)

    n_groups = n_pages // pack
    for j in range(n_groups):
        pages = range(j * pack, (j + 1) * pack)
        k4 = jnp.concatenate([kp[p][0].reshape(prow, dh).astype(BF) for p in pages], axis=1)
        bias = sel_bias[(j * pack) // bp:(j * pack) // bp + 1, :]
        for g in range(1, pack):
            blk = (j * pack + g) // bp
            bias = jnp.where(group >= g, sel_bias[blk:blk + 1, :], bias)
        s = _dot(k4, rhs) + (hb_ref[...] + bias)
        s_sc[j * prow:(j + 1) * prow, :] = s
        mx = jnp.maximum(mx, jnp.max(s, axis=0, keepdims=True))
    mx = _lane_group_reduce(mx, jnp.maximum, ncol)

    pe = jnp.exp(s_own - mx)
    l = jnp.sum(pe, axis=0, keepdims=True)
    acc = _dot_tn(pe.astype(BF), vn_ref[0].astype(BF))[:ncol]
    for j in range(n_groups):
        pages = range(j * pack, (j + 1) * pack)
        pe = jnp.exp(s_sc[j * prow:(j + 1) * prow, :] - mx)
        l = l + jnp.sum(pe, axis=0, keepdims=True)
        v4 = jnp.concatenate([vp[p][0].reshape(prow, dh).astype(BF) for p in pages], axis=1)
        r = _dot_tn(pe.astype(BF), v4)
        for g in range(pack):
            acc = acc + r[g * ncol:(g + 1) * ncol, g * dh:(g + 1) * dh]
    l = _lane_group_reduce(l, jnp.add, ncol)
    l_col = jnp.broadcast_to(l, (LANES, LANES)).T[:ncol, :1]
    o_ref[0] = acc / l_col


def moba_sample_core(qt, kn, vn, pool_k, pool_v, page_table, name):
    bsz, dh, ncol = qt.shape
    n_pages = page_table.shape[1]
    page, nh = pool_k.shape[1:3]
    n_own = kn.shape[1]
    pack = LANES // ncol
    qt = jnp.tile(qt, (1, 1, pack))
    r = np.arange(page * nh)[:, None]
    c = np.arange(LANES)[None, :]
    head_bias = jnp.asarray(np.where(r % nh == c % ncol % nh, 0.0, NEG_INF), F32)

    def page_spec(p):
        return pl.BlockSpec((1, page, nh, dh), lambda b, pt: (pt[b, p], 0, 0, 0))

    grid_spec = pltpu.PrefetchScalarGridSpec(
        num_scalar_prefetch=1,
        grid=(bsz,),
        in_specs=[pl.BlockSpec((1, dh, LANES), lambda b, pt: (b, 0, 0)),
                  pl.BlockSpec(head_bias.shape, lambda b, pt: (0, 0)),
                  pl.BlockSpec((1, n_own, dh), lambda b, pt: (b, 0, 0)),
                  pl.BlockSpec((1, n_own, dh), lambda b, pt: (b, 0, 0))]
                 + [page_spec(p) for p in range(n_pages)] * 2,
        out_specs=pl.BlockSpec((1, ncol, dh), lambda b, pt: (b, 0, 0)),
        scratch_shapes=[pltpu.VMEM((n_pages // pack * page * nh, LANES), F32)],
    )
    return pl.pallas_call(
        functools.partial(_moba_s_body, n_pages=n_pages, ncol=ncol),
        grid_spec=grid_spec,
        out_shape=SDS((bsz, ncol, dh), F32),
        compiler_params=_params(1),
        name=name,
    )(page_table, qt, head_bias, kn, vn, *([pool_k] * n_pages), *([pool_v] * n_pages))


def _to_pm(a):
    b, l, c = a.shape
    return jnp.transpose(a, (1, 0, 2)).reshape(l * b, c)


def _from_pm(a, bsz):
    n, c = a.shape
    return jnp.transpose(a.reshape(n // bsz, bsz, c), (1, 0, 2))


def _pad_lanes(vec, offset):
    out = jnp.zeros((1, LANES), F32)
    return lax.dynamic_update_slice(out, vec.reshape(1, -1).astype(F32), (0, offset))


def kernel(x_prompt, x_sample, state_a_conv, state_b_conv, state_b_rec, cache_c_k, cache_c_v, cache_mem_k, cache_mem_v, page_table, mem_prompt, norm_mix, norm_mem, norm_xattn, norm_ffn, norm_final, a_w_in, a_w_conv, a_w_out, b_w_in, b_w_conv, b_a_log, b_dt_bias, b_norm, b_w_out, c_w_qkv, c_w_out, x_w_q, x_w_kv, x_w_o, f_w_up, f_w_down):
    bp, seq, d = x_prompt.shape
    bs, dseq, _ = x_sample.shape
    depth = norm_mix.shape[0]
    n_mixers = 3
    tm = 512
    ts = bs * dseq

    bf = lambda w: w.astype(BF)
    a_w_in, a_w_out, b_w_out, c_w_qkv, c_w_out = map(bf, (a_w_in, a_w_out, b_w_out, c_w_qkv, c_w_out))
    x_w_q, x_w_kv, x_w_o, f_w_up, f_w_down = map(bf, (x_w_q, x_w_kv, x_w_o, f_w_up, f_w_down))
    cc = b_w_conv.shape[-1]
    hd = GDN_HEADS * GDN_DK
    b_w_qkvz = bf(b_w_in[:, :, :cc + hd])
    b_w_ba = bf(jnp.pad(b_w_in[:, :, cc + hd:], ((0, 0), (0, 0), (0, LANES - 2 * GDN_HEADS))))

    xp = x_prompt.reshape(bp * seq, d)
    xs = _to_pm(x_sample)

    mem_len = mem_prompt.shape[1]
    mk32, mv32, mkbf, mvbf = memory_kv_all(mem_prompt.reshape(bp * mem_len, d), norm_mem, x_w_kv, tm)

    a_p, a_s, bc_p, bc_s, br_p, br_s = [], [], [], [], [], []
    ck_p, cv_p, ck_s, cv_s = [], [], [], []
    for i in range(depth):
        kind, j = i % n_mixers, i // n_mixers
        tag = f"l{i}"
        if kind == 0:
            xp, st = mixer_a_prompt(xp, norm_mix[i], a_w_in[j], a_w_conv[j], a_w_out[j], bp, tm, tag + "_mixa_p")
            a_p.append(st)
            st_pm = _to_pm(state_a_conv[j])
            xs, nst = mixer_a_sample(xs, st_pm, norm_mix[i], a_w_in[j], a_w_conv[j], a_w_out[j], bs, tag + "_mixa_s")
            a_s.append(_from_pm(nst, bs))
        elif kind == 1:
            alog = _pad_lanes(b_a_log[j], GDN_HEADS)
            dtb = _pad_lanes(b_dt_bias[j], GDN_HEADS)
            q, k, v, z, bg, st = gdn_proj_prompt(xp, norm_mix[i], b_w_qkvz[j], b_w_ba[j], b_w_conv[j], alog, dtb,
                                                 bp, 256, tag + "_gdnproj_p")
            bc_p.append(st)
            s0 = jnp.zeros((bp, GDN_HEADS, GDN_DK, GDN_DK), F32)
            xp, s_fin = gdn_scan(xp, q, k, v, z, bg, s0, b_norm[j], b_w_out[j], c=math.gcd(seq, GDN_CHUNK),
                                 rows=256, n_seq=bp, per_chunk_state=False, name=tag + "_gdnscan_p")
            br_p.append(s_fin)
            st_pm = _to_pm(state_b_conv[j])
            q, k, v, z, bg, nst = gdn_proj_sample(xs, st_pm, norm_mix[i], b_w_qkvz[j], b_w_ba[j], b_w_conv[j],
                                                  alog, dtb, bs, tag + "_gdnproj_s")
            bc_s.append(_from_pm(nst, bs))
            cpad = SUBLANES
            padded = lambda a: jnp.pad(_from_pm(a, bs), ((0, 0), (0, cpad - dseq), (0, 0))).reshape(bs * cpad, -1)
            nbatch = 8
            y, s_fin = gdn_scan(None, padded(q), padded(k), padded(v), padded(z), padded(bg), state_b_rec[j],
                                b_norm[j], None, c=cpad, rows=nbatch * cpad, n_seq=bs // nbatch,
                                per_chunk_state=True, name=tag + "_gdnscan_s")
            br_s.append(s_fin)
            y = _to_pm(y.reshape(bs, cpad, hd)[:, :dseq])
            xs = mm_res(y, b_w_out[j], xs, ts, tag + "_gdnout_s")
        else:
            q, k, v = norm_mm(xp, norm_mix[i], c_w_qkv[j], 3, tm, tag + "_mobaqkv_p")
            ck_p.append(k.reshape(bp, seq, MOBA_HEADS, MOBA_DH))
            cv_p.append(v.reshape(bp, seq, MOBA_HEADS, MOBA_DH))
            o = moba_prompt_core(q, k, v, bp, tag + "_moba_p")
            xp = mm_res(o, c_w_out[j], xp, tm, tag + "_mobaout_p")
            q, k, v = norm_mm(xs, norm_mix[i], c_w_qkv[j], 3, ts, tag + "_mobaqkv_s")
            kn = _from_pm(k, bs)
            vn = _from_pm(v, bs)
            ck_s.append(kn.reshape(bs, dseq, MOBA_HEADS, MOBA_DH))
            cv_s.append(vn.reshape(bs, dseq, MOBA_HEADS, MOBA_DH))
            qt = jnp.transpose(_from_pm(q, bs).reshape(bs, dseq * MOBA_HEADS, MOBA_DH), (0, 2, 1))
            rows_kv = lambda a: a.reshape(bs, dseq * MOBA_HEADS, MOBA_DH)
            o = moba_sample_core(qt, rows_kv(kn), rows_kv(vn), cache_c_k[j], cache_c_v[j], page_table, tag + "_moba_s")
            xs = mm_res(_to_pm(o.reshape(bs, dseq, MOBA_HEADS * MOBA_DH)), c_w_out[j], xs, ts, tag + "_mobaout_s")

        xp = xattn_prompt(xp, norm_xattn[i], x_w_q[i], mkbf[i], mvbf[i], x_w_o[i], bp, tm, tag + "_xattn_p")
        (qs,) = norm_mm(xs, norm_xattn[i], x_w_q[i], 1, ts, tag + "_xattnq_s")
        dh = d // XA_HEADS
        qt = jnp.transpose(_from_pm(qs, bs).reshape(bs, dseq * XA_HEADS, dh), (0, 2, 1))
        o = xattn_sample_core(qt, cache_mem_k, cache_mem_v, i, 4, tag + "_xattn_s")
        xs = mm_res(_to_pm(o.reshape(bs, dseq, d)), x_w_o[i], xs, ts, tag + "_xattnout_s")

        last = i == depth - 1
        xp = ffn(xp, norm_ffn[i], f_w_up[i], f_w_down[i], norm_final, last, tm, tag + "_ffn_p")
        xs = ffn(xs, norm_ffn[i], f_w_up[i], f_w_down[i], norm_final, last, ts, tag + "_ffn_s")

    mem_shape = (depth, bp, mem_len, XA_HEADS, d // XA_HEADS)
    return (xp.reshape(bp, seq, d), _from_pm(xs, bs),
            jnp.stack(a_p), jnp.stack(a_s),
            jnp.stack(bc_p), jnp.stack(bc_s),
            jnp.stack(br_p), jnp.stack(br_s),
            jnp.stack(ck_p), jnp.stack(cv_p), jnp.stack(ck_s), jnp.stack(cv_s),
            mk32.reshape(mem_shape), mv32.reshape(mem_shape))
```

```python
import functools
import math

import numpy as np
import jax
import jax.numpy as jnp
from jax import lax
from jax.experimental import pallas as pl
from jax.experimental.pallas import tpu as pltpu

F32 = jnp.float32
BF = jnp.bfloat16
SDS = jax.ShapeDtypeStruct

EPS = 1e-6
NEG_INF = -1e30
LANES = 128
SUBLANES = 8
VMEM_LIMIT = 56 * 1024 * 1024

GDN_HEADS = 8
GDN_DK = 128
GDN_CHUNK = 64
MOBA_HEADS = 8
MOBA_DH = 128
MOBA_BLOCK = 256
MOBA_TOPK = 3
PAGE_SIZE = 128
XA_HEADS = 4


def _params(n_axes):
    return pltpu.CompilerParams(dimension_semantics=("arbitrary",) * n_axes,
                                vmem_limit_bytes=VMEM_LIMIT)


def _const_spec(shape):
    nd = len(shape)
    return pl.BlockSpec(shape, lambda *_: (0,) * nd, pipeline_mode=pl.Buffered(1))


class _Layer:
    def __init__(self, arr, j):
        self.arr, self.j = arr, j

    @property
    def shape(self):
        return self.arr.shape[1:]


def _wspec(w):
    if not isinstance(w, _Layer):
        return _const_spec(w.shape)
    nd = len(w.shape)
    j = w.j
    return pl.BlockSpec((None,) + w.shape, lambda *_: (j,) + (0,) * nd, pipeline_mode=pl.Buffered(1))


def _warr(w):
    return w.arr if isinstance(w, _Layer) else w


def _whole_spec(shape):
    nd = len(shape)
    return pl.BlockSpec(shape, lambda *_: (0,) * nd)


def _dot(a, b):
    return jnp.dot(a, b, preferred_element_type=F32)


def _dot_nt(a, b):
    return lax.dot_general(a, b, (((1,), (1,)), ((), ())), preferred_element_type=F32)


def _dot_tn(a, b):
    return lax.dot_general(a, b, (((0,), (0,)), ((), ())), preferred_element_type=F32)


def _split3(a):
    a0 = a.astype(BF)
    r = a - a0.astype(F32)
    a1 = r.astype(BF)
    a2 = (r - a1.astype(F32)).astype(BF)
    return a0, a1, a2


def _dot_exact_rhs(a, b_bf, dot=_dot):
    a0, a1, a2 = _split3(a)
    return dot(a0, b_bf) + dot(a1, b_bf) + dot(a2, b_bf)


def _dot_hi(a, b, dot=_dot):
    a0, a1, _ = _split3(a)
    b0, b1, _ = _split3(b)
    return dot(a0, b0) + (dot(a0, b1) + dot(a1, b0))


def _rms(x, g):
    ms = jnp.mean(x * x, axis=-1, keepdims=True)
    return x * lax.rsqrt(ms + EPS) * g


def _silu(x):
    return x * jax.nn.sigmoid(x)


def _softplus(x):
    return jnp.maximum(x, 0.0) + jnp.log1p(jnp.exp(-jnp.abs(x)))


def _norm_mm_body(x_ref, g_ref, w_ref, *o_refs):
    h = _rms(x_ref[...], g_ref[...]).astype(BF)
    n = o_refs[0].shape[-1]
    for j, o_ref in enumerate(o_refs):
        o_ref[...] = _dot(h, w_ref[:, j * n:(j + 1) * n])


def norm_mm(x, g, w, n_out, tm, name):
    t, k = x.shape
    n = w.shape[1] // n_out
    return pl.pallas_call(
        _norm_mm_body,
        grid=(t // tm,),
        in_specs=[pl.BlockSpec((tm, k), lambda i: (i, 0)), _const_spec((1, k)), _wspec(w)],
        out_specs=[pl.BlockSpec((tm, n), lambda i: (i, 0))] * n_out,
        out_shape=[SDS((t, n), F32)] * n_out,
        compiler_params=_params(1),
        name=name,
    )(x, g.reshape(1, k), _warr(w))


def _mm_res_body(a_ref, w_ref, r_ref, o_ref):
    o_ref[...] = r_ref[...] + _dot(a_ref[...].astype(BF), w_ref[...])


def mm_res(a, w, res, tm, name):
    t, k = a.shape
    n = w.shape[1]
    return pl.pallas_call(
        _mm_res_body,
        grid=(t // tm,),
        in_specs=[pl.BlockSpec((tm, k), lambda i: (i, 0)), _wspec(w),
                  pl.BlockSpec((tm, n), lambda i: (i, 0))],
        out_specs=pl.BlockSpec((tm, n), lambda i: (i, 0)),
        out_shape=SDS((t, n), F32),
        compiler_params=_params(1),
        name=name,
    )(a, _warr(w), res)


def _memkv_body(x_ref, g_ref, w_ref, k32_ref, v32_ref, kbf_ref, vbf_ref):
    h = _rms(x_ref[...], g_ref[0]).astype(BF)
    d = kbf_ref.shape[-1]
    nh, dh = k32_ref.shape[2:]
    for j, (split_ref, bf_ref) in enumerate(((k32_ref, kbf_ref), (v32_ref, vbf_ref))):
        r = _dot(h, w_ref[0, :, j * d:(j + 1) * d])
        bf_ref[0] = r.astype(BF)
        for hh in range(nh):
            split_ref[0, :, hh, :] = r[:, hh * dh:(hh + 1) * dh]


def memory_kv_all(mem2d, norm_mem, w_kv, nh, tm):
    depth, d, _ = w_kv.shape
    t = mem2d.shape[0]
    flat = pl.BlockSpec((1, tm, d), lambda l, i: (l, i, 0))
    split = pl.BlockSpec((1, tm, nh, d // nh), lambda l, i: (l, i, 0, 0))
    return pl.pallas_call(
        _memkv_body,
        grid=(depth, t // tm),
        in_specs=[pl.BlockSpec((tm, d), lambda l, i: (i, 0)),
                  pl.BlockSpec((1, 1, d), lambda l, i: (l, 0, 0)),
                  pl.BlockSpec((1, d, 2 * d), lambda l, i: (l, 0, 0))],
        out_specs=[split, split, flat, flat],
        out_shape=[SDS((depth, t, nh, d // nh), F32)] * 2 + [SDS((depth, t, d), BF)] * 2,
        compiler_params=_params(2),
        name="memory_kv",
    )(mem2d, norm_mem.reshape(depth, 1, d), w_kv)


def _ffn_body(x_ref, g_ref, wu_ref, wd_ref, gf_ref, o_ref, *, ck, final):
    x = x_ref[...]
    h = _rms(x, g_ref[...]).astype(BF)
    dff = wd_ref.shape[0]
    acc = x
    for c0 in range(0, dff, ck):
        gate = _dot(h, wu_ref[:, c0:c0 + ck])
        up = _dot(h, wu_ref[:, dff + c0:dff + c0 + ck])
        a = (_silu(gate) * up).astype(BF)
        acc = acc + _dot(a, wd_ref[c0:c0 + ck, :])
    if final:
        acc = _rms(acc, gf_ref[...])
    o_ref[...] = acc


def ffn(x, g, w_up, w_down, g_final, final, tm, name):
    t, d = x.shape
    body = functools.partial(_ffn_body, ck=2 * LANES, final=final)
    return pl.pallas_call(
        body,
        grid=(t // tm,),
        in_specs=[pl.BlockSpec((tm, d), lambda i: (i, 0)), _const_spec((1, d)),
                  _wspec(w_up), _wspec(w_down), _const_spec((1, d))],
        out_specs=pl.BlockSpec((tm, d), lambda i: (i, 0)),
        out_shape=SDS((t, d), F32),
        compiler_params=_params(1),
        name=name,
    )(x, g.reshape(1, d), _warr(w_up), _warr(w_down), g_final.reshape(1, d))


def _xattn_p_body(x_ref, g_ref, wq_ref, k_ref, v_ref, wo_ref, o_ref, *, nh):
    x = x_ref[...]
    d = x.shape[-1]
    dh = d // nh
    h = _rms(x, g_ref[...]).astype(BF)
    q = (_dot(h, wq_ref[...]) * dh ** -0.5).astype(BF)
    outs = []
    for hh in range(nh):
        cs = slice(hh * dh, (hh + 1) * dh)
        s = _dot_nt(q[:, cs], k_ref[0, :, cs])
        p = jnp.exp(s - jnp.max(s, axis=-1, keepdims=True))
        l = jnp.sum(p, axis=-1, keepdims=True)
        outs.append((_dot(p.astype(BF), v_ref[0, :, cs]) / l).astype(BF))
    o_ref[...] = x + _dot(jnp.concatenate(outs, axis=-1), wo_ref[...])


def xattn_prompt(x, g, w_q, k_bf, v_bf, layer, w_o, bsz, tm, name):
    t, d = x.shape
    nt = t // bsz // tm
    m_len = k_bf.shape[1] // bsz
    kv = lambda a: a.reshape(a.shape[0] * bsz, m_len, d)
    row = lambda b, i: (b * nt + i, 0)
    return pl.pallas_call(
        functools.partial(_xattn_p_body, nh=XA_HEADS),
        grid=(bsz, nt),
        in_specs=[pl.BlockSpec((tm, d), row), _const_spec((1, d)), _wspec(w_q),
                  pl.BlockSpec((1, m_len, d), lambda b, i: (layer * bsz + b, 0, 0)),
                  pl.BlockSpec((1, m_len, d), lambda b, i: (layer * bsz + b, 0, 0)),
                  _wspec(w_o)],
        out_specs=pl.BlockSpec((tm, d), row),
        out_shape=SDS((t, d), F32),
        compiler_params=_params(2),
        name=name,
    )(x, g.reshape(1, d), _warr(w_q), kv(k_bf), kv(v_bf), _warr(w_o))


def _xattn_s_body(qt_ref, k_ref, v_ref, o_ref, *, nb, nh):
    m_len, _, dh = k_ref.shape[2:]
    rows = m_len * nh
    ncol = qt_ref.shape[-1]
    rh = lax.broadcasted_iota(jnp.int32, (rows, ncol), 0) % nh
    ch = lax.broadcasted_iota(jnp.int32, (rows, ncol), 1) % nh
    same_head = rh == ch
    head_bias = jnp.where(same_head, 0.0, NEG_INF)
    s = [_dot(k_ref[0, j].reshape(rows, dh).astype(BF), (qt_ref[j] * dh ** -0.5).astype(BF)) + head_bias
         for j in range(nb)]
    p = [jnp.exp(a - jnp.max(a, axis=0, keepdims=True)) for a in s]
    p = [(a / jnp.sum(a, axis=0, keepdims=True)).astype(BF) for a in p]
    for j in range(nb):
        o_ref[j] = _dot_tn(p[j], v_ref[0, j].reshape(rows, dh).astype(BF))


def xattn_sample_core(qt, cache_k, cache_v, layer, nb, name):
    bsz, dh, ncol = qt.shape
    m_len, nh = cache_k.shape[2:4]
    kv_spec = pl.BlockSpec((1, nb, m_len, nh, dh), lambda i: (layer, i, 0, 0, 0))
    return pl.pallas_call(
        functools.partial(_xattn_s_body, nb=nb, nh=nh),
        grid=(bsz // nb,),
        in_specs=[pl.BlockSpec((nb, dh, ncol), lambda i: (i, 0, 0)), kv_spec, kv_spec],
        out_specs=pl.BlockSpec((nb, ncol, dh), lambda i: (i, 0, 0)),
        out_shape=SDS((bsz, ncol, dh), F32),
        compiler_params=_params(1),
        name=name,
    )(qt, cache_k, cache_v)


def _mixa_p_body(x_ref, g_ref, win_ref, wc_ref, wout_ref, o_ref, st_ref, zbuf, *, nt):
    i = pl.program_id(1)
    tm, d = x_ref.shape
    width = wc_ref.shape[0]

    @pl.when(i == 0)
    def _():
        zbuf[0:SUBLANES, :] = jnp.zeros((SUBLANES, d), F32)

    x = x_ref[...]
    h = _rms(x, g_ref[...]).astype(BF)
    gate_b = _dot(h, win_ref[:, 0:d])
    zn = _dot(h, win_ref[:, d:2 * d]) * _dot(h, win_ref[:, 2 * d:3 * d])
    zbuf[SUBLANES:SUBLANES + tm, :] = zn
    conv = wc_ref[width - 1:width, :] * zn
    for j in range(width - 1):
        off = SUBLANES - (width - 1) + j
        conv = conv + wc_ref[j:j + 1, :] * zbuf[off:off + tm, :]
    o_ref[...] = x + _dot((gate_b * conv).astype(BF), wout_ref[...])
    zbuf[0:SUBLANES, :] = zbuf[tm:tm + SUBLANES, :]

    @pl.when(i == nt - 1)
    def _():
        st_ref[0] = zbuf[SUBLANES - (width - 1):SUBLANES, :]


def mixer_a_prompt(x, g, w_in, w_conv, w_out, bsz, tm, name):
    t, d = x.shape
    nt = t // bsz // tm
    width = w_conv.shape[0]
    row = lambda b, i: (b * nt + i, 0)
    return pl.pallas_call(
        functools.partial(_mixa_p_body, nt=nt),
        grid=(bsz, nt),
        in_specs=[pl.BlockSpec((tm, d), row), _const_spec((1, d)), _wspec(w_in),
                  _const_spec(w_conv.shape), _wspec(w_out)],
        out_specs=[pl.BlockSpec((tm, d), row), pl.BlockSpec((1, width - 1, d), lambda b, i: (b, 0, 0))],
        out_shape=[SDS((t, d), F32), SDS((bsz, width - 1, d), F32)],
        scratch_shapes=[pltpu.VMEM((tm + SUBLANES, d), F32)],
        compiler_params=_params(2),
        name=name,
    )(x, g.reshape(1, d), _warr(w_in), w_conv, _warr(w_out))


def _posmajor_conv(state, z, wc_ref, bsz):
    width = wc_ref.shape[0]
    n = z.shape[0]
    zpad = jnp.concatenate([state, z], axis=0)
    conv = wc_ref[0:1, :] * zpad[0:n]
    for j in range(1, width):
        conv = conv + wc_ref[j:j + 1, :] * zpad[j * bsz:j * bsz + n]
    return conv, zpad[n:]


def _mixa_s_body(x_ref, st_ref, g_ref, win_ref, wc_ref, wout_ref, o_ref, nst_ref, *, bsz):
    x = x_ref[...]
    d = x.shape[-1]
    h = _rms(x, g_ref[...]).astype(BF)
    gate_b = _dot(h, win_ref[:, 0:d])
    zn = _dot(h, win_ref[:, d:2 * d]) * _dot(h, win_ref[:, 2 * d:3 * d])
    conv, new_state = _posmajor_conv(st_ref[...], zn, wc_ref, bsz)
    o_ref[...] = x + _dot((gate_b * conv).astype(BF), wout_ref[...])
    nst_ref[...] = new_state


def mixer_a_sample(x, state_pm, g, w_in, w_conv, w_out, bsz, name):
    t, d = x.shape
    return pl.pallas_call(
        functools.partial(_mixa_s_body, bsz=bsz),
        grid=(1,),
        in_specs=[_const_spec(x.shape), _const_spec(state_pm.shape), _const_spec((1, d)),
                  _wspec(w_in), _const_spec(w_conv.shape), _wspec(w_out)],
        out_specs=[_whole_spec(x.shape), _whole_spec(state_pm.shape)],
        out_shape=[SDS(x.shape, F32), SDS(state_pm.shape, F32)],
        compiler_params=_params(1),
        name=name,
    )(x, state_pm, g.reshape(1, d), _warr(w_in), w_conv, _warr(w_out))


def _gdn_activations(conv, ba, alog_ref, dtb_ref, q_ref, k_ref, v_ref, bg_ref):
    nh, dk = GDN_HEADS, GDN_DK
    qkv = _silu(conv)
    for hh in range(nh):
        for (ref, base, scale) in ((q_ref, 0, dk ** -0.5), (k_ref, nh * dk, 1.0)):
            t = qkv[:, base + hh * dk:base + (hh + 1) * dk]
            t = t * lax.rsqrt(jnp.sum(t * t, axis=-1, keepdims=True) + EPS)
            ref[:, hh * dk:(hh + 1) * dk] = t * scale if scale != 1.0 else t
    v_ref[...] = qkv[:, 2 * nh * dk:]
    lane = lax.broadcasted_iota(jnp.int32, ba.shape, 1)
    beta = jax.nn.sigmoid(ba)
    g = -jnp.exp(alog_ref[...]) * _softplus(ba + dtb_ref[...])
    bg_ref[...] = jnp.where(lane < nh, beta, jnp.where(lane < 2 * nh, g, 0.0))


def _gdn_proj_p_body(x_ref, g_ref, w_ref, wba_ref, wc_ref, alog_ref, dtb_ref,
                     q_ref, k_ref, v_ref, z_ref, bg_ref, st_ref, cbuf, *, nt):
    i = pl.program_id(1)
    tm = x_ref.shape[0]
    width, cc = wc_ref.shape

    @pl.when(i == 0)
    def _():
        cbuf[0:SUBLANES, :] = jnp.zeros((SUBLANES, cc), F32)

    h = _rms(x_ref[...], g_ref[...]).astype(BF)
    qkv = _dot(h, w_ref[:, :cc])
    z_ref[...] = _dot(h, w_ref[:, cc:])
    ba = _dot(h, wba_ref[...])
    cbuf[SUBLANES:SUBLANES + tm, :] = qkv
    conv = wc_ref[width - 1:width, :] * qkv
    for j in range(width - 1):
        off = SUBLANES - (width - 1) + j
        conv = conv + wc_ref[j:j + 1, :] * cbuf[off:off + tm, :]
    _gdn_activations(conv, ba, alog_ref, dtb_ref, q_ref, k_ref, v_ref, bg_ref)
    cbuf[0:SUBLANES, :] = cbuf[tm:tm + SUBLANES, :]

    @pl.when(i == nt - 1)
    def _():
        st_ref[0] = cbuf[SUBLANES - (width - 1):SUBLANES, :]


def gdn_proj_prompt(x, g, w_qkvz, w_ba, w_conv, alog, dtb, bsz, tm, name):
    t, d = x.shape
    nt = t // bsz // tm
    width, cc = w_conv.shape
    hd = w_qkvz.shape[1] - cc
    row = lambda b, i: (b * nt + i, 0)
    tile = lambda n: pl.BlockSpec((tm, n), row)
    return pl.pallas_call(
        functools.partial(_gdn_proj_p_body, nt=nt),
        grid=(bsz, nt),
        in_specs=[tile(d), _const_spec((1, d)), _wspec(w_qkvz), _wspec(w_ba),
                  _const_spec(w_conv.shape), _const_spec(alog.shape), _const_spec(dtb.shape)],
        out_specs=[tile(hd), tile(hd), tile(hd), tile(hd), tile(LANES),
                   pl.BlockSpec((1, width - 1, cc), lambda b, i: (b, 0, 0))],
        out_shape=[SDS((t, hd), F32)] * 4 + [SDS((t, LANES), F32), SDS((bsz, width - 1, cc), F32)],
        scratch_shapes=[pltpu.VMEM((tm + SUBLANES, cc), F32)],
        compiler_params=_params(2),
        name=name,
    )(x, g.reshape(1, d), _warr(w_qkvz), _warr(w_ba), w_conv, alog, dtb)


def _gdn_proj_s_body(x_ref, st_ref, g_ref, w_ref, wba_ref, wc_ref, alog_ref, dtb_ref,
                     q_ref, k_ref, v_ref, z_ref, bg_ref, nst_ref, *, bsz):
    cc = wc_ref.shape[1]
    h = _rms(x_ref[...], g_ref[...]).astype(BF)
    qkv = _dot(h, w_ref[:, :cc])
    z_ref[...] = _dot(h, w_ref[:, cc:])
    ba = _dot(h, wba_ref[...])
    conv, new_state = _posmajor_conv(st_ref[...], qkv, wc_ref, bsz)
    _gdn_activations(conv, ba, alog_ref, dtb_ref, q_ref, k_ref, v_ref, bg_ref)
    nst_ref[...] = new_state


def gdn_proj_sample(x, state_pm, g, w_qkvz, w_ba, w_conv, alog, dtb, bsz, name):
    t, d = x.shape
    cc = w_conv.shape[1]
    hd = w_qkvz.shape[1] - cc
    ins = (x, state_pm, g.reshape(1, d), w_qkvz, w_ba, w_conv, alog, dtb)
    outs = [SDS((t, hd), F32)] * 4 + [SDS((t, LANES), F32), SDS(state_pm.shape, F32)]
    return pl.pallas_call(
        functools.partial(_gdn_proj_s_body, bsz=bsz),
        grid=(1,),
        in_specs=[_wspec(a) for a in ins],
        out_specs=[_whole_spec(o.shape) for o in outs],
        out_shape=outs,
        compiler_params=_params(1),
        name=name,
    )(*[_warr(a) for a in ins])


def _pad_transpose(a):
    c = a.shape[0]
    if c < LANES:
        a = jnp.concatenate([a, jnp.zeros((LANES - c, a.shape[1]), a.dtype)], axis=0)
    return a.T


def _gdn_scan_body(*refs, c, per_chunk_state, fuse_out):
    if fuse_out:
        x_ref, refs = refs[0], refs[1:]
    (q_ref, k_ref, v_ref, z_ref, bg_ref, tri_ref, e_ref, s0_ref, gn_ref) = refs[:9]
    refs = refs[9:]
    if fuse_out:
        wout_ref, refs = refs[0], refs[1:]
    o_ref, s_ref, qg_sc, kdec_sc, rk_sc, rv_sc, gam_sc, beta_sc, egl_sc, o_sc = refs

    nh, dk = GDN_HEADS, GDN_DK
    rows, hd = q_ref.shape
    n_chunks = rows // c
    n_double = int(math.log2(c)) - 1

    if not per_chunk_state:
        @pl.when(pl.program_id(1) == 0)
        def _():
            s_ref[...] = s0_ref[...]

    bg = bg_ref[...]
    lane = lax.broadcasted_iota(jnp.int32, bg.shape, 1)
    gam = _dot_exact_rhs(bg, tri_ref[...], dot=lambda a, b: _dot(b, a))
    y = jnp.where(lane < nh, bg, gam)
    yb = _dot_exact_rhs(y, e_ref[...])
    beta_b = yb[:, :hd]
    gam_b = yb[:, hd:]
    gl_b = jnp.broadcast_to(gam_b.reshape(n_chunks, c, hd)[:, c - 1:c, :], (n_chunks, c, hd)).reshape(rows, hd)
    eg = jnp.exp(gam_b)
    k = k_ref[...]
    qg_sc[...] = q_ref[...] * eg
    rk_sc[...] = beta_b * eg * k
    rv_sc[...] = beta_b * v_ref[...]
    kdec_sc[...] = k * jnp.exp(gl_b - gam_b)
    gam_sc[...] = gam_b
    beta_sc[...] = beta_b
    egl_sc[...] = jnp.exp(gl_b)

    ii = lax.broadcasted_iota(jnp.int32, (c, c), 0)
    jj = lax.broadcasted_iota(jnp.int32, (c, c), 1)
    incl = ii >= jj
    strict = ii > jj

    chains = [(slice(ci * c, (ci + 1) * c), slice(hh * dk, (hh + 1) * dk))
              for ci in range(n_chunks) for hh in range(nh)]
    kq, decay, pm = [], [], []
    for rs, hc in chains:
        kc = k_ref[rs, hc]
        kq.append(_dot_nt(jnp.concatenate([kc, q_ref[rs, hc]], axis=0).astype(BF), kc.astype(BF)))
    for (rs, hc), kq_i in zip(chains, kq):
        gcol = gam_sc[rs, hc]
        diff = gcol[:, :c] - _pad_transpose(gcol)[:c, :c]
        dec = jnp.where(incl, jnp.exp(jnp.where(incl, diff, 0.0)), 0.0)
        decay.append(dec)
        pm.append(jnp.where(strict, -(beta_sc[rs, hc][:, :c] * kq_i[:c] * dec), 0.0))
    mk = pm
    for _ in range(n_double):
        mk = [_dot(a.astype(BF), a.astype(BF)) for a in mk]
        pm = [p + a + _dot(p.astype(BF), a.astype(BF)) for p, a in zip(pm, mk)]
    sol, lhs2 = [], []
    for (rs, hc), p, kq_i, dec in zip(chains, pm, kq, decay):
        rhs = jnp.concatenate([rv_sc[rs, hc], rk_sc[rs, hc]], axis=1)
        sol.append(rhs + _dot(p.astype(BF), rhs.astype(BF)))
        kdt = _pad_transpose(kdec_sc[rs, hc])[:, :c]
        lhs2.append(jnp.concatenate([kq_i[c:] * dec, kdt], axis=0).astype(BF))

    def advance(idxs, states):
        t1 = []
        for idx, s in zip(idxs, states):
            rs, hc = chains[idx]
            t1.append(_dot(jnp.concatenate([sol[idx][:, dk:], qg_sc[rs, hc]], axis=0).astype(BF), s.astype(BF)))
        t2 = [_dot(lhs2[idx], (sol[idx][:, :dk] - t[:c]).astype(BF)) for idx, t in zip(idxs, t1)]
        new_states = []
        for idx, s, ta, tb in zip(idxs, states, t1, t2):
            rs, hc = chains[idx]
            o_sc[rs, hc] = ta[c:] + tb[:c]
            egl = jnp.broadcast_to(egl_sc[rs.start:rs.start + 1, hc], (dk, dk))
            new_states.append(egl * s + tb[c:])
        return new_states

    if per_chunk_state:
        idxs = list(range(len(chains)))
        new_states = advance(idxs, [s0_ref[idx // nh, idx % nh] for idx in idxs])
        for idx, s_new in zip(idxs, new_states):
            s_ref[idx // nh, idx % nh] = s_new
    else:
        state = [s_ref[0, hh] for hh in range(nh)]
        for ci in range(n_chunks):
            state = advance([ci * nh + hh for hh in range(nh)], state)
        for hh in range(nh):
            s_ref[0, hh] = state[hh]

    outs = []
    for hh in range(nh):
        hc = slice(hh * dk, (hh + 1) * dk)
        outs.append(_rms(o_sc[:, hc], gn_ref[...]) * _silu(z_ref[:, hc]))
    yv = jnp.concatenate(outs, axis=-1)
    if fuse_out:
        o_ref[...] = x_ref[...] + _dot(yv.astype(BF), wout_ref[...])
    else:
        o_ref[...] = yv


def _gdn_consts(rows, c, hd):
    r = np.arange(rows)
    tri = ((r[:, None] >= r[None, :]) & (r[:, None] // c == r[None, :] // c)).astype(np.float32)
    lane = np.arange(LANES)[:, None]
    col = np.arange(2 * hd)[None, :]
    head = (col % hd) // GDN_DK
    e = np.where(col < hd, lane == head, lane == GDN_HEADS + head).astype(np.float32)
    return jnp.asarray(tri, BF), jnp.asarray(e, BF)


def gdn_scan(x, q, k, v, z, bg, s0, g_norm, w_out, *, c, rows, n_seq, per_chunk_state, name):
    t, hd = q.shape
    nt = t // n_seq // rows
    fuse_out = w_out is not None
    tri, e = _gdn_consts(rows, c, hd)
    ns = rows // c if per_chunk_state else 1
    row = lambda b, i: (b * nt + i, 0)
    tile = lambda n: pl.BlockSpec((rows, n), row)
    st_spec = pl.BlockSpec((ns,) + s0.shape[1:], lambda b, i: (b, 0, 0, 0))
    ins, specs = [], []
    if fuse_out:
        ins.append(x)
        specs.append(tile(x.shape[1]))
    ins += [q, k, v, z, bg, tri, e, s0, g_norm.reshape(1, GDN_DK)]
    specs += [tile(hd)] * 4 + [tile(LANES), _const_spec(tri.shape), _const_spec(e.shape), st_spec,
                               _const_spec((1, GDN_DK))]
    d_out = hd
    if fuse_out:
        ins.append(_warr(w_out))
        specs.append(_wspec(w_out))
        d_out = w_out.shape[1]
    return pl.pallas_call(
        functools.partial(_gdn_scan_body, c=c, per_chunk_state=per_chunk_state, fuse_out=fuse_out),
        grid=(n_seq, nt),
        in_specs=specs,
        out_specs=[tile(d_out), st_spec],
        out_shape=[SDS((t, d_out), F32), SDS(s0.shape, F32)],
        scratch_shapes=[pltpu.VMEM((rows, hd), F32)] * 8,
        compiler_params=_params(2),
        name=name,
    )(*ins)


def _topk_mask(sb, valid, axis):
    n = sb.shape[axis]
    idx = lax.broadcasted_iota(jnp.int32, sb.shape, axis)
    cnt = jnp.zeros(sb.shape, F32)
    for m in range(n):
        sm = lax.slice_in_dim(sb, m, m + 1, axis=axis)
        ahead = (sm > sb) | ((sm == sb) & (m < idx))
        cnt = cnt + jnp.where(ahead, 1.0, 0.0)
    return jnp.where((cnt < MOBA_TOPK) & valid, 1.0, 0.0)


def _moba_p_body(q_ref, k_ref, v_ref, oh_ref, o_ref):
    length, dh = k_ref.shape
    blk = MOBA_BLOCK
    nb = length // blk
    scale = dh ** -0.5
    k = k_ref[...]
    kaug = jnp.concatenate([k.astype(BF), oh_ref[...]], axis=1)
    vb = v_ref[...].astype(BF)
    km = jnp.sum(k.reshape(nb, blk, dh), axis=1) * (1.0 / blk)
    brow = lax.broadcasted_iota(jnp.int32, (nb, blk), 0)
    rr = lax.broadcasted_iota(jnp.int32, (blk, blk), 0)
    cc = lax.broadcasted_iota(jnp.int32, (blk, blk), 1)
    causal = cc <= rr
    zero_bias = jnp.zeros((blk, dh), BF)
    pad_rows = jnp.zeros((LANES - nb, blk), F32)
    qaug = []
    for i in range(nb):
        q = q_ref[i * blk:(i + 1) * blk, :]
        qs = (q * scale).astype(BF)
        if i == 0:
            qaug.append(jnp.concatenate([qs, zero_bias], axis=1))
            continue
        sbt = jnp.where(brow < i, _dot_hi(km, q, dot=_dot_nt), NEG_INF)
        selt = _topk_mask(sbt, sbt > NEG_INF / 2, axis=0)
        bias_t = jnp.where((selt > 0.5) | (brow == i), 0.0, NEG_INF)
        bias = jnp.concatenate([bias_t, pad_rows], axis=0).T
        qaug.append(jnp.concatenate([qs, bias.astype(BF)], axis=1))
    for i in range(nb):
        n_keys = (i + 1) * blk
        s = _dot_nt(qaug[i], kaug[:n_keys])
        s_own = jnp.where(causal, s[:, i * blk:], NEG_INF)
        s = jnp.concatenate([s[:, :i * blk], s_own], axis=-1) if i > 0 else s_own
        pe = jnp.exp(s - jnp.max(s, axis=-1, keepdims=True))
        l = jnp.sum(pe, axis=-1, keepdims=True)
        o_ref[i * blk:(i + 1) * blk, :] = _dot(pe.astype(BF), vb[:n_keys]) / l


def moba_prompt_core(q, k, v, bsz, name):
    t, w = q.shape
    dh = MOBA_DH
    nh = w // dh
    length = t // bsz
    nb = length // MOBA_BLOCK
    del nb
    expand = jnp.asarray(np.arange(length)[:, None] // MOBA_BLOCK == np.arange(dh)[None, :], BF)
    spec = pl.BlockSpec((length, dh), lambda b, h: (b, h))
    return pl.pallas_call(
        _moba_p_body,
        grid=(bsz, nh),
        in_specs=[spec, spec, spec, _const_spec(expand.shape)],
        out_specs=spec,
        out_shape=SDS((t, w), F32),
        compiler_params=_params(2),
        name=name,
    )(q, k, v, expand)


def _lane_group_reduce(row, op, ncol):
    a = jnp.broadcast_to(row, (SUBLANES, LANES))
    shift = ncol
    while shift < LANES:
        a = op(a, pltpu.roll(a, shift, 1))
        shift *= 2
    return a[:1]


def _moba_s_body(pt_ref, qt_ref, hb_ref, kn_ref, vn_ref, *refs, n_pages, ncol):
    del pt_ref
    kp = refs[:n_pages]
    vp = refs[n_pages:2 * n_pages]
    o_ref, s_sc = refs[2 * n_pages:]
    page, nh, dh = kp[0].shape[1:]
    bp = MOBA_BLOCK // page
    nb = n_pages // bp
    prow = page * nh
    pack = LANES // ncol
    qt = qt_ref[0]
    lane = lax.broadcasted_iota(jnp.int32, (1, LANES), 1)
    group = lane // ncol
    qs = qt * dh ** -0.5
    rhs = jnp.concatenate([jnp.where(group == g, qs, 0.0).astype(BF) for g in range(pack)], axis=0)

    km = []
    for n in range(nb):
        tot = kp[n * bp][0].sum(axis=0)
        for j in range(1, bp):
            tot = tot + kp[n * bp + j][0].sum(axis=0)
        km.append(tot * (1.0 / MOBA_BLOCK))
    km2 = jnp.concatenate(km, axis=0)
    sbm = _dot_hi(km2, qt)
    rh = lax.broadcasted_iota(jnp.int32, sbm.shape, 0) % nh
    ch = lax.broadcasted_iota(jnp.int32, sbm.shape, 1) % ncol % nh
    sb = jnp.sum(jnp.where(rh == ch, sbm, 0.0).reshape(nb, nh, LANES), axis=1)
    sel = _topk_mask(sb, jnp.full(sb.shape, True), axis=0)
    sel_bias = jnp.where(sel > 0.5, 0.0, NEG_INF)

    n_own = kn_ref.shape[1]
    ro = lax.broadcasted_iota(jnp.int32, (n_own, LANES), 0)
    co = lax.broadcasted_iota(jnp.int32, (n_own, LANES), 1)
    own_ok = (ro % nh == co % nh) & (ro // nh <= co // nh) & (co < ncol)
    s_own = jnp.where(own_ok, _dot(kn_ref[0].astype(BF), qs.astype(BF)), NEG_INF)
    mx = jnp.max(s_own, axis=0, keepdims=True)

    n_groups = n_pages // pack
    for j in range(n_groups):
        pages = range(j * pack, (j + 1) * pack)
        k4 = jnp.concatenate([kp[p][0].reshape(prow, dh).astype(BF) for p in pages], axis=1)
        bias = sel_bias[(j * pack) // bp:(j * pack) // bp + 1, :]
        for g in range(1, pack):
            blk = (j * pack + g) // bp
            bias = jnp.where(group >= g, sel_bias[blk:blk + 1, :], bias)
        s = _dot(k4, rhs) + (hb_ref[...] + bias)
        s_sc[j * prow:(j + 1) * prow, :] = s
        mx = jnp.maximum(mx, jnp.max(s, axis=0, keepdims=True))
    mx = _lane_group_reduce(mx, jnp.maximum, ncol)

    pe = jnp.exp(s_own - mx)
    l = jnp.sum(pe, axis=0, keepdims=True)
    acc = _dot_tn(pe.astype(BF), vn_ref[0].astype(BF))[:ncol]
    for j in range(n_groups):
        pages = range(j * pack, (j + 1) * pack)
        pe = jnp.exp(s_sc[j * prow:(j + 1) * prow, :] - mx)
        l = l + jnp.sum(pe, axis=0, keepdims=True)
        v4 = jnp.concatenate([vp[p][0].reshape(prow, dh).astype(BF) for p in pages], axis=1)
        r = _dot_tn(pe.astype(BF), v4)
        for g in range(pack):
            acc = acc + r[g * ncol:(g + 1) * ncol, g * dh:(g + 1) * dh]
    l = _lane_group_reduce(l, jnp.add, ncol)
    l_col = jnp.broadcast_to(l, (LANES, LANES)).T[:ncol, :1]
    o_ref[0] = acc / l_col


def moba_sample_core(qt, kn, vn, pool_k, pool_v, page_table, name):
    bsz, dh, ncol = qt.shape
    n_pages = page_table.shape[1]
    page, nh = pool_k.shape[1:3]
    n_own = kn.shape[1]
    pack = LANES // ncol
    qt = jnp.tile(qt, (1, 1, pack))
    r = np.arange(page * nh)[:, None]
    c = np.arange(LANES)[None, :]
    head_bias = jnp.asarray(np.where(r % nh == c % ncol % nh, 0.0, NEG_INF), F32)

    def page_spec(p):
        return pl.BlockSpec((1, page, nh, dh), lambda b, pt: (pt[b, p], 0, 0, 0))

    grid_spec = pltpu.PrefetchScalarGridSpec(
        num_scalar_prefetch=1,
        grid=(bsz,),
        in_specs=[pl.BlockSpec((1, dh, LANES), lambda b, pt: (b, 0, 0)),
                  pl.BlockSpec(head_bias.shape, lambda b, pt: (0, 0)),
                  pl.BlockSpec((1, n_own, dh), lambda b, pt: (b, 0, 0)),
                  pl.BlockSpec((1, n_own, dh), lambda b, pt: (b, 0, 0))]
                 + [page_spec(p) for p in range(n_pages)] * 2,
        out_specs=pl.BlockSpec((1, ncol, dh), lambda b, pt: (b, 0, 0)),
        scratch_shapes=[pltpu.VMEM((n_pages // pack * page * nh, LANES), F32)],
    )
    return pl.pallas_call(
        functools.partial(_moba_s_body, n_pages=n_pages, ncol=ncol),
        grid_spec=grid_spec,
        out_shape=SDS((bsz, ncol, dh), F32),
        compiler_params=_params(1),
        name=name,
    )(page_table, qt, head_bias, kn, vn, *([pool_k] * n_pages), *([pool_v] * n_pages))


def _to_pm(a):
    b, l, c = a.shape
    return jnp.transpose(a, (1, 0, 2)).reshape(l * b, c)


def _from_pm(a, bsz):
    n, c = a.shape
    return jnp.transpose(a.reshape(n // bsz, bsz, c), (1, 0, 2))


def _pad_lanes(vec, offset):
    out = jnp.zeros((1, LANES), F32)
    return lax.dynamic_update_slice(out, vec.reshape(1, -1).astype(F32), (0, offset))


def kernel(x_prompt, x_sample, state_a_conv, state_b_conv, state_b_rec, cache_c_k, cache_c_v, cache_mem_k, cache_mem_v, page_table, mem_prompt, norm_mix, norm_mem, norm_xattn, norm_ffn, norm_final, a_w_in, a_w_conv, a_w_out, b_w_in, b_w_conv, b_a_log, b_dt_bias, b_norm, b_w_out, c_w_qkv, c_w_out, x_w_q, x_w_kv, x_w_o, f_w_up, f_w_down):
    bp, seq, d = x_prompt.shape
    bs, dseq, _ = x_sample.shape
    depth = norm_mix.shape[0]
    n_mixers = 3
    tm = 512
    ts = bs * dseq

    bf = lambda w: w.astype(BF)
    a_w_in, a_w_out, b_w_out, c_w_qkv, c_w_out = map(bf, (a_w_in, a_w_out, b_w_out, c_w_qkv, c_w_out))
    x_w_q, x_w_kv, x_w_o, f_w_up, f_w_down = map(bf, (x_w_q, x_w_kv, x_w_o, f_w_up, f_w_down))
    cc = b_w_conv.shape[-1]
    hd = GDN_HEADS * GDN_DK
    b_w_qkvz = bf(b_w_in[:, :, :cc + hd])
    b_w_ba = bf(jnp.pad(b_w_in[:, :, cc + hd:], ((0, 0), (0, 0), (0, LANES - 2 * GDN_HEADS))))

    xp = x_prompt.reshape(bp * seq, d)
    xs = _to_pm(x_sample)

    mem_len = mem_prompt.shape[1]
    mk32, mv32, mkbf, mvbf = memory_kv_all(mem_prompt.reshape(bp * mem_len, d), norm_mem, x_w_kv, XA_HEADS, tm)

    a_p, a_s, bc_p, bc_s, br_p, br_s = [], [], [], [], [], []
    ck_p, cv_p, ck_s, cv_s = [], [], [], []
    for i in range(depth):
        kind, j = i % n_mixers, i // n_mixers
        tag = f"l{i}"
        if kind == 0:
            xp, st = mixer_a_prompt(xp, norm_mix[i], _Layer(a_w_in, j), a_w_conv[j], _Layer(a_w_out, j), bp, tm, tag + "_mixa_p")
            a_p.append(st)
            st_pm = _to_pm(state_a_conv[j])
            xs, nst = mixer_a_sample(xs, st_pm, norm_mix[i], _Layer(a_w_in, j), a_w_conv[j], _Layer(a_w_out, j), bs, tag + "_mixa_s")
            a_s.append(_from_pm(nst, bs))
        elif kind == 1:
            alog = _pad_lanes(b_a_log[j], GDN_HEADS)
            dtb = _pad_lanes(b_dt_bias[j], GDN_HEADS)
            q, k, v, z, bg, st = gdn_proj_prompt(xp, norm_mix[i], _Layer(b_w_qkvz, j), _Layer(b_w_ba, j), b_w_conv[j], alog, dtb,
                                                 bp, 256, tag + "_gdnproj_p")
            bc_p.append(st)
            s0 = jnp.zeros((bp, GDN_HEADS, GDN_DK, GDN_DK), F32)
            xp, s_fin = gdn_scan(xp, q, k, v, z, bg, s0, b_norm[j], _Layer(b_w_out, j), c=math.gcd(seq, GDN_CHUNK),
                                 rows=256, n_seq=bp, per_chunk_state=False, name=tag + "_gdnscan_p")
            br_p.append(s_fin)
            st_pm = _to_pm(state_b_conv[j])
            q, k, v, z, bg, nst = gdn_proj_sample(xs, st_pm, norm_mix[i], _Layer(b_w_qkvz, j), _Layer(b_w_ba, j), b_w_conv[j],
                                                  alog, dtb, bs, tag + "_gdnproj_s")
            bc_s.append(_from_pm(nst, bs))
            cpad = SUBLANES
            padded = lambda a: jnp.pad(_from_pm(a, bs), ((0, 0), (0, cpad - dseq), (0, 0))).reshape(bs * cpad, -1)
            nbatch = 8
            y, s_fin = gdn_scan(None, padded(q), padded(k), padded(v), padded(z), padded(bg), state_b_rec[j],
                                b_norm[j], None, c=cpad, rows=nbatch * cpad, n_seq=bs // nbatch,
                                per_chunk_state=True, name=tag + "_gdnscan_s")
            br_s.append(s_fin)
            y = _to_pm(y.reshape(bs, cpad, hd)[:, :dseq])
            xs = mm_res(y, _Layer(b_w_out, j), xs, ts, tag + "_gdnout_s")
        else:
            q, k, v = norm_mm(xp, norm_mix[i], _Layer(c_w_qkv, j), 3, tm, tag + "_mobaqkv_p")
            ck_p.append(k.reshape(bp, seq, MOBA_HEADS, MOBA_DH))
            cv_p.append(v.reshape(bp, seq, MOBA_HEADS, MOBA_DH))
            o = moba_prompt_core(q, k, v, bp, tag + "_moba_p")
            xp = mm_res(o, _Layer(c_w_out, j), xp, tm, tag + "_mobaout_p")
            q, k, v = norm_mm(xs, norm_mix[i], _Layer(c_w_qkv, j), 3, ts, tag + "_mobaqkv_s")
            kn = _from_pm(k, bs)
            vn = _from_pm(v, bs)
            ck_s.append(kn.reshape(bs, dseq, MOBA_HEADS, MOBA_DH))
            cv_s.append(vn.reshape(bs, dseq, MOBA_HEADS, MOBA_DH))
            qt = jnp.transpose(_from_pm(q, bs).reshape(bs, dseq * MOBA_HEADS, MOBA_DH), (0, 2, 1))
            rows_kv = lambda a: a.reshape(bs, dseq * MOBA_HEADS, MOBA_DH)
            o = moba_sample_core(qt, rows_kv(kn), rows_kv(vn), cache_c_k[j], cache_c_v[j], page_table, tag + "_moba_s")
            xs = mm_res(_to_pm(o.reshape(bs, dseq, MOBA_HEADS * MOBA_DH)), _Layer(c_w_out, j), xs, ts, tag + "_mobaout_s")

        xp = xattn_prompt(xp, norm_xattn[i], _Layer(x_w_q, i), mkbf, mvbf, i, _Layer(x_w_o, i), bp, tm, tag + "_xattn_p")
        (qs,) = norm_mm(xs, norm_xattn[i], _Layer(x_w_q, i), 1, ts, tag + "_xattnq_s")
        dh = d // XA_HEADS
        qt = jnp.transpose(_from_pm(qs, bs).reshape(bs, dseq * XA_HEADS, dh), (0, 2, 1))
        o = xattn_sample_core(qt, cache_mem_k, cache_mem_v, i, 8, tag + "_xattn_s")
        xs = mm_res(_to_pm(o.reshape(bs, dseq, d)), _Layer(x_w_o, i), xs, ts, tag + "_xattnout_s")

        last = i == depth - 1
        xp = ffn(xp, norm_ffn[i], _Layer(f_w_up, i), _Layer(f_w_down, i), norm_final, last, tm, tag + "_ffn_p")
        xs = ffn(xs, norm_ffn[i], _Layer(f_w_up, i), _Layer(f_w_down, i), norm_final, last, ts, tag + "_ffn_s")

    mem_shape = (depth, bp, mem_len, XA_HEADS, d // XA_HEADS)
    return (xp.reshape(bp, seq, d), _from_pm(xs, bs),
            jnp.stack(a_p), jnp.stack(a_s),
            jnp.stack(bc_p), jnp.stack(bc_s),
            jnp.stack(br_p), jnp.stack(br_s),
            jnp.stack(ck_p), jnp.stack(cv_p), jnp.stack(ck_s), jnp.stack(cv_s),
            mk32.reshape(mem_shape), mv32.reshape(mem_shape))
```

```python
import functools
import math

import numpy as np
import jax
import jax.numpy as jnp
from jax import lax
from jax.experimental import pallas as pl
from jax.experimental.pallas import tpu as pltpu

F32 = jnp.float32
BF = jnp.bfloat16
SDS = jax.ShapeDtypeStruct

EPS = 1e-6
NEG_INF = -1e30
LANES = 128
SUBLANES = 8
VMEM_LIMIT = 56 * 1024 * 1024

GDN_HEADS = 8
GDN_DK = 128
GDN_CHUNK = 64
MOBA_HEADS = 8
MOBA_DH = 128
MOBA_BLOCK = 256
MOBA_TOPK = 3
PAGE_SIZE = 128
XA_HEADS = 4


def _params(n_axes):
    return pltpu.CompilerParams(dimension_semantics=("arbitrary",) * n_axes,
                                vmem_limit_bytes=VMEM_LIMIT)


def _const_spec(shape):
    nd = len(shape)
    return pl.BlockSpec(shape, lambda *_: (0,) * nd, pipeline_mode=pl.Buffered(1))


class _Layer:
    def __init__(self, arr, j):
        self.arr, self.j = arr, j

    @property
    def shape(self):
        return self.arr.shape[1:]


def _wspec(w):
    if not isinstance(w, _Layer):
        return _const_spec(w.shape)
    nd = len(w.shape)
    j = w.j
    return pl.BlockSpec((None,) + w.shape, lambda *_: (j,) + (0,) * nd, pipeline_mode=pl.Buffered(1))


def _warr(w):
    return w.arr if isinstance(w, _Layer) else w


def _whole_spec(shape):
    nd = len(shape)
    return pl.BlockSpec(shape, lambda *_: (0,) * nd)


def _dot(a, b):
    return jnp.dot(a, b, preferred_element_type=F32)


def _dot_nt(a, b):
    return lax.dot_general(a, b, (((1,), (1,)), ((), ())), preferred_element_type=F32)


def _dot_tn(a, b):
    return lax.dot_general(a, b, (((0,), (0,)), ((), ())), preferred_element_type=F32)


def _split3(a):
    a0 = a.astype(BF)
    r = a - a0.astype(F32)
    a1 = r.astype(BF)
    a2 = (r - a1.astype(F32)).astype(BF)
    return a0, a1, a2


def _dot_exact_rhs(a, b_bf, dot=_dot):
    a0, a1, a2 = _split3(a)
    return dot(a0, b_bf) + dot(a1, b_bf) + dot(a2, b_bf)


def _dot_hi(a, b, dot=_dot):
    a0, a1, _ = _split3(a)
    b0, b1, _ = _split3(b)
    return dot(a0, b0) + (dot(a0, b1) + dot(a1, b0))


def _rms(x, g):
    ms = jnp.mean(x * x, axis=-1, keepdims=True)
    return x * lax.rsqrt(ms + EPS) * g


def _silu(x):
    return x * jax.nn.sigmoid(x)


def _softplus(x):
    return jnp.maximum(x, 0.0) + jnp.log1p(jnp.exp(-jnp.abs(x)))


def _norm_mm_body(x_ref, g_ref, w_ref, *o_refs):
    h = _rms(x_ref[...], g_ref[...]).astype(BF)
    n = o_refs[0].shape[-1]
    for j, o_ref in enumerate(o_refs):
        o_ref[...] = _dot(h, w_ref[:, j * n:(j + 1) * n])


def norm_mm(x, g, w, n_out, tm, name):
    t, k = x.shape
    n = w.shape[1] // n_out
    return pl.pallas_call(
        _norm_mm_body,
        grid=(t // tm,),
        in_specs=[pl.BlockSpec((tm, k), lambda i: (i, 0)), _const_spec((1, k)), _wspec(w)],
        out_specs=[pl.BlockSpec((tm, n), lambda i: (i, 0))] * n_out,
        out_shape=[SDS((t, n), F32)] * n_out,
        compiler_params=_params(1),
        name=name,
    )(x, g.reshape(1, k), _warr(w))


def _mm_res_body(a_ref, w_ref, r_ref, o_ref):
    o_ref[...] = r_ref[...] + _dot(a_ref[...].astype(BF), w_ref[...])


def mm_res(a, w, res, tm, name):
    t, k = a.shape
    n = w.shape[1]
    return pl.pallas_call(
        _mm_res_body,
        grid=(t // tm,),
        in_specs=[pl.BlockSpec((tm, k), lambda i: (i, 0)), _wspec(w),
                  pl.BlockSpec((tm, n), lambda i: (i, 0))],
        out_specs=pl.BlockSpec((tm, n), lambda i: (i, 0)),
        out_shape=SDS((t, n), F32),
        compiler_params=_params(1),
        name=name,
    )(a, _warr(w), res)


def _memkv_body(x_ref, g_ref, w_ref, k32_ref, v32_ref, kbf_ref, vbf_ref):
    h = _rms(x_ref[...], g_ref[0]).astype(BF)
    d = kbf_ref.shape[-1]
    nh, dh = k32_ref.shape[2:]
    for j, (split_ref, bf_ref) in enumerate(((k32_ref, kbf_ref), (v32_ref, vbf_ref))):
        r = _dot(h, w_ref[0, :, j * d:(j + 1) * d])
        bf_ref[0] = r.astype(BF)
        for hh in range(nh):
            split_ref[0, :, hh, :] = r[:, hh * dh:(hh + 1) * dh]


def memory_kv_all(mem2d, norm_mem, w_kv, nh, tm):
    depth, d, _ = w_kv.shape
    t = mem2d.shape[0]
    flat = pl.BlockSpec((1, tm, d), lambda l, i: (l, i, 0))
    split = pl.BlockSpec((1, tm, nh, d // nh), lambda l, i: (l, i, 0, 0))
    return pl.pallas_call(
        _memkv_body,
        grid=(depth, t // tm),
        in_specs=[pl.BlockSpec((tm, d), lambda l, i: (i, 0)),
                  pl.BlockSpec((1, 1, d), lambda l, i: (l, 0, 0)),
                  pl.BlockSpec((1, d, 2 * d), lambda l, i: (l, 0, 0))],
        out_specs=[split, split, flat, flat],
        out_shape=[SDS((depth, t, nh, d // nh), F32)] * 2 + [SDS((depth, t, d), BF)] * 2,
        compiler_params=_params(2),
        name="memory_kv",
    )(mem2d, norm_mem.reshape(depth, 1, d), w_kv)


def _ffn_body(xp_ref, xs_ref, g_ref, wu_ref, wd_ref, gf_ref, op_ref, os_ref, *, ck, final, ntp):
    def tile(x_ref, o_ref):
        x = x_ref[...]
        h = _rms(x, g_ref[...]).astype(BF)
        dff = wd_ref.shape[0]
        acc = x
        for c0 in range(0, dff, ck):
            gate = _dot(h, wu_ref[:, c0:c0 + ck])
            up = _dot(h, wu_ref[:, dff + c0:dff + c0 + ck])
            a = (_silu(gate) * up).astype(BF)
            acc = acc + _dot(a, wd_ref[c0:c0 + ck, :])
        if final:
            acc = _rms(acc, gf_ref[...])
        o_ref[...] = acc

    i = pl.program_id(0)
    pl.when(i < ntp)(lambda: tile(xp_ref, op_ref))
    pl.when(i == ntp)(lambda: tile(xs_ref, os_ref))


def ffn(xp, xs, g, w_up, w_down, g_final, final, tm, name):
    t, d = xp.shape
    ntp = t // tm
    body = functools.partial(_ffn_body, ck=2 * LANES, final=final, ntp=ntp)
    p_spec = pl.BlockSpec((tm, d), lambda i: (jnp.minimum(i, ntp - 1), 0))
    s_spec = pl.BlockSpec(xs.shape, lambda i: (0, 0))
    return pl.pallas_call(
        body,
        grid=(ntp + 1,),
        in_specs=[p_spec, s_spec, _const_spec((1, d)), _wspec(w_up), _wspec(w_down), _const_spec((1, d))],
        out_specs=[p_spec, s_spec],
        out_shape=[SDS((t, d), F32), SDS(xs.shape, F32)],
        compiler_params=_params(1),
        name=name,
    )(xp, xs, g.reshape(1, d), _warr(w_up), _warr(w_down), g_final.reshape(1, d))


def _xattn_p_stages(x_ref, g_ref, wq_ref, k_ref, v_ref, wo_ref, o_ref, *, nh):
    x = x_ref[...]
    d = x.shape[-1]
    dh = d // nh
    h = _rms(x, g_ref[...]).astype(BF)
    q = (_dot(h, wq_ref[...]) * dh ** -0.5).astype(BF)
    yield
    outs = []
    for hh in range(nh):
        cs = slice(hh * dh, (hh + 1) * dh)
        s = _dot_nt(q[:, cs], k_ref[0, :, cs])
        p = jnp.exp(s - jnp.max(s, axis=-1, keepdims=True))
        l = jnp.sum(p, axis=-1, keepdims=True)
        outs.append((_dot(p.astype(BF), v_ref[0, :, cs]) / l).astype(BF))
        yield
    o_ref[...] = x + _dot(jnp.concatenate(outs, axis=-1), wo_ref[...])


def _xattn_s_stages(qt_ref, k_ref, v_ref, o_ref, *, nb, nh):
    m_len, _, dh = k_ref.shape[2:]
    rows = m_len * nh
    ncol = qt_ref.shape[-1]
    rh = lax.broadcasted_iota(jnp.int32, (rows, ncol), 0) % nh
    ch = lax.broadcasted_iota(jnp.int32, (rows, ncol), 1) % nh
    head_bias = jnp.where(rh == ch, 0.0, NEG_INF)
    s = [_dot(k_ref[0, j].reshape(rows, dh).astype(BF), (qt_ref[j] * dh ** -0.5).astype(BF)) + head_bias
         for j in range(nb)]
    yield
    p = [jnp.exp(a - jnp.max(a, axis=0, keepdims=True)) for a in s]
    yield
    p = [(a / jnp.sum(a, axis=0, keepdims=True)).astype(BF) for a in p]
    yield
    for j in range(nb):
        o_ref[j] = _dot_tn(p[j], v_ref[0, j].reshape(rows, dh).astype(BF))


def _interleave(*stage_generators):
    live = list(stage_generators)
    while live:
        for gen in list(live):
            if next(gen, StopIteration) is StopIteration:
                live.remove(gen)


def _xattn_both_body(x_ref, g_ref, wq_ref, k_ref, v_ref, wo_ref, qt_ref, ck_ref, cv_ref, o_ref, os_ref, *, nb):
    _interleave(_xattn_p_stages(x_ref, g_ref, wq_ref, k_ref, v_ref, wo_ref, o_ref, nh=XA_HEADS),
                _xattn_s_stages(qt_ref, ck_ref, cv_ref, os_ref, nb=nb, nh=ck_ref.shape[3]))


def xattn_both(x, g, w_q, k_bf, v_bf, layer, w_o, qt, cache_k, cache_v, bsz, tm, name):
    t, d = x.shape
    nt = t // bsz // tm
    m_len = k_bf.shape[1] // bsz
    bs, dh, ncol = qt.shape
    nh_s = cache_k.shape[3]
    steps = bsz * nt
    assert bs % steps == 0, (bs, steps)
    nb = bs // steps
    kv = lambda a: a.reshape(a.shape[0] * bsz, m_len, d)
    row = lambda b, i: (b * nt + i, 0)
    cache_spec = pl.BlockSpec((1, nb, m_len, nh_s, dh), lambda b, i: (layer, b * nt + i, 0, 0, 0))
    return pl.pallas_call(
        functools.partial(_xattn_both_body, nb=nb),
        grid=(bsz, nt),
        in_specs=[pl.BlockSpec((tm, d), row), _const_spec((1, d)), _wspec(w_q),
                  pl.BlockSpec((1, m_len, d), lambda b, i: (layer * bsz + b, 0, 0)),
                  pl.BlockSpec((1, m_len, d), lambda b, i: (layer * bsz + b, 0, 0)),
                  _wspec(w_o),
                  pl.BlockSpec((nb, dh, ncol), lambda b, i: (b * nt + i, 0, 0)), cache_spec, cache_spec],
        out_specs=[pl.BlockSpec((tm, d), row), pl.BlockSpec((nb, ncol, dh), lambda b, i: (b * nt + i, 0, 0))],
        out_shape=[SDS((t, d), F32), SDS((bs, ncol, dh), F32)],
        compiler_params=_params(2),
        name=name,
    )(x, g.reshape(1, d), _warr(w_q), kv(k_bf), kv(v_bf), _warr(w_o), qt, cache_k, cache_v)


def _mixa_p_body(x_ref, g_ref, win_ref, wc_ref, wout_ref, o_ref, st_ref, zbuf, *, nt):
    i = pl.program_id(1)
    tm, d = x_ref.shape
    width = wc_ref.shape[0]

    @pl.when(i == 0)
    def _():
        zbuf[0:SUBLANES, :] = jnp.zeros((SUBLANES, d), F32)

    x = x_ref[...]
    h = _rms(x, g_ref[...]).astype(BF)
    gate_b = _dot(h, win_ref[:, 0:d])
    zn = _dot(h, win_ref[:, d:2 * d]) * _dot(h, win_ref[:, 2 * d:3 * d])
    zbuf[SUBLANES:SUBLANES + tm, :] = zn
    conv = wc_ref[width - 1:width, :] * zn
    for j in range(width - 1):
        off = SUBLANES - (width - 1) + j
        conv = conv + wc_ref[j:j + 1, :] * zbuf[off:off + tm, :]
    o_ref[...] = x + _dot((gate_b * conv).astype(BF), wout_ref[...])
    zbuf[0:SUBLANES, :] = zbuf[tm:tm + SUBLANES, :]

    @pl.when(i == nt - 1)
    def _():
        st_ref[0] = zbuf[SUBLANES - (width - 1):SUBLANES, :]


def mixer_a_prompt(x, g, w_in, w_conv, w_out, bsz, tm, name):
    t, d = x.shape
    nt = t // bsz // tm
    width = w_conv.shape[0]
    row = lambda b, i: (b * nt + i, 0)
    return pl.pallas_call(
        functools.partial(_mixa_p_body, nt=nt),
        grid=(bsz, nt),
        in_specs=[pl.BlockSpec((tm, d), row), _const_spec((1, d)), _wspec(w_in),
                  _const_spec(w_conv.shape), _wspec(w_out)],
        out_specs=[pl.BlockSpec((tm, d), row), pl.BlockSpec((1, width - 1, d), lambda b, i: (b, 0, 0))],
        out_shape=[SDS((t, d), F32), SDS((bsz, width - 1, d), F32)],
        scratch_shapes=[pltpu.VMEM((tm + SUBLANES, d), F32)],
        compiler_params=_params(2),
        name=name,
    )(x, g.reshape(1, d), _warr(w_in), w_conv, _warr(w_out))


def _posmajor_conv(state, z, wc_ref, bsz):
    width = wc_ref.shape[0]
    n = z.shape[0]
    zpad = jnp.concatenate([state, z], axis=0)
    conv = wc_ref[0:1, :] * zpad[0:n]
    for j in range(1, width):
        conv = conv + wc_ref[j:j + 1, :] * zpad[j * bsz:j * bsz + n]
    return conv, zpad[n:]


def _mixa_s_body(x_ref, st_ref, g_ref, win_ref, wc_ref, wout_ref, o_ref, nst_ref, *, bsz):
    x = x_ref[...]
    d = x.shape[-1]
    h = _rms(x, g_ref[...]).astype(BF)
    gate_b = _dot(h, win_ref[:, 0:d])
    zn = _dot(h, win_ref[:, d:2 * d]) * _dot(h, win_ref[:, 2 * d:3 * d])
    conv, new_state = _posmajor_conv(st_ref[...], zn, wc_ref, bsz)
    o_ref[...] = x + _dot((gate_b * conv).astype(BF), wout_ref[...])
    nst_ref[...] = new_state


def mixer_a_sample(x, state_pm, g, w_in, w_conv, w_out, bsz, name):
    t, d = x.shape
    return pl.pallas_call(
        functools.partial(_mixa_s_body, bsz=bsz),
        grid=(1,),
        in_specs=[_const_spec(x.shape), _const_spec(state_pm.shape), _const_spec((1, d)),
                  _wspec(w_in), _const_spec(w_conv.shape), _wspec(w_out)],
        out_specs=[_whole_spec(x.shape), _whole_spec(state_pm.shape)],
        out_shape=[SDS(x.shape, F32), SDS(state_pm.shape, F32)],
        compiler_params=_params(1),
        name=name,
    )(x, state_pm, g.reshape(1, d), _warr(w_in), w_conv, _warr(w_out))


GDN_COL_CHUNK = 2 * LANES


def _gdn_qkv_chunk(conv, c0, q_ref, k_ref, v_ref):
    nh, dk = GDN_HEADS, GDN_DK
    hd = nh * dk
    act = _silu(conv)
    section, col = c0 // hd, c0 % hd
    if section == 2:
        v_ref[:, col:col + GDN_COL_CHUNK] = act
        return
    ref = q_ref if section == 0 else k_ref
    for j in range(GDN_COL_CHUNK // dk):
        t = act[:, j * dk:(j + 1) * dk]
        t = t * lax.rsqrt(jnp.sum(t * t, axis=-1, keepdims=True) + EPS)
        ref[:, col + j * dk:col + (j + 1) * dk] = t * dk ** -0.5 if section == 0 else t


def _gdn_gates(ba, alog_ref, dtb_ref, bg_ref):
    nh = GDN_HEADS
    lane = lax.broadcasted_iota(jnp.int32, ba.shape, 1)
    beta = jax.nn.sigmoid(ba)
    g = -jnp.exp(alog_ref[...]) * _softplus(ba + dtb_ref[...])
    bg_ref[...] = jnp.where(lane < nh, beta, jnp.where(lane < 2 * nh, g, 0.0))


def _gdn_proj_p_body(x_ref, g_ref, w_ref, wba_ref, wc_ref, alog_ref, dtb_ref,
                     q_ref, k_ref, v_ref, z_ref, bg_ref, st_ref, cbuf, *, nt):
    i = pl.program_id(1)
    tm = x_ref.shape[0]
    width, cc = wc_ref.shape
    ck = GDN_COL_CHUNK

    @pl.when(i == 0)
    def _():
        cbuf[0:SUBLANES, :] = jnp.zeros((SUBLANES, cc), F32)

    h = _rms(x_ref[...], g_ref[...]).astype(BF)
    for c0 in range(0, cc, ck):
        cols = slice(c0, c0 + ck)
        pre = _dot(h, w_ref[:, cols])
        cbuf[SUBLANES:SUBLANES + tm, cols] = pre
        conv = wc_ref[width - 1:width, cols] * pre
        for j in range(width - 1):
            off = SUBLANES - (width - 1) + j
            conv = conv + wc_ref[j:j + 1, cols] * cbuf[off:off + tm, cols]
        _gdn_qkv_chunk(conv, c0, q_ref, k_ref, v_ref)
    for c0 in range(0, z_ref.shape[1], ck):
        z_ref[:, c0:c0 + ck] = _dot(h, w_ref[:, cc + c0:cc + c0 + ck])
    _gdn_gates(_dot(h, wba_ref[...]), alog_ref, dtb_ref, bg_ref)
    cbuf[0:SUBLANES, :] = cbuf[tm:tm + SUBLANES, :]

    @pl.when(i == nt - 1)
    def _():
        st_ref[0] = cbuf[SUBLANES - (width - 1):SUBLANES, :]


def gdn_proj_prompt(x, g, w_qkvz, w_ba, w_conv, alog, dtb, bsz, tm, name):
    t, d = x.shape
    nt = t // bsz // tm
    width, cc = w_conv.shape
    hd = w_qkvz.shape[1] - cc
    row = lambda b, i: (b * nt + i, 0)
    tile = lambda n: pl.BlockSpec((tm, n), row)
    return pl.pallas_call(
        functools.partial(_gdn_proj_p_body, nt=nt),
        grid=(bsz, nt),
        in_specs=[tile(d), _const_spec((1, d)), _wspec(w_qkvz), _wspec(w_ba),
                  _const_spec(w_conv.shape), _const_spec(alog.shape), _const_spec(dtb.shape)],
        out_specs=[tile(hd), tile(hd), tile(hd), tile(hd), tile(LANES),
                   pl.BlockSpec((1, width - 1, cc), lambda b, i: (b, 0, 0))],
        out_shape=[SDS((t, hd), F32)] * 4 + [SDS((t, LANES), F32), SDS((bsz, width - 1, cc), F32)],
        scratch_shapes=[pltpu.VMEM((tm + SUBLANES, cc), F32)],
        compiler_params=_params(2),
        name=name,
    )(x, g.reshape(1, d), _warr(w_qkvz), _warr(w_ba), w_conv, alog, dtb)


def _gdn_proj_s_body(x_ref, st_ref, g_ref, w_ref, wba_ref, wc_ref, alog_ref, dtb_ref,
                     q_ref, k_ref, v_ref, z_ref, bg_ref, nst_ref, *, bsz):
    cc = wc_ref.shape[1]
    h = _rms(x_ref[...], g_ref[...]).astype(BF)
    qkv = _dot(h, w_ref[:, :cc])
    z_ref[...] = _dot(h, w_ref[:, cc:])
    ba = _dot(h, wba_ref[...])
    conv, new_state = _posmajor_conv(st_ref[...], qkv, wc_ref, bsz)
    for c0 in range(0, cc, GDN_COL_CHUNK):
        _gdn_qkv_chunk(conv[:, c0:c0 + GDN_COL_CHUNK], c0, q_ref, k_ref, v_ref)
    _gdn_gates(ba, alog_ref, dtb_ref, bg_ref)
    nst_ref[...] = new_state


def gdn_proj_sample(x, state_pm, g, w_qkvz, w_ba, w_conv, alog, dtb, bsz, name):
    t, d = x.shape
    cc = w_conv.shape[1]
    hd = w_qkvz.shape[1] - cc
    ins = (x, state_pm, g.reshape(1, d), w_qkvz, w_ba, w_conv, alog, dtb)
    outs = [SDS((t, hd), F32)] * 4 + [SDS((t, LANES), F32), SDS(state_pm.shape, F32)]
    return pl.pallas_call(
        functools.partial(_gdn_proj_s_body, bsz=bsz),
        grid=(1,),
        in_specs=[_wspec(a) for a in ins],
        out_specs=[_whole_spec(o.shape) for o in outs],
        out_shape=outs,
        compiler_params=_params(1),
        name=name,
    )(*[_warr(a) for a in ins])


def _pad_transpose(a):
    c = a.shape[0]
    if c < LANES:
        a = jnp.concatenate([a, jnp.zeros((LANES - c, a.shape[1]), a.dtype)], axis=0)
    return a.T


def _gdn_scan_body(*refs, c, per_chunk_state, fuse_out):
    if fuse_out:
        x_ref, refs = refs[0], refs[1:]
    (q_ref, k_ref, v_ref, z_ref, bg_ref, tri_ref, e_ref, s0_ref, gn_ref) = refs[:9]
    refs = refs[9:]
    if fuse_out:
        wout_ref, refs = refs[0], refs[1:]
    o_ref, s_ref, qg_sc, kdec_sc, rk_sc, rv_sc, gam_sc, beta_sc, egl_sc, o_sc = refs

    nh, dk = GDN_HEADS, GDN_DK
    rows, hd = q_ref.shape
    n_chunks = rows // c
    n_double = int(math.log2(c)) - 1

    if not per_chunk_state:
        @pl.when(pl.program_id(1) == 0)
        def _():
            s_ref[...] = s0_ref[...]

    bg = bg_ref[...]
    lane = lax.broadcasted_iota(jnp.int32, bg.shape, 1)
    gam = _dot_exact_rhs(bg, tri_ref[...], dot=lambda a, b: _dot(b, a))
    y = jnp.where(lane < nh, bg, gam)
    yb = _dot_exact_rhs(y, e_ref[...])
    beta_b = yb[:, :hd]
    gam_b = yb[:, hd:]
    gl_b = jnp.broadcast_to(gam_b.reshape(n_chunks, c, hd)[:, c - 1:c, :], (n_chunks, c, hd)).reshape(rows, hd)
    eg = jnp.exp(gam_b)
    k = k_ref[...]
    qg_sc[...] = q_ref[...] * eg
    rk_sc[...] = beta_b * eg * k
    rv_sc[...] = beta_b * v_ref[...]
    kdec_sc[...] = k * jnp.exp(gl_b - gam_b)
    gam_sc[...] = gam_b
    beta_sc[...] = beta_b
    egl_sc[...] = jnp.exp(gl_b)

    ii = lax.broadcasted_iota(jnp.int32, (c, c), 0)
    jj = lax.broadcasted_iota(jnp.int32, (c, c), 1)
    incl = ii >= jj
    strict = ii > jj

    chains = [(slice(ci * c, (ci + 1) * c), slice(hh * dk, (hh + 1) * dk))
              for ci in range(n_chunks) for hh in range(nh)]
    kq, decay, pm = [], [], []
    for rs, hc in chains:
        kc = k_ref[rs, hc]
        kq.append(_dot_nt(jnp.concatenate([kc, q_ref[rs, hc]], axis=0).astype(BF), kc.astype(BF)))
    for (rs, hc), kq_i in zip(chains, kq):
        gcol = gam_sc[rs, hc]
        diff = gcol[:, :c] - _pad_transpose(gcol)[:c, :c]
        dec = jnp.where(incl, jnp.exp(jnp.where(incl, diff, 0.0)), 0.0)
        decay.append(dec)
        pm.append(jnp.where(strict, -(beta_sc[rs, hc][:, :c] * kq_i[:c] * dec), 0.0))
    mk = pm
    for _ in range(n_double):
        mk = [_dot(a.astype(BF), a.astype(BF)) for a in mk]
        pm = [p + a + _dot(p.astype(BF), a.astype(BF)) for p, a in zip(pm, mk)]
    sol, lhs2 = [], []
    for (rs, hc), p, kq_i, dec in zip(chains, pm, kq, decay):
        rhs = jnp.concatenate([rv_sc[rs, hc], rk_sc[rs, hc]], axis=1)
        sol.append(rhs + _dot(p.astype(BF), rhs.astype(BF)))
        kdt = _pad_transpose(kdec_sc[rs, hc])[:, :c]
        lhs2.append(jnp.concatenate([kq_i[c:] * dec, kdt], axis=0).astype(BF))

    def advance(idxs, states):
        t1 = []
        for idx, s in zip(idxs, states):
            rs, hc = chains[idx]
            t1.append(_dot(jnp.concatenate([sol[idx][:, dk:], qg_sc[rs, hc]], axis=0).astype(BF), s.astype(BF)))
        t2 = [_dot(lhs2[idx], (sol[idx][:, :dk] - t[:c]).astype(BF)) for idx, t in zip(idxs, t1)]
        new_states = []
        for idx, s, ta, tb in zip(idxs, states, t1, t2):
            rs, hc = chains[idx]
            o_sc[rs, hc] = ta[c:] + tb[:c]
            egl = jnp.broadcast_to(egl_sc[rs.start:rs.start + 1, hc], (dk, dk))
            new_states.append(egl * s + tb[c:])
        return new_states

    if per_chunk_state:
        idxs = list(range(len(chains)))
        new_states = advance(idxs, [s0_ref[idx // nh, idx % nh] for idx in idxs])
        for idx, s_new in zip(idxs, new_states):
            s_ref[idx // nh, idx % nh] = s_new
    else:
        state = [s_ref[0, hh] for hh in range(nh)]
        for ci in range(n_chunks):
            state = advance([ci * nh + hh for hh in range(nh)], state)
        for hh in range(nh):
            s_ref[0, hh] = state[hh]

    outs = []
    for hh in range(nh):
        hc = slice(hh * dk, (hh + 1) * dk)
        outs.append(_rms(o_sc[:, hc], gn_ref[...]) * _silu(z_ref[:, hc]))
    yv = jnp.concatenate(outs, axis=-1)
    if fuse_out:
        o_ref[...] = x_ref[...] + _dot(yv.astype(BF), wout_ref[...])
    else:
        o_ref[...] = yv


def _gdn_consts(rows, c, hd):
    r = np.arange(rows)
    tri = ((r[:, None] >= r[None, :]) & (r[:, None] // c == r[None, :] // c)).astype(np.float32)
    lane = np.arange(LANES)[:, None]
    col = np.arange(2 * hd)[None, :]
    head = (col % hd) // GDN_DK
    e = np.where(col < hd, lane == head, lane == GDN_HEADS + head).astype(np.float32)
    return jnp.asarray(tri, BF), jnp.asarray(e, BF)


def gdn_scan(x, q, k, v, z, bg, s0, g_norm, w_out, *, c, rows, n_seq, per_chunk_state, name):
    t, hd = q.shape
    nt = t // n_seq // rows
    fuse_out = w_out is not None
    tri, e = _gdn_consts(rows, c, hd)
    ns = rows // c if per_chunk_state else 1
    row = lambda b, i: (b * nt + i, 0)
    tile = lambda n: pl.BlockSpec((rows, n), row)
    st_spec = pl.BlockSpec((ns,) + s0.shape[1:], lambda b, i: (b, 0, 0, 0))
    ins, specs = [], []
    if fuse_out:
        ins.append(x)
        specs.append(tile(x.shape[1]))
    ins += [q, k, v, z, bg, tri, e, s0, g_norm.reshape(1, GDN_DK)]
    specs += [tile(hd)] * 4 + [tile(LANES), _const_spec(tri.shape), _const_spec(e.shape), st_spec,
                               _const_spec((1, GDN_DK))]
    d_out = hd
    if fuse_out:
        ins.append(_warr(w_out))
        specs.append(_wspec(w_out))
        d_out = w_out.shape[1]
    return pl.pallas_call(
        functools.partial(_gdn_scan_body, c=c, per_chunk_state=per_chunk_state, fuse_out=fuse_out),
        grid=(n_seq, nt),
        in_specs=specs,
        out_specs=[tile(d_out), st_spec],
        out_shape=[SDS((t, d_out), F32), SDS(s0.shape, F32)],
        scratch_shapes=[pltpu.VMEM((rows, hd), F32)] * 8,
        compiler_params=_params(2),
        name=name,
    )(*ins)


def _topk_mask(sb, valid, axis):
    n = sb.shape[axis]
    idx = lax.broadcasted_iota(jnp.int32, sb.shape, axis)
    cnt = jnp.zeros(sb.shape, F32)
    for m in range(n):
        sm = lax.slice_in_dim(sb, m, m + 1, axis=axis)
        ahead = (sm > sb) | ((sm == sb) & (m < idx))
        cnt = cnt + jnp.where(ahead, 1.0, 0.0)
    return jnp.where((cnt < MOBA_TOPK) & valid, 1.0, 0.0)


def _moba_p_body(q_ref, k_ref, v_ref, oh_ref, o_ref):
    length, dh = k_ref.shape
    blk = MOBA_BLOCK
    nb = length // blk
    scale = dh ** -0.5
    k = k_ref[...]
    kaug = jnp.concatenate([k.astype(BF), oh_ref[...]], axis=1)
    vb = v_ref[...].astype(BF)
    km = jnp.sum(k.reshape(nb, blk, dh), axis=1) * (1.0 / blk)
    brow = lax.broadcasted_iota(jnp.int32, (nb, blk), 0)
    rr = lax.broadcasted_iota(jnp.int32, (blk, blk), 0)
    cc = lax.broadcasted_iota(jnp.int32, (blk, blk), 1)
    causal = cc <= rr
    zero_bias = jnp.zeros((blk, dh), BF)
    pad_rows = jnp.zeros((LANES - nb, blk), F32)
    qaug = []
    for i in range(nb):
        q = q_ref[i * blk:(i + 1) * blk, :]
        qs = (q * scale).astype(BF)
        if i == 0:
            qaug.append(jnp.concatenate([qs, zero_bias], axis=1))
            continue
        sbt = jnp.where(brow < i, _dot_hi(km, q, dot=_dot_nt), NEG_INF)
        selt = _topk_mask(sbt, sbt > NEG_INF / 2, axis=0)
        bias_t = jnp.where((selt > 0.5) | (brow == i), 0.0, NEG_INF)
        bias = jnp.concatenate([bias_t, pad_rows], axis=0).T
        qaug.append(jnp.concatenate([qs, bias.astype(BF)], axis=1))
    for i in range(nb):
        n_keys = (i + 1) * blk
        s = _dot_nt(qaug[i], kaug[:n_keys])
        s_own = jnp.where(causal, s[:, i * blk:], NEG_INF)
        s = jnp.concatenate([s[:, :i * blk], s_own], axis=-1) if i > 0 else s_own
        pe = jnp.exp(s - jnp.max(s, axis=-1, keepdims=True))
        l = jnp.sum(pe, axis=-1, keepdims=True)
        o_ref[i * blk:(i + 1) * blk, :] = _dot(pe.astype(BF), vb[:n_keys]) / l


def moba_prompt_core(q, k, v, bsz, name):
    t, w = q.shape
    dh = MOBA_DH
    nh = w // dh
    length = t // bsz
    nb = length // MOBA_BLOCK
    del nb
    expand = jnp.asarray(np.arange(length)[:, None] // MOBA_BLOCK == np.arange(dh)[None, :], BF)
    spec = pl.BlockSpec((length, dh), lambda b, h: (b, h))
    return pl.pallas_call(
        _moba_p_body,
        grid=(bsz, nh),
        in_specs=[spec, spec, spec, _const_spec(expand.shape)],
        out_specs=spec,
        out_shape=SDS((t, w), F32),
        compiler_params=_params(2),
        name=name,
    )(q, k, v, expand)


def _lane_group_reduce(row, op, ncol):
    a = jnp.broadcast_to(row, (SUBLANES, LANES))
    shift = ncol
    while shift < LANES:
        a = op(a, pltpu.roll(a, shift, 1))
        shift *= 2
    return a[:1]


def _moba_s_body(pt_ref, qt_ref, hb_ref, kn_ref, vn_ref, *refs, n_pages, ncol):
    del pt_ref
    kp = refs[:n_pages]
    vp = refs[n_pages:2 * n_pages]
    o_ref, s_sc = refs[2 * n_pages:]
    page, nh, dh = kp[0].shape[1:]
    bp = MOBA_BLOCK // page
    nb = n_pages // bp
    prow = page * nh
    pack = LANES // ncol
    qt = qt_ref[0]
    lane = lax.broadcasted_iota(jnp.int32, (1, LANES), 1)
    group = lane // ncol
    qs = qt * dh ** -0.5
    rhs = jnp.concatenate([jnp.where(group == g, qs, 0.0).astype(BF) for g in range(pack)], axis=0)

    km = []
    for n in range(nb):
        tot = kp[n * bp][0].sum(axis=0)
        for j in range(1, bp):
            tot = tot + kp[n * bp + j][0].sum(axis=0)
        km.append(tot * (1.0 / MOBA_BLOCK))
    km2 = jnp.concatenate(km, axis=0)
    sbm = _dot_hi(km2, qt)
    rh = lax.broadcasted_iota(jnp.int32, sbm.shape, 0) % nh
    ch = lax.broadcasted_iota(jnp.int32, sbm.shape, 1) % ncol % nh
    sb = jnp.sum(jnp.where(rh == ch, sbm, 0.0).reshape(nb, nh, LANES), axis=1)
    sel = _topk_mask(sb, jnp.full(sb.shape, True), axis=0)
    sel_bias = jnp.where(sel > 0.5, 0.0, NEG_INF)

    n_own = kn_ref.shape[1]
    ro = lax.broadcasted_iota(jnp.int32, (n_own, LANES), 0)
    co = lax.broadcasted_iota(jnp.int32, (n_own, LANES), 1)
    own_ok = (ro % nh == co % nh) & (ro // nh <= co // nh) & (co < ncol)
    s_own = jnp.where(own_ok, _dot(kn_ref[0].astype(BF), qs.astype(BF)), NEG_INF)
    mx = jnp.max(s_own, axis=0, keepdims=True)

    n_groups = n_pages // pack
    for j in range(n_groups):
        pages = range(j * pack, (j + 1) * pack)
        k4 = jnp.concatenate([kp[p][0].reshape(prow, dh).astype(BF) for p in pages], axis=1)
        bias = sel_bias[(j * pack) // bp:(j * pack) // bp + 1, :]
        for g in range(1, pack):
            blk = (j * pack + g) // bp
            bias = jnp.where(group >= g, sel_bias[blk:blk + 1, :], bias)
        s = _dot(k4, rhs) + (hb_ref[...] + bias)
        s_sc[j * prow:(j + 1) * prow, :] = s
        mx = jnp.maximum(mx, jnp.max(s, axis=0, keepdims=True))
    mx = _lane_group_reduce(mx, jnp.maximum, ncol)

    pe = jnp.exp(s_own - mx)
    l = jnp.sum(pe, axis=0, keepdims=True)
    acc = _dot_tn(pe.astype(BF), vn_ref[0].astype(BF))[:ncol]
    for j in range(n_groups):
        pages = range(j * pack, (j + 1) * pack)
        pe = jnp.exp(s_sc[j * prow:(j + 1) * prow, :] - mx)
        l = l + jnp.sum(pe, axis=0, keepdims=True)
        v4 = jnp.concatenate([vp[p][0].reshape(prow, dh).astype(BF) for p in pages], axis=1)
        r = _dot_tn(pe.astype(BF), v4)
        for g in range(pack):
            acc = acc + r[g * ncol:(g + 1) * ncol, g * dh:(g + 1) * dh]
    l = _lane_group_reduce(l, jnp.add, ncol)
    l_col = jnp.broadcast_to(l, (LANES, LANES)).T[:ncol, :1]
    o_ref[0] = acc / l_col


def moba_sample_core(qt, kn, vn, pool_k, pool_v, page_table, name):
    bsz, dh, ncol = qt.shape
    n_pages = page_table.shape[1]
    page, nh = pool_k.shape[1:3]
    n_own = kn.shape[1]
    pack = LANES // ncol
    qt = jnp.tile(qt, (1, 1, pack))
    r = np.arange(page * nh)[:, None]
    c = np.arange(LANES)[None, :]
    head_bias = jnp.asarray(np.where(r % nh == c % ncol % nh, 0.0, NEG_INF), F32)

    def page_spec(p):
        return pl.BlockSpec((1, page, nh, dh), lambda b, pt: (pt[b, p], 0, 0, 0))

    grid_spec = pltpu.PrefetchScalarGridSpec(
        num_scalar_prefetch=1,
        grid=(bsz,),
        in_specs=[pl.BlockSpec((1, dh, LANES), lambda b, pt: (b, 0, 0)),
                  pl.BlockSpec(head_bias.shape, lambda b, pt: (0, 0)),
                  pl.BlockSpec((1, n_own, dh), lambda b, pt: (b, 0, 0)),
                  pl.BlockSpec((1, n_own, dh), lambda b, pt: (b, 0, 0))]
                 + [page_spec(p) for p in range(n_pages)] * 2,
        out_specs=pl.BlockSpec((1, ncol, dh), lambda b, pt: (b, 0, 0)),
        scratch_shapes=[pltpu.VMEM((n_pages // pack * page * nh, LANES), F32)],
    )
    return pl.pallas_call(
        functools.partial(_moba_s_body, n_pages=n_pages, ncol=ncol),
        grid_spec=grid_spec,
        out_shape=SDS((bsz, ncol, dh), F32),
        compiler_params=_params(1),
        name=name,
    )(page_table, qt, head_bias, kn, vn, *([pool_k] * n_pages), *([pool_v] * n_pages))


def _to_pm(a):
    b, l, c = a.shape
    return jnp.transpose(a, (1, 0, 2)).reshape(l * b, c)


def _from_pm(a, bsz):
    n, c = a.shape
    return jnp.transpose(a.reshape(n // bsz, bsz, c), (1, 0, 2))


def _pad_lanes(vec, offset):
    out = jnp.zeros((1, LANES), F32)
    return lax.dynamic_update_slice(out, vec.reshape(1, -1).astype(F32), (0, offset))


def kernel(x_prompt, x_sample, state_a_conv, state_b_conv, state_b_rec, cache_c_k, cache_c_v, cache_mem_k, cache_mem_v, page_table, mem_prompt, norm_mix, norm_mem, norm_xattn, norm_ffn, norm_final, a_w_in, a_w_conv, a_w_out, b_w_in, b_w_conv, b_a_log, b_dt_bias, b_norm, b_w_out, c_w_qkv, c_w_out, x_w_q, x_w_kv, x_w_o, f_w_up, f_w_down):
    bp, seq, d = x_prompt.shape
    bs, dseq, _ = x_sample.shape
    depth = norm_mix.shape[0]
    n_mixers = 3
    tm = 512
    ts = bs * dseq

    bf = lambda w: w.astype(BF)
    a_w_in, a_w_out, b_w_out, c_w_qkv, c_w_out = map(bf, (a_w_in, a_w_out, b_w_out, c_w_qkv, c_w_out))
    x_w_q, x_w_kv, x_w_o, f_w_up, f_w_down = map(bf, (x_w_q, x_w_kv, x_w_o, f_w_up, f_w_down))
    cc = b_w_conv.shape[-1]
    hd = GDN_HEADS * GDN_DK
    b_w_qkvz = bf(b_w_in[:, :, :cc + hd])
    b_w_ba = bf(jnp.pad(b_w_in[:, :, cc + hd:], ((0, 0), (0, 0), (0, LANES - 2 * GDN_HEADS))))

    xp = x_prompt.reshape(bp * seq, d)
    xs = _to_pm(x_sample)

    mem_len = mem_prompt.shape[1]
    mk32, mv32, mkbf, mvbf = memory_kv_all(mem_prompt.reshape(bp * mem_len, d), norm_mem, x_w_kv, XA_HEADS, tm)

    a_p, a_s, bc_p, bc_s, br_p, br_s = [], [], [], [], [], []
    ck_p, cv_p, ck_s, cv_s = [], [], [], []
    for i in range(depth):
        kind, j = i % n_mixers, i // n_mixers
        tag = f"l{i}"
        if kind == 0:
            xp, st = mixer_a_prompt(xp, norm_mix[i], _Layer(a_w_in, j), a_w_conv[j], _Layer(a_w_out, j), bp, tm, tag + "_mixa_p")
            a_p.append(st)
            st_pm = _to_pm(state_a_conv[j])
            xs, nst = mixer_a_sample(xs, st_pm, norm_mix[i], _Layer(a_w_in, j), a_w_conv[j], _Layer(a_w_out, j), bs, tag + "_mixa_s")
            a_s.append(_from_pm(nst, bs))
        elif kind == 1:
            alog = _pad_lanes(b_a_log[j], GDN_HEADS)
            dtb = _pad_lanes(b_dt_bias[j], GDN_HEADS)
            q, k, v, z, bg, st = gdn_proj_prompt(xp, norm_mix[i], _Layer(b_w_qkvz, j), _Layer(b_w_ba, j), b_w_conv[j], alog, dtb,
                                                 bp, 256, tag + "_gdnproj_p")
            bc_p.append(st)
            s0 = jnp.zeros((bp, GDN_HEADS, GDN_DK, GDN_DK), F32)
            xp, s_fin = gdn_scan(xp, q, k, v, z, bg, s0, b_norm[j], _Layer(b_w_out, j), c=math.gcd(seq, GDN_CHUNK),
                                 rows=256, n_seq=bp, per_chunk_state=False, name=tag + "_gdnscan_p")
            br_p.append(s_fin)
            st_pm = _to_pm(state_b_conv[j])
            q, k, v, z, bg, nst = gdn_proj_sample(xs, st_pm, norm_mix[i], _Layer(b_w_qkvz, j), _Layer(b_w_ba, j), b_w_conv[j],
                                                  alog, dtb, bs, tag + "_gdnproj_s")
            bc_s.append(_from_pm(nst, bs))
            cpad = SUBLANES
            padded = lambda a: jnp.pad(_from_pm(a, bs), ((0, 0), (0, cpad - dseq), (0, 0))).reshape(bs * cpad, -1)
            nbatch = 8
            y, s_fin = gdn_scan(None, padded(q), padded(k), padded(v), padded(z), padded(bg), state_b_rec[j],
                                b_norm[j], None, c=cpad, rows=nbatch * cpad, n_seq=bs // nbatch,
                                per_chunk_state=True, name=tag + "_gdnscan_s")
            br_s.append(s_fin)
            y = _to_pm(y.reshape(bs, cpad, hd)[:, :dseq])
            xs = mm_res(y, _Layer(b_w_out, j), xs, ts, tag + "_gdnout_s")
        else:
            q, k, v = norm_mm(xp, norm_mix[i], _Layer(c_w_qkv, j), 3, tm, tag + "_mobaqkv_p")
            ck_p.append(k.reshape(bp, seq, MOBA_HEADS, MOBA_DH))
            cv_p.append(v.reshape(bp, seq, MOBA_HEADS, MOBA_DH))
            o = moba_prompt_core(q, k, v, bp, tag + "_moba_p")
            xp = mm_res(o, _Layer(c_w_out, j), xp, tm, tag + "_mobaout_p")
            q, k, v = norm_mm(xs, norm_mix[i], _Layer(c_w_qkv, j), 3, ts, tag + "_mobaqkv_s")
            kn = _from_pm(k, bs)
            vn = _from_pm(v, bs)
            ck_s.append(kn.reshape(bs, dseq, MOBA_HEADS, MOBA_DH))
            cv_s.append(vn.reshape(bs, dseq, MOBA_HEADS, MOBA_DH))
            qt = jnp.transpose(_from_pm(q, bs).reshape(bs, dseq * MOBA_HEADS, MOBA_DH), (0, 2, 1))
            rows_kv = lambda a: a.reshape(bs, dseq * MOBA_HEADS, MOBA_DH)
            o = moba_sample_core(qt, rows_kv(kn), rows_kv(vn), cache_c_k[j], cache_c_v[j], page_table, tag + "_moba_s")
            xs = mm_res(_to_pm(o.reshape(bs, dseq, MOBA_HEADS * MOBA_DH)), _Layer(c_w_out, j), xs, ts, tag + "_mobaout_s")

        (qs,) = norm_mm(xs, norm_xattn[i], _Layer(x_w_q, i), 1, ts, tag + "_xattnq_s")
        dh = d // XA_HEADS
        qt = jnp.transpose(_from_pm(qs, bs).reshape(bs, dseq * XA_HEADS, dh), (0, 2, 1))
        xp, o = xattn_both(xp, norm_xattn[i], _Layer(x_w_q, i), mkbf, mvbf, i, _Layer(x_w_o, i),
                           qt, cache_mem_k, cache_mem_v, bp, tm, tag + "_xattn")
        xs = mm_res(_to_pm(o.reshape(bs, dseq, d)), _Layer(x_w_o, i), xs, ts, tag + "_xattnout_s")

        last = i == depth - 1
        xp, xs = ffn(xp, xs, norm_ffn[i], _Layer(f_w_up, i), _Layer(f_w_down, i), norm_final, last, tm, tag + "_ffn")

    mem_shape = (depth, bp, mem_len, XA_HEADS, d // XA_HEADS)
    return (xp.reshape(bp, seq, d), _from_pm(xs, bs),
            jnp.stack(a_p), jnp.stack(a_s),
            jnp.stack(bc_p), jnp.stack(bc_s),
            jnp.stack(br_p), jnp.stack(br_s),
            jnp.stack(ck_p), jnp.stack(cv_p), jnp.stack(ck_s), jnp.stack(cv_s),
            mk32.reshape(mem_shape), mv32.reshape(mem_shape))
```

```python
import functools
import math

import numpy as np
import jax
import jax.numpy as jnp
from jax import lax
from jax.experimental import pallas as pl
from jax.experimental.pallas import tpu as pltpu

F32 = jnp.float32
BF = jnp.bfloat16
SDS = jax.ShapeDtypeStruct

EPS = 1e-6
NEG_INF = -1e30
LANES = 128
SUBLANES = 8
VMEM_LIMIT = 56 * 1024 * 1024

GDN_HEADS = 8
GDN_DK = 128
GDN_CHUNK = 64
MOBA_HEADS = 8
MOBA_DH = 128
MOBA_BLOCK = 256
MOBA_TOPK = 3
PAGE_SIZE = 128
XA_HEADS = 4


def _params(n_axes):
    return pltpu.CompilerParams(dimension_semantics=("arbitrary",) * n_axes,
                                vmem_limit_bytes=VMEM_LIMIT)


def _const_spec(shape):
    nd = len(shape)
    return pl.BlockSpec(shape, lambda *_: (0,) * nd, pipeline_mode=pl.Buffered(1))


class _Layer:
    def __init__(self, arr, j):
        self.arr, self.j = arr, j

    @property
    def shape(self):
        return self.arr.shape[1:]


def _wspec(w):
    if not isinstance(w, _Layer):
        return _const_spec(w.shape)
    nd = len(w.shape)
    j = w.j
    return pl.BlockSpec((None,) + w.shape, lambda *_: (j,) + (0,) * nd, pipeline_mode=pl.Buffered(1))


def _warr(w):
    return w.arr if isinstance(w, _Layer) else w


def _whole_spec(shape):
    nd = len(shape)
    return pl.BlockSpec(shape, lambda *_: (0,) * nd)


def _dot(a, b):
    return jnp.dot(a, b, preferred_element_type=F32)


def _dot_nt(a, b):
    return lax.dot_general(a, b, (((1,), (1,)), ((), ())), preferred_element_type=F32)


def _dot_tn(a, b):
    return lax.dot_general(a, b, (((0,), (0,)), ((), ())), preferred_element_type=F32)


def _split3(a):
    a0 = a.astype(BF)
    r = a - a0.astype(F32)
    a1 = r.astype(BF)
    a2 = (r - a1.astype(F32)).astype(BF)
    return a0, a1, a2


def _dot_exact_rhs(a, b_bf, dot=_dot):
    a0, a1, a2 = _split3(a)
    return dot(a0, b_bf) + dot(a1, b_bf) + dot(a2, b_bf)


def _dot_hi(a, b, dot=_dot):
    a0, a1, _ = _split3(a)
    b0, b1, _ = _split3(b)
    return dot(a0, b0) + (dot(a0, b1) + dot(a1, b0))


def _rms(x, g):
    ms = jnp.mean(x * x, axis=-1, keepdims=True)
    return x * lax.rsqrt(ms + EPS) * g


def _silu(x):
    return x * jax.nn.sigmoid(x)


def _softplus(x):
    return jnp.maximum(x, 0.0) + jnp.log1p(jnp.exp(-jnp.abs(x)))


def _norm_mm_body(x_ref, g_ref, w_ref, *o_refs):
    h = _rms(x_ref[...], g_ref[...]).astype(BF)
    n = o_refs[0].shape[-1]
    for j, o_ref in enumerate(o_refs):
        o_ref[...] = _dot(h, w_ref[:, j * n:(j + 1) * n])


def norm_mm(x, g, w, n_out, tm, name):
    t, k = x.shape
    n = w.shape[1] // n_out
    return pl.pallas_call(
        _norm_mm_body,
        grid=(t // tm,),
        in_specs=[pl.BlockSpec((tm, k), lambda i: (i, 0)), _const_spec((1, k)), _wspec(w)],
        out_specs=[pl.BlockSpec((tm, n), lambda i: (i, 0))] * n_out,
        out_shape=[SDS((t, n), F32)] * n_out,
        compiler_params=_params(1),
        name=name,
    )(x, g.reshape(1, k), _warr(w))


def _mm_res_body(a_ref, w_ref, r_ref, o_ref):
    o_ref[...] = r_ref[...] + _dot(a_ref[...].astype(BF), w_ref[...])


def mm_res(a, w, res, tm, name):
    t, k = a.shape
    n = w.shape[1]
    return pl.pallas_call(
        _mm_res_body,
        grid=(t // tm,),
        in_specs=[pl.BlockSpec((tm, k), lambda i: (i, 0)), _wspec(w),
                  pl.BlockSpec((tm, n), lambda i: (i, 0))],
        out_specs=pl.BlockSpec((tm, n), lambda i: (i, 0)),
        out_shape=SDS((t, n), F32),
        compiler_params=_params(1),
        name=name,
    )(a, _warr(w), res)


def _memkv_body(x_ref, g_ref, w_ref, k32_ref, v32_ref, kbf_ref, vbf_ref):
    h = _rms(x_ref[...], g_ref[0]).astype(BF)
    d = kbf_ref.shape[-1]
    nh, dh = k32_ref.shape[2:]
    for j, (split_ref, bf_ref) in enumerate(((k32_ref, kbf_ref), (v32_ref, vbf_ref))):
        r = _dot(h, w_ref[0, :, j * d:(j + 1) * d])
        bf_ref[0] = r.astype(BF)
        for hh in range(nh):
            split_ref[0, :, hh, :] = r[:, hh * dh:(hh + 1) * dh]


def memory_kv_all(mem2d, norm_mem, w_kv, nh, tm):
    depth, d, _ = w_kv.shape
    t = mem2d.shape[0]
    flat = pl.BlockSpec((1, tm, d), lambda l, i: (l, i, 0))
    split = pl.BlockSpec((1, tm, nh, d // nh), lambda l, i: (l, i, 0, 0))
    return pl.pallas_call(
        _memkv_body,
        grid=(depth, t // tm),
        in_specs=[pl.BlockSpec((tm, d), lambda l, i: (i, 0)),
                  pl.BlockSpec((1, 1, d), lambda l, i: (l, 0, 0)),
                  pl.BlockSpec((1, d, 2 * d), lambda l, i: (l, 0, 0))],
        out_specs=[split, split, flat, flat],
        out_shape=[SDS((depth, t, nh, d // nh), F32)] * 2 + [SDS((depth, t, d), BF)] * 2,
        compiler_params=_params(2),
        name="memory_kv",
    )(mem2d, norm_mem.reshape(depth, 1, d), w_kv)


def _ffn_body(xp_ref, xs_ref, g_ref, wu_ref, wd_ref, gf_ref, op_ref, os_ref, *, ck, final, ntp):
    def tile(x_ref, o_ref):
        x = x_ref[...]
        h = _rms(x, g_ref[...]).astype(BF)
        dff = wd_ref.shape[0]
        acc = x
        for c0 in range(0, dff, ck):
            gate = _dot(h, wu_ref[:, c0:c0 + ck])
            up = _dot(h, wu_ref[:, dff + c0:dff + c0 + ck])
            a = (_silu(gate) * up).astype(BF)
            acc = acc + _dot(a, wd_ref[c0:c0 + ck, :])
        if final:
            acc = _rms(acc, gf_ref[...])
        o_ref[...] = acc

    i = pl.program_id(0)
    pl.when(i < ntp)(lambda: tile(xp_ref, op_ref))
    pl.when(i == ntp)(lambda: tile(xs_ref, os_ref))


def ffn(xp, xs, g, w_up, w_down, g_final, final, tm, name):
    t, d = xp.shape
    ntp = t // tm
    body = functools.partial(_ffn_body, ck=2 * LANES, final=final, ntp=ntp)
    p_spec = pl.BlockSpec((tm, d), lambda i: (jnp.minimum(i, ntp - 1), 0))
    s_spec = pl.BlockSpec(xs.shape, lambda i: (0, 0))
    return pl.pallas_call(
        body,
        grid=(ntp + 1,),
        in_specs=[p_spec, s_spec, _const_spec((1, d)), _wspec(w_up), _wspec(w_down), _const_spec((1, d))],
        out_specs=[p_spec, s_spec],
        out_shape=[SDS((t, d), F32), SDS(xs.shape, F32)],
        compiler_params=_params(1),
        name=name,
    )(xp, xs, g.reshape(1, d), _warr(w_up), _warr(w_down), g_final.reshape(1, d))


def _xattn_p_stages(x_ref, g_ref, wq_ref, k_ref, v_ref, wo_ref, o_ref, *, nh):
    x = x_ref[...]
    d = x.shape[-1]
    dh = d // nh
    h = _rms(x, g_ref[...]).astype(BF)
    q = (_dot(h, wq_ref[...]) * dh ** -0.5).astype(BF)
    yield
    outs = []
    for hh in range(nh):
        cs = slice(hh * dh, (hh + 1) * dh)
        s = _dot_nt(q[:, cs], k_ref[0, :, cs])
        p = jnp.exp(s - jnp.max(s, axis=-1, keepdims=True))
        l = jnp.sum(p, axis=-1, keepdims=True)
        outs.append((_dot(p.astype(BF), v_ref[0, :, cs]) / l).astype(BF))
        yield
    o_ref[...] = x + _dot(jnp.concatenate(outs, axis=-1), wo_ref[...])


def _xattn_s_stages(qt_ref, k_ref, v_ref, o_ref, *, nb, nh):
    m_len, _, dh = k_ref.shape[2:]
    rows = m_len * nh
    ncol = qt_ref.shape[1]
    rh = lax.broadcasted_iota(jnp.int32, (rows, ncol), 0) % nh
    ch = lax.broadcasted_iota(jnp.int32, (rows, ncol), 1) % nh
    head_bias = jnp.where(rh == ch, 0.0, NEG_INF)
    s = [_dot_nt(k_ref[0, j].reshape(rows, dh).astype(BF), (qt_ref[j] * dh ** -0.5).astype(BF)) + head_bias
         for j in range(nb)]
    yield
    p = [jnp.exp(a - jnp.max(a, axis=0, keepdims=True)) for a in s]
    yield
    p = [(a / jnp.sum(a, axis=0, keepdims=True)).astype(BF) for a in p]
    yield
    for j in range(nb):
        o_ref[j] = _dot_tn(p[j], v_ref[0, j].reshape(rows, dh).astype(BF))


def _interleave(*stage_generators):
    live = list(stage_generators)
    while live:
        for gen in list(live):
            if next(gen, StopIteration) is StopIteration:
                live.remove(gen)


def _xattn_both_body(x_ref, g_ref, wq_ref, k_ref, v_ref, wo_ref, qt_ref, ck_ref, cv_ref, o_ref, os_ref, *, nb):
    _interleave(_xattn_p_stages(x_ref, g_ref, wq_ref, k_ref, v_ref, wo_ref, o_ref, nh=XA_HEADS),
                _xattn_s_stages(qt_ref, ck_ref, cv_ref, os_ref, nb=nb, nh=ck_ref.shape[3]))


def xattn_both(x, g, w_q, k_bf, v_bf, layer, w_o, qt, cache_k, cache_v, bsz, tm, name):
    t, d = x.shape
    nt = t // bsz // tm
    m_len = k_bf.shape[1] // bsz
    bs, ncol, dh = qt.shape
    nh_s = cache_k.shape[3]
    steps = bsz * nt
    assert bs % steps == 0, (bs, steps)
    nb = bs // steps
    kv = lambda a: a.reshape(a.shape[0] * bsz, m_len, d)
    row = lambda b, i: (b * nt + i, 0)
    cache_spec = pl.BlockSpec((1, nb, m_len, nh_s, dh), lambda b, i: (layer, b * nt + i, 0, 0, 0))
    return pl.pallas_call(
        functools.partial(_xattn_both_body, nb=nb),
        grid=(bsz, nt),
        in_specs=[pl.BlockSpec((tm, d), row), _const_spec((1, d)), _wspec(w_q),
                  pl.BlockSpec((1, m_len, d), lambda b, i: (layer * bsz + b, 0, 0)),
                  pl.BlockSpec((1, m_len, d), lambda b, i: (layer * bsz + b, 0, 0)),
                  _wspec(w_o),
                  pl.BlockSpec((nb, ncol, dh), lambda b, i: (b * nt + i, 0, 0)), cache_spec, cache_spec],
        out_specs=[pl.BlockSpec((tm, d), row), pl.BlockSpec((nb, ncol, dh), lambda b, i: (b * nt + i, 0, 0))],
        out_shape=[SDS((t, d), F32), SDS((bs, ncol, dh), F32)],
        compiler_params=_params(2),
        name=name,
    )(x, g.reshape(1, d), _warr(w_q), kv(k_bf), kv(v_bf), _warr(w_o), qt, cache_k, cache_v)


def _mixa_p_body(x_ref, g_ref, win_ref, wc_ref, wout_ref, o_ref, st_ref, zbuf, *, nt):
    i = pl.program_id(1)
    tm, d = x_ref.shape
    width = wc_ref.shape[0]

    @pl.when(i == 0)
    def _():
        zbuf[0:SUBLANES, :] = jnp.zeros((SUBLANES, d), F32)

    x = x_ref[...]
    h = _rms(x, g_ref[...]).astype(BF)
    gate_b = _dot(h, win_ref[:, 0:d])
    zn = _dot(h, win_ref[:, d:2 * d]) * _dot(h, win_ref[:, 2 * d:3 * d])
    zbuf[SUBLANES:SUBLANES + tm, :] = zn
    conv = wc_ref[width - 1:width, :] * zn
    for j in range(width - 1):
        off = SUBLANES - (width - 1) + j
        conv = conv + wc_ref[j:j + 1, :] * zbuf[off:off + tm, :]
    o_ref[...] = x + _dot((gate_b * conv).astype(BF), wout_ref[...])
    zbuf[0:SUBLANES, :] = zbuf[tm:tm + SUBLANES, :]

    @pl.when(i == nt - 1)
    def _():
        st_ref[0] = zbuf[SUBLANES - (width - 1):SUBLANES, :]


def mixer_a_prompt(x, g, w_in, w_conv, w_out, bsz, tm, name):
    t, d = x.shape
    nt = t // bsz // tm
    width = w_conv.shape[0]
    row = lambda b, i: (b * nt + i, 0)
    return pl.pallas_call(
        functools.partial(_mixa_p_body, nt=nt),
        grid=(bsz, nt),
        in_specs=[pl.BlockSpec((tm, d), row), _const_spec((1, d)), _wspec(w_in),
                  _const_spec(w_conv.shape), _wspec(w_out)],
        out_specs=[pl.BlockSpec((tm, d), row), pl.BlockSpec((1, width - 1, d), lambda b, i: (b, 0, 0))],
        out_shape=[SDS((t, d), F32), SDS((bsz, width - 1, d), F32)],
        scratch_shapes=[pltpu.VMEM((tm + SUBLANES, d), F32)],
        compiler_params=_params(2),
        name=name,
    )(x, g.reshape(1, d), _warr(w_in), w_conv, _warr(w_out))


def _posmajor_conv(state, z, wc_ref, bsz):
    width = wc_ref.shape[0]
    n = z.shape[0]
    zpad = jnp.concatenate([state, z], axis=0)
    conv = wc_ref[0:1, :] * zpad[0:n]
    for j in range(1, width):
        conv = conv + wc_ref[j:j + 1, :] * zpad[j * bsz:j * bsz + n]
    return conv, zpad[n:]


def _mixa_s_body(x_ref, st_ref, g_ref, win_ref, wc_ref, wout_ref, o_ref, nst_ref, *, bsz):
    x = x_ref[...]
    d = x.shape[-1]
    h = _rms(x, g_ref[...]).astype(BF)
    gate_b = _dot(h, win_ref[:, 0:d])
    zn = _dot(h, win_ref[:, d:2 * d]) * _dot(h, win_ref[:, 2 * d:3 * d])
    conv, new_state = _posmajor_conv(st_ref[...], zn, wc_ref, bsz)
    o_ref[...] = x + _dot((gate_b * conv).astype(BF), wout_ref[...])
    nst_ref[...] = new_state


def mixer_a_sample(x, state_pm, g, w_in, w_conv, w_out, bsz, name):
    t, d = x.shape
    return pl.pallas_call(
        functools.partial(_mixa_s_body, bsz=bsz),
        grid=(1,),
        in_specs=[_const_spec(x.shape), _const_spec(state_pm.shape), _const_spec((1, d)),
                  _wspec(w_in), _const_spec(w_conv.shape), _wspec(w_out)],
        out_specs=[_whole_spec(x.shape), _whole_spec(state_pm.shape)],
        out_shape=[SDS(x.shape, F32), SDS(state_pm.shape, F32)],
        compiler_params=_params(1),
        name=name,
    )(x, state_pm, g.reshape(1, d), _warr(w_in), w_conv, _warr(w_out))


GDN_COL_CHUNK = 2 * LANES


def _gdn_qkv_chunk(conv, c0, q_ref, k_ref, v_ref):
    nh, dk = GDN_HEADS, GDN_DK
    hd = nh * dk
    act = _silu(conv)
    section, col = c0 // hd, c0 % hd
    if section == 2:
        v_ref[:, col:col + GDN_COL_CHUNK] = act
        return
    ref = q_ref if section == 0 else k_ref
    for j in range(GDN_COL_CHUNK // dk):
        t = act[:, j * dk:(j + 1) * dk]
        t = t * lax.rsqrt(jnp.sum(t * t, axis=-1, keepdims=True) + EPS)
        ref[:, col + j * dk:col + (j + 1) * dk] = t * dk ** -0.5 if section == 0 else t


def _gdn_gates(ba, alog_ref, dtb_ref, bg_ref):
    nh = GDN_HEADS
    lane = lax.broadcasted_iota(jnp.int32, ba.shape, 1)
    beta = jax.nn.sigmoid(ba)
    g = -jnp.exp(alog_ref[...]) * _softplus(ba + dtb_ref[...])
    bg_ref[...] = jnp.where(lane < nh, beta, jnp.where(lane < 2 * nh, g, 0.0))


def _gdn_proj_p_body(x_ref, g_ref, w_ref, wba_ref, wc_ref, alog_ref, dtb_ref,
                     q_ref, k_ref, v_ref, z_ref, bg_ref, st_ref, cbuf, *, nt):
    i = pl.program_id(1)
    tm = x_ref.shape[0]
    width, cc = wc_ref.shape
    ck = GDN_COL_CHUNK

    @pl.when(i == 0)
    def _():
        cbuf[0:SUBLANES, :] = jnp.zeros((SUBLANES, cc), F32)

    h = _rms(x_ref[...], g_ref[...]).astype(BF)
    for c0 in range(0, cc, ck):
        cols = slice(c0, c0 + ck)
        pre = _dot(h, w_ref[:, cols])
        cbuf[SUBLANES:SUBLANES + tm, cols] = pre
        conv = wc_ref[width - 1:width, cols] * pre
        for j in range(width - 1):
            off = SUBLANES - (width - 1) + j
            conv = conv + wc_ref[j:j + 1, cols] * cbuf[off:off + tm, cols]
        _gdn_qkv_chunk(conv, c0, q_ref, k_ref, v_ref)
    for c0 in range(0, z_ref.shape[1], ck):
        z_ref[:, c0:c0 + ck] = _dot(h, w_ref[:, cc + c0:cc + c0 + ck])
    _gdn_gates(_dot(h, wba_ref[...]), alog_ref, dtb_ref, bg_ref)
    cbuf[0:SUBLANES, :] = cbuf[tm:tm + SUBLANES, :]

    @pl.when(i == nt - 1)
    def _():
        st_ref[0] = cbuf[SUBLANES - (width - 1):SUBLANES, :]


def gdn_proj_prompt(x, g, w_qkvz, w_ba, w_conv, alog, dtb, bsz, tm, name):
    t, d = x.shape
    nt = t // bsz // tm
    width, cc = w_conv.shape
    hd = GDN_HEADS * GDN_DK
    row = lambda b, i: (b * nt + i, 0)
    tile = lambda n: pl.BlockSpec((tm, n), row)
    return pl.pallas_call(
        functools.partial(_gdn_proj_p_body, nt=nt),
        grid=(bsz, nt),
        in_specs=[tile(d), _const_spec((1, d)), _wspec(w_qkvz), _wspec(w_ba),
                  _const_spec(w_conv.shape), _const_spec(alog.shape), _const_spec(dtb.shape)],
        out_specs=[tile(hd), tile(hd), tile(hd), tile(hd), tile(LANES),
                   pl.BlockSpec((1, width - 1, cc), lambda b, i: (b, 0, 0))],
        out_shape=[SDS((t, hd), F32)] * 4 + [SDS((t, LANES), F32), SDS((bsz, width - 1, cc), F32)],
        scratch_shapes=[pltpu.VMEM((tm + SUBLANES, cc), F32)],
        compiler_params=_params(2),
        name=name,
    )(x, g.reshape(1, d), _warr(w_qkvz), _warr(w_ba), w_conv, alog, dtb)


def _gdn_proj_s_body(x_ref, st_ref, g_ref, w_ref, wba_ref, wc_ref, alog_ref, dtb_ref,
                     q_ref, k_ref, v_ref, z_ref, bg_ref, nst_ref, *, bsz):
    cc = wc_ref.shape[1]
    h = _rms(x_ref[...], g_ref[...]).astype(BF)
    qkv = _dot(h, w_ref[:, :cc])
    z_ref[...] = _dot(h, w_ref[:, cc:cc + z_ref.shape[1]])
    ba = _dot(h, wba_ref[...])
    conv, new_state = _posmajor_conv(st_ref[...], qkv, wc_ref, bsz)
    for c0 in range(0, cc, GDN_COL_CHUNK):
        _gdn_qkv_chunk(conv[:, c0:c0 + GDN_COL_CHUNK], c0, q_ref, k_ref, v_ref)
    _gdn_gates(ba, alog_ref, dtb_ref, bg_ref)
    nst_ref[...] = new_state


def gdn_proj_sample(x, state_pm, g, w_qkvz, w_ba, w_conv, alog, dtb, bsz, name):
    t, d = x.shape
    cc = w_conv.shape[1]
    hd = GDN_HEADS * GDN_DK
    ins = (x, state_pm, g.reshape(1, d), w_qkvz, w_ba, w_conv, alog, dtb)
    outs = [SDS((t, hd), F32)] * 4 + [SDS((t, LANES), F32), SDS(state_pm.shape, F32)]
    return pl.pallas_call(
        functools.partial(_gdn_proj_s_body, bsz=bsz),
        grid=(1,),
        in_specs=[_wspec(a) for a in ins],
        out_specs=[_whole_spec(o.shape) for o in outs],
        out_shape=outs,
        compiler_params=_params(1),
        name=name,
    )(*[_warr(a) for a in ins])


def _pad_transpose(a):
    c = a.shape[0]
    if c < LANES:
        a = jnp.concatenate([a, jnp.zeros((LANES - c, a.shape[1]), a.dtype)], axis=0)
    return a.T


def _gdn_scan_body(*refs, c, per_chunk_state, fuse_out):
    if fuse_out:
        x_ref, refs = refs[0], refs[1:]
    (q_ref, k_ref, v_ref, z_ref, bg_ref, tri_ref, e_ref, s0_ref, gn_ref) = refs[:9]
    refs = refs[9:]
    if fuse_out:
        wout_ref, refs = refs[0], refs[1:]
    o_ref, s_ref, qg_sc, kdec_sc, rk_sc, rv_sc, gam_sc, beta_sc, egl_sc, o_sc = refs

    nh, dk = GDN_HEADS, GDN_DK
    rows, hd = q_ref.shape
    n_chunks = rows // c
    n_double = int(math.log2(c)) - 1

    if not per_chunk_state:
        @pl.when(pl.program_id(1) == 0)
        def _():
            s_ref[...] = s0_ref[...]

    bg = bg_ref[...]
    lane = lax.broadcasted_iota(jnp.int32, bg.shape, 1)
    gam = _dot_exact_rhs(bg, tri_ref[...], dot=lambda a, b: _dot(b, a))
    y = jnp.where(lane < nh, bg, gam)
    yb = _dot_exact_rhs(y, e_ref[...])
    beta_b = yb[:, :hd]
    gam_b = yb[:, hd:]
    gl_b = jnp.broadcast_to(gam_b.reshape(n_chunks, c, hd)[:, c - 1:c, :], (n_chunks, c, hd)).reshape(rows, hd)
    eg = jnp.exp(gam_b)
    k = k_ref[...]
    qg_sc[...] = q_ref[...] * eg
    rk_sc[...] = beta_b * eg * k
    rv_sc[...] = beta_b * v_ref[...]
    kdec_sc[...] = k * jnp.exp(gl_b - gam_b)
    gam_sc[...] = gam_b
    beta_sc[...] = beta_b
    egl_sc[...] = jnp.exp(gl_b)

    ii = lax.broadcasted_iota(jnp.int32, (c, c), 0)
    jj = lax.broadcasted_iota(jnp.int32, (c, c), 1)
    incl = ii >= jj
    strict = ii > jj

    chains = [(slice(ci * c, (ci + 1) * c), slice(hh * dk, (hh + 1) * dk))
              for ci in range(n_chunks) for hh in range(nh)]
    kq, decay, pm = [], [], []
    for rs, hc in chains:
        kc = k_ref[rs, hc]
        kq.append(_dot_nt(jnp.concatenate([kc, q_ref[rs, hc]], axis=0).astype(BF), kc.astype(BF)))
    for (rs, hc), kq_i in zip(chains, kq):
        gcol = gam_sc[rs, hc]
        diff = gcol[:, :c] - _pad_transpose(gcol)[:c, :c]
        dec = jnp.where(incl, jnp.exp(jnp.where(incl, diff, 0.0)), 0.0)
        decay.append(dec)
        pm.append(jnp.where(strict, -(beta_sc[rs, hc][:, :c] * kq_i[:c] * dec), 0.0))
    mk = pm
    for _ in range(n_double):
        mk = [_dot(a.astype(BF), a.astype(BF)) for a in mk]
        pm = [p + a + _dot(p.astype(BF), a.astype(BF)) for p, a in zip(pm, mk)]
    sol, lhs2 = [], []
    for (rs, hc), p, kq_i, dec in zip(chains, pm, kq, decay):
        rhs = jnp.concatenate([rv_sc[rs, hc], rk_sc[rs, hc]], axis=1)
        sol.append(rhs + _dot(p.astype(BF), rhs.astype(BF)))
        kdt = _pad_transpose(kdec_sc[rs, hc])[:, :c]
        lhs2.append(jnp.concatenate([kq_i[c:] * dec, kdt], axis=0).astype(BF))

    def advance(idxs, states):
        t1 = []
        for idx, s in zip(idxs, states):
            rs, hc = chains[idx]
            t1.append(_dot(jnp.concatenate([sol[idx][:, dk:], qg_sc[rs, hc]], axis=0).astype(BF), s.astype(BF)))
        t2 = [_dot(lhs2[idx], (sol[idx][:, :dk] - t[:c]).astype(BF)) for idx, t in zip(idxs, t1)]
        new_states = []
        for idx, s, ta, tb in zip(idxs, states, t1, t2):
            rs, hc = chains[idx]
            o_sc[rs, hc] = ta[c:] + tb[:c]
            egl = jnp.broadcast_to(egl_sc[rs.start:rs.start + 1, hc], (dk, dk))
            new_states.append(egl * s + tb[c:])
        return new_states

    if per_chunk_state:
        idxs = list(range(len(chains)))
        new_states = advance(idxs, [s0_ref[idx // nh, idx % nh] for idx in idxs])
        for idx, s_new in zip(idxs, new_states):
            s_ref[idx // nh, idx % nh] = s_new
    else:
        state = [s_ref[0, hh] for hh in range(nh)]
        for ci in range(n_chunks):
            state = advance([ci * nh + hh for hh in range(nh)], state)
        for hh in range(nh):
            s_ref[0, hh] = state[hh]

    outs = []
    for hh in range(nh):
        hc = slice(hh * dk, (hh + 1) * dk)
        outs.append(_rms(o_sc[:, hc], gn_ref[...]) * _silu(z_ref[:, hc]))
    yv = jnp.concatenate(outs, axis=-1)
    if fuse_out:
        o_ref[...] = x_ref[...] + _dot(yv.astype(BF), wout_ref[...])
    else:
        o_ref[...] = yv


def _gdn_consts(rows, c, hd):
    r = np.arange(rows)
    tri = ((r[:, None] >= r[None, :]) & (r[:, None] // c == r[None, :] // c)).astype(np.float32)
    lane = np.arange(LANES)[:, None]
    col = np.arange(2 * hd)[None, :]
    head = (col % hd) // GDN_DK
    e = np.where(col < hd, lane == head, lane == GDN_HEADS + head).astype(np.float32)
    return jnp.asarray(tri, BF), jnp.asarray(e, BF)


def gdn_scan(x, q, k, v, z, bg, s0, g_norm, w_out, *, c, rows, n_seq, per_chunk_state, name):
    t, hd = q.shape
    nt = t // n_seq // rows
    fuse_out = w_out is not None
    tri, e = _gdn_consts(rows, c, hd)
    ns = rows // c if per_chunk_state else 1
    row = lambda b, i: (b * nt + i, 0)
    tile = lambda n: pl.BlockSpec((rows, n), row)
    st_spec = pl.BlockSpec((ns,) + s0.shape[1:], lambda b, i: (b, 0, 0, 0))
    ins, specs = [], []
    if fuse_out:
        ins.append(x)
        specs.append(tile(x.shape[1]))
    ins += [q, k, v, z, bg, tri, e, s0, g_norm.reshape(1, GDN_DK)]
    specs += [tile(hd)] * 4 + [tile(LANES), _const_spec(tri.shape), _const_spec(e.shape), st_spec,
                               _const_spec((1, GDN_DK))]
    d_out = hd
    if fuse_out:
        ins.append(_warr(w_out))
        specs.append(_wspec(w_out))
        d_out = w_out.shape[1]
    return pl.pallas_call(
        functools.partial(_gdn_scan_body, c=c, per_chunk_state=per_chunk_state, fuse_out=fuse_out),
        grid=(n_seq, nt),
        in_specs=specs,
        out_specs=[tile(d_out), st_spec],
        out_shape=[SDS((t, d_out), F32), SDS(s0.shape, F32)],
        scratch_shapes=[pltpu.VMEM((rows, hd), F32)] * 8,
        compiler_params=_params(2),
        name=name,
    )(*ins)


def _topk_mask(sb, valid, axis):
    n = sb.shape[axis]
    idx = lax.broadcasted_iota(jnp.int32, sb.shape, axis)
    cnt = jnp.zeros(sb.shape, F32)
    for m in range(n):
        sm = lax.slice_in_dim(sb, m, m + 1, axis=axis)
        ahead = (sm > sb) | ((sm == sb) & (m < idx))
        cnt = cnt + jnp.where(ahead, 1.0, 0.0)
    return jnp.where((cnt < MOBA_TOPK) & valid, 1.0, 0.0)


def _moba_p_body(q_ref, k_ref, v_ref, oh_ref, o_ref, *, blocks):
    length, dh = k_ref.shape
    blk = MOBA_BLOCK
    nb = length // blk
    scale = dh ** -0.5
    k = k_ref[...]
    kaug = jnp.concatenate([k.astype(BF), oh_ref[...]], axis=1)
    vb = v_ref[...].astype(BF)
    km = jnp.sum(k.reshape(nb, blk, dh), axis=1) * (1.0 / blk)
    brow = lax.broadcasted_iota(jnp.int32, (nb, blk), 0)
    rr = lax.broadcasted_iota(jnp.int32, (blk, blk), 0)
    cc = lax.broadcasted_iota(jnp.int32, (blk, blk), 1)
    causal = cc <= rr
    zero_bias = jnp.zeros((blk, dh), BF)
    pad_rows = jnp.zeros((LANES - nb, blk), F32)
    qaug = {}
    for i in blocks:
        q = q_ref[i * blk:(i + 1) * blk, :]
        qs = (q * scale).astype(BF)
        if i == 0:
            qaug[i] = jnp.concatenate([qs, zero_bias], axis=1)
            continue
        sbt = jnp.where(brow < i, _dot_hi(km, q, dot=_dot_nt), NEG_INF)
        selt = _topk_mask(sbt, sbt > NEG_INF / 2, axis=0)
        bias_t = jnp.where((selt > 0.5) | (brow == i), 0.0, NEG_INF)
        bias = jnp.concatenate([bias_t, pad_rows], axis=0).T
        qaug[i] = jnp.concatenate([qs, bias.astype(BF)], axis=1)
    for i in blocks:
        n_keys = (i + 1) * blk
        s = _dot_nt(qaug[i], kaug[:n_keys])
        s_own = jnp.where(causal, s[:, i * blk:], NEG_INF)
        s = jnp.concatenate([s[:, :i * blk], s_own], axis=-1) if i > 0 else s_own
        pe = jnp.exp(s - jnp.max(s, axis=-1, keepdims=True))
        l = jnp.sum(pe, axis=-1, keepdims=True)
        o_ref[i * blk:(i + 1) * blk, :] = _dot(pe.astype(BF), vb[:n_keys]) / l


def _balanced_block_groups(nb, parts):
    total = nb * (nb + 1) // 2
    groups, start, done = [], 0, 0
    for part in range(parts):
        end = start
        target = total * (part + 1) / parts
        while end < nb and (done + end + 1 <= target or end == start):
            done += end + 1
            end += 1
        if part == parts - 1:
            end = nb
        groups.append(tuple(range(start, end)))
        start = end
    return groups


def _lane_group_reduce(row, op, ncol):
    a = jnp.broadcast_to(row, (SUBLANES, LANES))
    shift = ncol
    while shift < LANES:
        a = op(a, pltpu.roll(a, shift, 1))
        shift *= 2
    return a[:1]


def _moba_s_body(pt_ref, qt_ref, hb_ref, kn_ref, vn_ref, *refs, n_pages, ncol):
    del pt_ref
    kp = refs[:n_pages]
    vp = refs[n_pages:2 * n_pages]
    o_ref, s_sc = refs[2 * n_pages:]
    page, nh, dh = kp[0].shape[1:]
    bp = MOBA_BLOCK // page
    nb = n_pages // bp
    prow = page * nh
    pack = LANES // ncol
    q_rep = jnp.concatenate([qt_ref[0]] * pack, axis=0)
    lane = lax.broadcasted_iota(jnp.int32, (1, LANES), 1)
    group = lane // ncol
    qs = q_rep * dh ** -0.5
    rg = lax.broadcasted_iota(jnp.int32, (LANES, pack * dh), 0) // ncol
    cg = lax.broadcasted_iota(jnp.int32, (LANES, pack * dh), 1) // dh
    rhs = jnp.where(rg == cg, jnp.concatenate([qs] * pack, axis=1), 0.0).astype(BF)

    km = []
    for n in range(nb):
        tot = kp[n * bp][0].sum(axis=0)
        for j in range(1, bp):
            tot = tot + kp[n * bp + j][0].sum(axis=0)
        km.append(tot * (1.0 / MOBA_BLOCK))
    km2 = jnp.concatenate(km, axis=0)
    sbm = _dot_hi(km2, q_rep, dot=_dot_nt)
    rh = lax.broadcasted_iota(jnp.int32, sbm.shape, 0) % nh
    ch = lax.broadcasted_iota(jnp.int32, sbm.shape, 1) % ncol % nh
    sb = jnp.sum(jnp.where(rh == ch, sbm, 0.0).reshape(nb, nh, LANES), axis=1)
    sel = _topk_mask(sb, jnp.full(sb.shape, True), axis=0)
    sel_bias = jnp.where(sel > 0.5, 0.0, NEG_INF)

    n_own = kn_ref.shape[1]
    ro = lax.broadcasted_iota(jnp.int32, (n_own, LANES), 0)
    co = lax.broadcasted_iota(jnp.int32, (n_own, LANES), 1)
    own_ok = (ro % nh == co % nh) & (ro // nh <= co // nh) & (co < ncol)
    s_own = jnp.where(own_ok, _dot_nt(kn_ref[0].astype(BF), qs.astype(BF)), NEG_INF)
    mx = jnp.max(s_own, axis=0, keepdims=True)

    n_groups = n_pages // pack
    for j in range(n_groups):
        pages = range(j * pack, (j + 1) * pack)
        k4 = jnp.concatenate([kp[p][0].reshape(prow, dh).astype(BF) for p in pages], axis=1)
        bias = sel_bias[(j * pack) // bp:(j * pack) // bp + 1, :]
        for g in range(1, pack):
            blk = (j * pack + g) // bp
            bias = jnp.where(group >= g, sel_bias[blk:blk + 1, :], bias)
        s = _dot_nt(k4, rhs) + (hb_ref[...] + bias)
        s_sc[j * prow:(j + 1) * prow, :] = s
        mx = jnp.maximum(mx, jnp.max(s, axis=0, keepdims=True))
    mx = _lane_group_reduce(mx, jnp.maximum, ncol)

    pe = jnp.exp(s_own - mx)
    l = jnp.sum(pe, axis=0, keepdims=True)
    acc = _dot_tn(pe.astype(BF), vn_ref[0].astype(BF))[:ncol]
    for j in range(n_groups):
        pages = range(j * pack, (j + 1) * pack)
        pe = jnp.exp(s_sc[j * prow:(j + 1) * prow, :] - mx)
        l = l + jnp.sum(pe, axis=0, keepdims=True)
        v4 = jnp.concatenate([vp[p][0].reshape(prow, dh).astype(BF) for p in pages], axis=1)
        r = _dot_tn(pe.astype(BF), v4)
        for g in range(pack):
            acc = acc + r[g * ncol:(g + 1) * ncol, g * dh:(g + 1) * dh]
    l = _lane_group_reduce(l, jnp.add, ncol)
    l_col = jnp.broadcast_to(l, (LANES, LANES)).T[:ncol, :1]
    o_ref[0] = acc / l_col


def _moba_both_body(pt_ref, q_ref, k_ref, v_ref, oh_ref, qt_ref, hb_ref, kn_ref, vn_ref, *refs,
                    n_pages, ncol, groups):
    pages = refs[:2 * n_pages]
    o_ref, os_ref, s_sc = refs[2 * n_pages:]
    _moba_s_body(pt_ref, qt_ref, hb_ref, kn_ref, vn_ref, *pages, os_ref, s_sc, n_pages=n_pages, ncol=ncol)
    part = pl.program_id(2)
    for gi, blocks in enumerate(groups):
        pl.when(part == gi)(functools.partial(_moba_p_body, q_ref, k_ref, v_ref, oh_ref, o_ref, blocks=blocks))


def moba_both(q, k, v, bsz, qt, kn, vn, pool_k, pool_v, page_table, name):
    t, w = q.shape
    dh = MOBA_DH
    nh = w // dh
    length = t // bsz
    nb = length // MOBA_BLOCK
    bs, ncol, _ = qt.shape
    assert bs % (bsz * nh) == 0, (bs, bsz, nh)
    parts = bs // (bsz * nh)
    assert 1 <= parts <= nb, (parts, nb)
    groups = _balanced_block_groups(nb, parts)
    n_pages = page_table.shape[1]
    page, nh_s = pool_k.shape[1:3]
    n_own = kn.shape[1]
    pack = LANES // ncol
    onehot = jnp.asarray(np.arange(length)[:, None] // MOBA_BLOCK == np.arange(dh)[None, :], BF)
    r = np.arange(page * nh_s)[:, None]
    c = np.arange(LANES)[None, :]
    head_bias = jnp.asarray(np.where(r % nh_s == c % ncol % nh_s, 0.0, NEG_INF), F32)

    seq = lambda b, h, p, pt: (b * nh + h) * parts + p
    p_spec = pl.BlockSpec((length, dh), lambda b, h, p, pt: (b, h))

    def page_spec(pg):
        return pl.BlockSpec((1, page, nh_s, dh), lambda b, h, p, pt: (pt[seq(b, h, p, pt), pg], 0, 0, 0))

    s_spec = lambda rows: pl.BlockSpec((1, rows, dh), lambda b, h, p, pt: (seq(b, h, p, pt), 0, 0))
    grid_spec = pltpu.PrefetchScalarGridSpec(
        num_scalar_prefetch=1,
        grid=(bsz, nh, parts),
        in_specs=[p_spec, p_spec, p_spec,
                  pl.BlockSpec(onehot.shape, lambda b, h, p, pt: (0, 0)),
                  s_spec(ncol),
                  pl.BlockSpec(head_bias.shape, lambda b, h, p, pt: (0, 0)),
                  s_spec(n_own), s_spec(n_own)]
                 + [page_spec(pg) for pg in range(n_pages)] * 2,
        out_specs=[p_spec, s_spec(ncol)],
        scratch_shapes=[pltpu.VMEM((n_pages // pack * page * nh_s, LANES), F32)],
    )
    return pl.pallas_call(
        functools.partial(_moba_both_body, n_pages=n_pages, ncol=ncol, groups=groups),
        grid_spec=grid_spec,
        out_shape=[SDS((t, w), F32), SDS((bs, ncol, dh), F32)],
        compiler_params=_params(3),
        name=name,
    )(page_table, q, k, v, onehot, qt, head_bias, kn, vn, *([pool_k] * n_pages), *([pool_v] * n_pages))


def _to_pm(a):
    b, l, c = a.shape
    return jnp.transpose(a, (1, 0, 2)).reshape(l * b, c)


def _from_pm(a, bsz):
    n, c = a.shape
    return jnp.transpose(a.reshape(n // bsz, bsz, c), (1, 0, 2))


def _pad_lanes(vec, offset):
    out = jnp.zeros((1, LANES), F32)
    return lax.dynamic_update_slice(out, vec.reshape(1, -1).astype(F32), (0, offset))


def kernel(x_prompt, x_sample, state_a_conv, state_b_conv, state_b_rec, cache_c_k, cache_c_v, cache_mem_k, cache_mem_v, page_table, mem_prompt, norm_mix, norm_mem, norm_xattn, norm_ffn, norm_final, a_w_in, a_w_conv, a_w_out, b_w_in, b_w_conv, b_a_log, b_dt_bias, b_norm, b_w_out, c_w_qkv, c_w_out, x_w_q, x_w_kv, x_w_o, f_w_up, f_w_down):
    bp, seq, d = x_prompt.shape
    bs, dseq, _ = x_sample.shape
    depth = norm_mix.shape[0]
    n_mixers = 3
    tm = 512
    ts = bs * dseq

    bf = lambda w: w.astype(BF)
    a_w_in, a_w_out, b_w_out, c_w_qkv, c_w_out = map(bf, (a_w_in, a_w_out, b_w_out, c_w_qkv, c_w_out))
    x_w_q, x_w_kv, x_w_o, f_w_up, f_w_down = map(bf, (x_w_q, x_w_kv, x_w_o, f_w_up, f_w_down))
    cc = b_w_conv.shape[-1]
    hd = GDN_HEADS * GDN_DK
    b_w_qkvz = bf(b_w_in)
    b_w_ba = bf(jnp.pad(b_w_in[:, :, cc + hd:], ((0, 0), (0, 0), (0, LANES - 2 * GDN_HEADS))))

    xp = x_prompt.reshape(bp * seq, d)
    xs = _to_pm(x_sample)

    mem_len = mem_prompt.shape[1]
    mk32, mv32, mkbf, mvbf = memory_kv_all(mem_prompt.reshape(bp * mem_len, d), norm_mem, x_w_kv, XA_HEADS, tm)

    a_p, a_s, bc_p, bc_s, br_p, br_s = [], [], [], [], [], []
    ck_p, cv_p, ck_s, cv_s = [], [], [], []
    for i in range(depth):
        kind, j = i % n_mixers, i // n_mixers
        tag = f"l{i}"
        if kind == 0:
            xp, st = mixer_a_prompt(xp, norm_mix[i], _Layer(a_w_in, j), a_w_conv[j], _Layer(a_w_out, j), bp, tm, tag + "_mixa_p")
            a_p.append(st)
            st_pm = _to_pm(state_a_conv[j])
            xs, nst = mixer_a_sample(xs, st_pm, norm_mix[i], _Layer(a_w_in, j), a_w_conv[j], _Layer(a_w_out, j), bs, tag + "_mixa_s")
            a_s.append(_from_pm(nst, bs))
        elif kind == 1:
            alog = _pad_lanes(b_a_log[j], GDN_HEADS)
            dtb = _pad_lanes(b_dt_bias[j], GDN_HEADS)
            q, k, v, z, bg, st = gdn_proj_prompt(xp, norm_mix[i], _Layer(b_w_qkvz, j), _Layer(b_w_ba, j), b_w_conv[j], alog, dtb,
                                                 bp, 256, tag + "_gdnproj_p")
            bc_p.append(st)
            s0 = jnp.zeros((bp, GDN_HEADS, GDN_DK, GDN_DK), F32)
            xp, s_fin = gdn_scan(xp, q, k, v, z, bg, s0, b_norm[j], _Layer(b_w_out, j), c=math.gcd(seq, GDN_CHUNK),
                                 rows=256, n_seq=bp, per_chunk_state=False, name=tag + "_gdnscan_p")
            br_p.append(s_fin)
            st_pm = _to_pm(state_b_conv[j])
            q, k, v, z, bg, nst = gdn_proj_sample(xs, st_pm, norm_mix[i], _Layer(b_w_qkvz, j), _Layer(b_w_ba, j), b_w_conv[j],
                                                  alog, dtb, bs, tag + "_gdnproj_s")
            bc_s.append(_from_pm(nst, bs))
            cpad = SUBLANES
            padded = lambda a: jnp.pad(_from_pm(a, bs), ((0, 0), (0, cpad - dseq), (0, 0))).reshape(bs * cpad, -1)
            nbatch = 8
            y, s_fin = gdn_scan(None, padded(q), padded(k), padded(v), padded(z), padded(bg), state_b_rec[j],
                                b_norm[j], None, c=cpad, rows=nbatch * cpad, n_seq=bs // nbatch,
                                per_chunk_state=True, name=tag + "_gdnscan_s")
            br_s.append(s_fin)
            y = _to_pm(y.reshape(bs, cpad, hd)[:, :dseq])
            xs = mm_res(y, _Layer(b_w_out, j), xs, ts, tag + "_gdnout_s")
        else:
            q, k, v = norm_mm(xp, norm_mix[i], _Layer(c_w_qkv, j), 3, tm, tag + "_mobaqkv_p")
            ck_p.append(k.reshape(bp, seq, MOBA_HEADS, MOBA_DH))
            cv_p.append(v.reshape(bp, seq, MOBA_HEADS, MOBA_DH))
            qs, ks, vs = norm_mm(xs, norm_mix[i], _Layer(c_w_qkv, j), 3, ts, tag + "_mobaqkv_s")
            kn = _from_pm(ks, bs)
            vn = _from_pm(vs, bs)
            ck_s.append(kn.reshape(bs, dseq, MOBA_HEADS, MOBA_DH))
            cv_s.append(vn.reshape(bs, dseq, MOBA_HEADS, MOBA_DH))
            qt = _from_pm(qs, bs).reshape(bs, dseq * MOBA_HEADS, MOBA_DH)
            rows_kv = lambda a: a.reshape(bs, dseq * MOBA_HEADS, MOBA_DH)
            o, os_ = moba_both(q, k, v, bp, qt, rows_kv(kn), rows_kv(vn), cache_c_k[j], cache_c_v[j], page_table,
                               tag + "_moba")
            xp = mm_res(o, _Layer(c_w_out, j), xp, tm, tag + "_mobaout_p")
            xs = mm_res(_to_pm(os_.reshape(bs, dseq, MOBA_HEADS * MOBA_DH)), _Layer(c_w_out, j), xs, ts, tag + "_mobaout_s")

        (qs,) = norm_mm(xs, norm_xattn[i], _Layer(x_w_q, i), 1, ts, tag + "_xattnq_s")
        dh = d // XA_HEADS
        qt = _from_pm(qs, bs).reshape(bs, dseq * XA_HEADS, dh)
        xp, o = xattn_both(xp, norm_xattn[i], _Layer(x_w_q, i), mkbf, mvbf, i, _Layer(x_w_o, i),
                           qt, cache_mem_k, cache_mem_v, bp, tm, tag + "_xattn")
        xs = mm_res(_to_pm(o.reshape(bs, dseq, d)), _Layer(x_w_o, i), xs, ts, tag + "_xattnout_s")

        last = i == depth - 1
        xp, xs = ffn(xp, xs, norm_ffn[i], _Layer(f_w_up, i), _Layer(f_w_down, i), norm_final, last, tm, tag + "_ffn")

    mem_shape = (depth, bp, mem_len, XA_HEADS, d // XA_HEADS)
    return (xp.reshape(bp, seq, d), _from_pm(xs, bs),
            jnp.stack(a_p), jnp.stack(a_s),
            jnp.stack(bc_p), jnp.stack(bc_s),
            jnp.stack(br_p), jnp.stack(br_s),
            jnp.stack(ck_p), jnp.stack(cv_p), jnp.stack(ck_s), jnp.stack(cv_s),
            mk32.reshape(mem_shape), mv32.reshape(mem_shape))
```

```python
import functools
import math

import numpy as np
import jax
import jax.numpy as jnp
from jax import lax
from jax.experimental import pallas as pl
from jax.experimental.pallas import tpu as pltpu

F32 = jnp.float32
BF = jnp.bfloat16
SDS = jax.ShapeDtypeStruct

EPS = 1e-6
NEG_INF = -1e30
LANES = 128
SUBLANES = 8
VMEM_LIMIT = 56 * 1024 * 1024

ROW_TILE = 512
GDN_ROW_TILE = 256
GDN_SAMPLE_SEQS_PER_STEP = 8

GDN_HEADS = 8
GDN_DK = 128
GDN_CHUNK = 64
MOBA_HEADS = 8
MOBA_DH = 128
MOBA_BLOCK = 256
MOBA_TOPK = 3
PAGE_SIZE = 128
XA_HEADS = 4


def _params(n_axes):
    return pltpu.CompilerParams(dimension_semantics=("arbitrary",) * n_axes,
                                vmem_limit_bytes=VMEM_LIMIT)


def _const_spec(shape):
    nd = len(shape)
    return pl.BlockSpec(shape, lambda *_: (0,) * nd, pipeline_mode=pl.Buffered(1))


class _Layer:
    def __init__(self, arr, j):
        self.arr, self.j = arr, j

    @property
    def shape(self):
        return self.arr.shape[1:]


def _wspec(w):
    if not isinstance(w, _Layer):
        return _const_spec(w.shape)
    nd = len(w.shape)
    j = w.j
    return pl.BlockSpec((None,) + w.shape, lambda *_: (j,) + (0,) * nd, pipeline_mode=pl.Buffered(1))


def _warr(w):
    return w.arr if isinstance(w, _Layer) else w


def _whole_spec(shape):
    nd = len(shape)
    return pl.BlockSpec(shape, lambda *_: (0,) * nd)


def _dot(a, b):
    return jnp.dot(a, b, preferred_element_type=F32)


def _dot_nt(a, b):
    return lax.dot_general(a, b, (((1,), (1,)), ((), ())), preferred_element_type=F32)


def _dot_tn(a, b):
    return lax.dot_general(a, b, (((0,), (0,)), ((), ())), preferred_element_type=F32)


def _split3(a):
    a0 = a.astype(BF)
    r = a - a0.astype(F32)
    a1 = r.astype(BF)
    a2 = (r - a1.astype(F32)).astype(BF)
    return a0, a1, a2


def _dot_exact_rhs(a, b_bf, dot=_dot):
    a0, a1, a2 = _split3(a)
    return dot(a0, b_bf) + dot(a1, b_bf) + dot(a2, b_bf)


def _dot_hi(a, b, dot=_dot):
    a0, a1, _ = _split3(a)
    b0, b1, _ = _split3(b)
    return dot(a0, b0) + (dot(a0, b1) + dot(a1, b0))


def _rms(x, g):
    ms = jnp.mean(x * x, axis=-1, keepdims=True)
    return x * lax.rsqrt(ms + EPS) * g


def _silu(x):
    return x * jax.nn.sigmoid(x)


def _softplus(x):
    return jnp.maximum(x, 0.0) + jnp.log1p(jnp.exp(-jnp.abs(x)))


def _norm_mm_body(x_ref, g_ref, w_ref, *o_refs):
    h = _rms(x_ref[...], g_ref[...]).astype(BF)
    n = o_refs[0].shape[-1]
    for j, o_ref in enumerate(o_refs):
        o_ref[...] = _dot(h, w_ref[:, j * n:(j + 1) * n])


def norm_mm(x, g, w, n_out, tm, name):
    t, k = x.shape
    n = w.shape[1] // n_out
    return pl.pallas_call(
        _norm_mm_body,
        grid=(t // tm,),
        in_specs=[pl.BlockSpec((tm, k), lambda i: (i, 0)), _const_spec((1, k)), _wspec(w)],
        out_specs=[pl.BlockSpec((tm, n), lambda i: (i, 0))] * n_out,
        out_shape=[SDS((t, n), F32)] * n_out,
        compiler_params=_params(1),
        name=name,
    )(x, g.reshape(1, k), _warr(w))


def _mm_res_body(a_ref, w_ref, r_ref, o_ref):
    o_ref[...] = r_ref[...] + _dot(a_ref[...].astype(BF), w_ref[...])


def mm_res(a, w, res, tm, name):
    t, k = a.shape
    n = w.shape[1]
    return pl.pallas_call(
        _mm_res_body,
        grid=(t // tm,),
        in_specs=[pl.BlockSpec((tm, k), lambda i: (i, 0)), _wspec(w),
                  pl.BlockSpec((tm, n), lambda i: (i, 0))],
        out_specs=pl.BlockSpec((tm, n), lambda i: (i, 0)),
        out_shape=SDS((t, n), F32),
        compiler_params=_params(1),
        name=name,
    )(a, _warr(w), res)


def _memkv_body(x_ref, g_ref, w_ref, k32_ref, v32_ref, kbf_ref, vbf_ref):
    h = _rms(x_ref[...], g_ref[0]).astype(BF)
    d = kbf_ref.shape[-1]
    nh, dh = k32_ref.shape[2:]
    for j, (split_ref, bf_ref) in enumerate(((k32_ref, kbf_ref), (v32_ref, vbf_ref))):
        r = _dot(h, w_ref[0, :, j * d:(j + 1) * d])
        bf_ref[0] = r.astype(BF)
        for hh in range(nh):
            split_ref[0, :, hh, :] = r[:, hh * dh:(hh + 1) * dh]


def memory_kv_all(mem2d, norm_mem, w_kv, nh, tm):
    depth, d, _ = w_kv.shape
    t = mem2d.shape[0]
    flat = pl.BlockSpec((1, tm, d), lambda l, i: (l, i, 0))
    split = pl.BlockSpec((1, tm, nh, d // nh), lambda l, i: (l, i, 0, 0))
    return pl.pallas_call(
        _memkv_body,
        grid=(depth, t // tm),
        in_specs=[pl.BlockSpec((tm, d), lambda l, i: (i, 0)),
                  pl.BlockSpec((1, 1, d), lambda l, i: (l, 0, 0)),
                  pl.BlockSpec((1, d, 2 * d), lambda l, i: (l, 0, 0))],
        out_specs=[split, split, flat, flat],
        out_shape=[SDS((depth, t, nh, d // nh), F32)] * 2 + [SDS((depth, t, d), BF)] * 2,
        compiler_params=_params(2),
        name="memory_kv",
    )(mem2d, norm_mem.reshape(depth, 1, d), w_kv)


def _ffn_body(xp_ref, xs_ref, g_ref, wu_ref, wd_ref, gf_ref, op_ref, os_ref, *, ck, final, ntp):
    def tile(x_ref, o_ref):
        x = x_ref[...]
        h = _rms(x, g_ref[...]).astype(BF)
        dff = wd_ref.shape[0]
        acc = x
        for c0 in range(0, dff, ck):
            gate = _dot(h, wu_ref[:, c0:c0 + ck])
            up = _dot(h, wu_ref[:, dff + c0:dff + c0 + ck])
            a = (_silu(gate) * up).astype(BF)
            acc = acc + _dot(a, wd_ref[c0:c0 + ck, :])
        if final:
            acc = _rms(acc, gf_ref[...])
        o_ref[...] = acc

    i = pl.program_id(0)
    pl.when(i < ntp)(lambda: tile(xp_ref, op_ref))
    pl.when(i == ntp)(lambda: tile(xs_ref, os_ref))


def ffn(xp, xs, g, w_up, w_down, g_final, final, tm, name):
    t, d = xp.shape
    ntp = t // tm
    body = functools.partial(_ffn_body, ck=2 * LANES, final=final, ntp=ntp)
    p_spec = pl.BlockSpec((tm, d), lambda i: (jnp.minimum(i, ntp - 1), 0))
    s_spec = pl.BlockSpec(xs.shape, lambda i: (0, 0))
    return pl.pallas_call(
        body,
        grid=(ntp + 1,),
        in_specs=[p_spec, s_spec, _const_spec((1, d)), _wspec(w_up), _wspec(w_down), _const_spec((1, d))],
        out_specs=[p_spec, s_spec],
        out_shape=[SDS((t, d), F32), SDS(xs.shape, F32)],
        compiler_params=_params(1),
        name=name,
    )(xp, xs, g.reshape(1, d), _warr(w_up), _warr(w_down), g_final.reshape(1, d))


def _xattn_p_stages(x_ref, g_ref, wq_ref, k_ref, v_ref, wo_ref, o_ref, *, nh, pre_refs=None):
    x = x_ref[...]
    if pre_refs is not None:
        x = x + _dot(pre_refs[0][...].astype(BF), pre_refs[1][...])
    d = x.shape[-1]
    dh = d // nh
    h = _rms(x, g_ref[...]).astype(BF)
    q = (_dot(h, wq_ref[...]) * dh ** -0.5).astype(BF)
    yield
    outs = []
    for hh in range(nh):
        cs = slice(hh * dh, (hh + 1) * dh)
        s = _dot_nt(q[:, cs], k_ref[0, :, cs])
        p = jnp.exp(s - jnp.max(s, axis=-1, keepdims=True))
        l = jnp.sum(p, axis=-1, keepdims=True)
        outs.append((_dot(p.astype(BF), v_ref[0, :, cs]) / l).astype(BF))
        yield
    o_ref[...] = x + _dot(jnp.concatenate(outs, axis=-1), wo_ref[...])


def _xattn_s_stages(qt_ref, k_ref, v_ref, o_ref, *, nb, nh):
    m_len, _, dh = k_ref.shape[2:]
    rows = m_len * nh
    ncol = qt_ref.shape[1]
    rh = lax.broadcasted_iota(jnp.int32, (rows, ncol), 0) % nh
    ch = lax.broadcasted_iota(jnp.int32, (rows, ncol), 1) % nh
    head_bias = jnp.where(rh == ch, 0.0, NEG_INF)
    s = [_dot_nt(k_ref[0, j].reshape(rows, dh).astype(BF), (qt_ref[j] * dh ** -0.5).astype(BF)) + head_bias
         for j in range(nb)]
    yield
    p = [jnp.exp(a - jnp.max(a, axis=0, keepdims=True)) for a in s]
    yield
    p = [(a / jnp.sum(a, axis=0, keepdims=True)).astype(BF) for a in p]
    yield
    for j in range(nb):
        o_ref[j] = _dot_tn(p[j], v_ref[0, j].reshape(rows, dh).astype(BF))


def _interleave(*stage_generators):
    live = list(stage_generators)
    while live:
        for gen in list(live):
            if next(gen, StopIteration) is StopIteration:
                live.remove(gen)


def _xattn_both_body(*refs, nb, has_pre):
    pre_refs = refs[:2] if has_pre else None
    x_ref, g_ref, wq_ref, k_ref, v_ref, wo_ref, qt_ref, ck_ref, cv_ref, o_ref, os_ref = refs[2 if has_pre else 0:]
    _interleave(_xattn_p_stages(x_ref, g_ref, wq_ref, k_ref, v_ref, wo_ref, o_ref, nh=XA_HEADS, pre_refs=pre_refs),
                _xattn_s_stages(qt_ref, ck_ref, cv_ref, os_ref, nb=nb, nh=ck_ref.shape[3]))


def xattn_both(x, g, w_q, k_bf, v_bf, layer, w_o, qt, cache_k, cache_v, bsz, tm, name, pre=None):
    t, d = x.shape
    nt = t // bsz // tm
    m_len = k_bf.shape[1] // bsz
    bs, ncol, dh = qt.shape
    nh_s = cache_k.shape[3]
    steps = bsz * nt
    assert bs % steps == 0, (bs, steps)
    nb = bs // steps
    kv = lambda a: a.reshape(a.shape[0] * bsz, m_len, d)
    row = lambda b, i: (b * nt + i, 0)
    cache_spec = pl.BlockSpec((1, nb, m_len, nh_s, dh), lambda b, i: (layer, b * nt + i, 0, 0, 0))
    pre_ins, pre_specs = [], []
    if pre is not None:
        pre_ins = [pre[0], _warr(pre[1])]
        pre_specs = [pl.BlockSpec((tm, pre[0].shape[1]), row), _wspec(pre[1])]
    return pl.pallas_call(
        functools.partial(_xattn_both_body, nb=nb, has_pre=pre is not None),
        grid=(bsz, nt),
        in_specs=pre_specs + [
            pl.BlockSpec((tm, d), row), _const_spec((1, d)), _wspec(w_q),
            pl.BlockSpec((1, m_len, d), lambda b, i: (layer * bsz + b, 0, 0)),
            pl.BlockSpec((1, m_len, d), lambda b, i: (layer * bsz + b, 0, 0)),
            _wspec(w_o),
            pl.BlockSpec((nb, ncol, dh), lambda b, i: (b * nt + i, 0, 0)), cache_spec, cache_spec],
        out_specs=[pl.BlockSpec((tm, d), row), pl.BlockSpec((nb, ncol, dh), lambda b, i: (b * nt + i, 0, 0))],
        out_shape=[SDS((t, d), F32), SDS((bs, ncol, dh), F32)],
        compiler_params=_params(2),
        name=name,
    )(*pre_ins, x, g.reshape(1, d), _warr(w_q), kv(k_bf), kv(v_bf), _warr(w_o), qt, cache_k, cache_v)


def _mixa_p_body(x_ref, g_ref, win_ref, wc_ref, wout_ref, o_ref, st_ref, zbuf, *, nt):
    i = pl.program_id(1)
    tm, d = x_ref.shape
    width = wc_ref.shape[0]

    @pl.when(i == 0)
    def _():
        zbuf[0:SUBLANES, :] = jnp.zeros((SUBLANES, d), F32)

    x = x_ref[...]
    h = _rms(x, g_ref[...]).astype(BF)
    gate_b = _dot(h, win_ref[:, 0:d])
    zn = _dot(h, win_ref[:, d:2 * d]) * _dot(h, win_ref[:, 2 * d:3 * d])
    zbuf[SUBLANES:SUBLANES + tm, :] = zn
    conv = wc_ref[width - 1:width, :] * zn
    for j in range(width - 1):
        off = SUBLANES - (width - 1) + j
        conv = conv + wc_ref[j:j + 1, :] * zbuf[off:off + tm, :]
    o_ref[...] = x + _dot((gate_b * conv).astype(BF), wout_ref[...])
    zbuf[0:SUBLANES, :] = zbuf[tm:tm + SUBLANES, :]

    @pl.when(i == nt - 1)
    def _():
        st_ref[0] = zbuf[SUBLANES - (width - 1):SUBLANES, :]


def mixer_a_prompt(x, g, w_in, w_conv, w_out, bsz, tm, name):
    t, d = x.shape
    nt = t // bsz // tm
    width = w_conv.shape[0]
    row = lambda b, i: (b * nt + i, 0)
    return pl.pallas_call(
        functools.partial(_mixa_p_body, nt=nt),
        grid=(bsz, nt),
        in_specs=[pl.BlockSpec((tm, d), row), _const_spec((1, d)), _wspec(w_in),
                  _const_spec(w_conv.shape), _wspec(w_out)],
        out_specs=[pl.BlockSpec((tm, d), row), pl.BlockSpec((1, width - 1, d), lambda b, i: (b, 0, 0))],
        out_shape=[SDS((t, d), F32), SDS((bsz, width - 1, d), F32)],
        scratch_shapes=[pltpu.VMEM((tm + SUBLANES, d), F32)],
        compiler_params=_params(2),
        name=name,
    )(x, g.reshape(1, d), _warr(w_in), w_conv, _warr(w_out))


def _posmajor_conv(state, z, wc_ref, bsz):
    width = wc_ref.shape[0]
    n = z.shape[0]
    zpad = jnp.concatenate([state, z], axis=0)
    conv = wc_ref[0:1, :] * zpad[0:n]
    for j in range(1, width):
        conv = conv + wc_ref[j:j + 1, :] * zpad[j * bsz:j * bsz + n]
    return conv, zpad[n:]


def _mixa_s_body(x_ref, st_ref, g_ref, win_ref, wc_ref, wout_ref, o_ref, nst_ref, *, bsz):
    x = x_ref[...]
    d = x.shape[-1]
    h = _rms(x, g_ref[...]).astype(BF)
    gate_b = _dot(h, win_ref[:, 0:d])
    zn = _dot(h, win_ref[:, d:2 * d]) * _dot(h, win_ref[:, 2 * d:3 * d])
    conv, new_state = _posmajor_conv(st_ref[...], zn, wc_ref, bsz)
    o_ref[...] = x + _dot((gate_b * conv).astype(BF), wout_ref[...])
    nst_ref[...] = new_state


def mixer_a_sample(x, state_pm, g, w_in, w_conv, w_out, bsz, name):
    t, d = x.shape
    return pl.pallas_call(
        functools.partial(_mixa_s_body, bsz=bsz),
        grid=(1,),
        in_specs=[_const_spec(x.shape), _const_spec(state_pm.shape), _const_spec((1, d)),
                  _wspec(w_in), _const_spec(w_conv.shape), _wspec(w_out)],
        out_specs=[_whole_spec(x.shape), _whole_spec(state_pm.shape)],
        out_shape=[SDS(x.shape, F32), SDS(state_pm.shape, F32)],
        compiler_params=_params(1),
        name=name,
    )(x, state_pm, g.reshape(1, d), _warr(w_in), w_conv, _warr(w_out))


GDN_COL_CHUNK = 2 * LANES


def _gdn_qkv_chunk(conv, c0, q_ref, k_ref, v_ref):
    nh, dk = GDN_HEADS, GDN_DK
    hd = nh * dk
    act = _silu(conv)
    section, col = c0 // hd, c0 % hd
    if section == 2:
        v_ref[:, col:col + GDN_COL_CHUNK] = act
        return
    ref = q_ref if section == 0 else k_ref
    for j in range(GDN_COL_CHUNK // dk):
        t = act[:, j * dk:(j + 1) * dk]
        t = t * lax.rsqrt(jnp.sum(t * t, axis=-1, keepdims=True) + EPS)
        ref[:, col + j * dk:col + (j + 1) * dk] = t * dk ** -0.5 if section == 0 else t


def _gdn_gates(ba, alog_ref, dtb_ref, bg_ref):
    nh = GDN_HEADS
    lane = lax.broadcasted_iota(jnp.int32, ba.shape, 1)
    beta = jax.nn.sigmoid(ba)
    g = -jnp.exp(alog_ref[...]) * _softplus(ba + dtb_ref[...])
    bg_ref[...] = jnp.where(lane < nh, beta, jnp.where(lane < 2 * nh, g, 0.0))


def _gdn_proj_p_body(x_ref, g_ref, w_ref, wba_ref, wc_ref, alog_ref, dtb_ref,
                     q_ref, k_ref, v_ref, z_ref, bg_ref, st_ref, cbuf, *, nt):
    i = pl.program_id(1)
    tm = x_ref.shape[0]
    width, cc = wc_ref.shape
    ck = GDN_COL_CHUNK

    @pl.when(i == 0)
    def _():
        cbuf[0:SUBLANES, :] = jnp.zeros((SUBLANES, cc), F32)

    h = _rms(x_ref[...], g_ref[...]).astype(BF)
    for c0 in range(0, cc, ck):
        cols = slice(c0, c0 + ck)
        pre = _dot(h, w_ref[:, cols])
        cbuf[SUBLANES:SUBLANES + tm, cols] = pre
        conv = wc_ref[width - 1:width, cols] * pre
        for j in range(width - 1):
            off = SUBLANES - (width - 1) + j
            conv = conv + wc_ref[j:j + 1, cols] * cbuf[off:off + tm, cols]
        _gdn_qkv_chunk(conv, c0, q_ref, k_ref, v_ref)
    for c0 in range(0, z_ref.shape[1], ck):
        z_ref[:, c0:c0 + ck] = _dot(h, w_ref[:, cc + c0:cc + c0 + ck])
    _gdn_gates(_dot(h, wba_ref[...]), alog_ref, dtb_ref, bg_ref)
    cbuf[0:SUBLANES, :] = cbuf[tm:tm + SUBLANES, :]

    @pl.when(i == nt - 1)
    def _():
        st_ref[0] = cbuf[SUBLANES - (width - 1):SUBLANES, :]


def gdn_proj_prompt(x, g, w_qkvz, w_ba, w_conv, alog, dtb, bsz, tm, name):
    t, d = x.shape
    nt = t // bsz // tm
    width, cc = w_conv.shape
    hd = GDN_HEADS * GDN_DK
    row = lambda b, i: (b * nt + i, 0)
    tile = lambda n: pl.BlockSpec((tm, n), row)
    return pl.pallas_call(
        functools.partial(_gdn_proj_p_body, nt=nt),
        grid=(bsz, nt),
        in_specs=[tile(d), _const_spec((1, d)), _wspec(w_qkvz), _wspec(w_ba),
                  _const_spec(w_conv.shape), _const_spec(alog.shape), _const_spec(dtb.shape)],
        out_specs=[tile(hd), tile(hd), tile(hd), tile(hd), tile(LANES),
                   pl.BlockSpec((1, width - 1, cc), lambda b, i: (b, 0, 0))],
        out_shape=[SDS((t, hd), F32)] * 4 + [SDS((t, LANES), F32), SDS((bsz, width - 1, cc), F32)],
        scratch_shapes=[pltpu.VMEM((tm + SUBLANES, cc), F32)],
        compiler_params=_params(2),
        name=name,
    )(x, g.reshape(1, d), _warr(w_qkvz), _warr(w_ba), w_conv, alog, dtb)


def _gdn_proj_s_body(x_ref, st_ref, g_ref, w_ref, wba_ref, wc_ref, alog_ref, dtb_ref,
                     q_ref, k_ref, v_ref, z_ref, bg_ref, nst_ref, *, bsz):
    cc = wc_ref.shape[1]
    h = _rms(x_ref[...], g_ref[...]).astype(BF)
    qkv = _dot(h, w_ref[:, :cc])
    z_ref[...] = _dot(h, w_ref[:, cc:cc + z_ref.shape[1]])
    ba = _dot(h, wba_ref[...])
    conv, new_state = _posmajor_conv(st_ref[...], qkv, wc_ref, bsz)
    for c0 in range(0, cc, GDN_COL_CHUNK):
        _gdn_qkv_chunk(conv[:, c0:c0 + GDN_COL_CHUNK], c0, q_ref, k_ref, v_ref)
    _gdn_gates(ba, alog_ref, dtb_ref, bg_ref)
    nst_ref[...] = new_state


def gdn_proj_sample(x, state_pm, g, w_qkvz, w_ba, w_conv, alog, dtb, bsz, name):
    t, d = x.shape
    cc = w_conv.shape[1]
    hd = GDN_HEADS * GDN_DK
    ins = (x, state_pm, g.reshape(1, d), w_qkvz, w_ba, w_conv, alog, dtb)
    outs = [SDS((t, hd), F32)] * 4 + [SDS((t, LANES), F32), SDS(state_pm.shape, F32)]
    return pl.pallas_call(
        functools.partial(_gdn_proj_s_body, bsz=bsz),
        grid=(1,),
        in_specs=[_wspec(a) for a in ins],
        out_specs=[_whole_spec(o.shape) for o in outs],
        out_shape=outs,
        compiler_params=_params(1),
        name=name,
    )(*[_warr(a) for a in ins])


def _pad_transpose(a):
    c = a.shape[0]
    if c < LANES:
        a = jnp.concatenate([a, jnp.zeros((LANES - c, a.shape[1]), a.dtype)], axis=0)
    return a.T


def _gdn_scan_body(*refs, c, per_chunk_state, fuse_out):
    if fuse_out:
        x_ref, refs = refs[0], refs[1:]
    (q_ref, k_ref, v_ref, z_ref, bg_ref, tri_ref, s0_ref, gn_ref) = refs[:8]
    refs = refs[8:]
    if fuse_out:
        wout_ref, refs = refs[0], refs[1:]
    o_ref, s_ref, qg_sc, kdec_sc, rk_sc, rv_sc, gam_sc, beta_sc, egl_sc, o_sc = refs

    nh, dk = GDN_HEADS, GDN_DK
    rows, hd = q_ref.shape
    n_chunks = rows // c
    n_double = int(math.log2(c)) - 1

    if not per_chunk_state:
        @pl.when(pl.program_id(1) == 0)
        def _():
            s_ref[...] = s0_ref[...]

    bg = bg_ref[...]
    gam = _dot_exact_rhs(bg, tri_ref[...], dot=lambda a, b: _dot(b, a))
    beta_b = jnp.concatenate([jnp.broadcast_to(bg[:, hh:hh + 1], (rows, dk)) for hh in range(nh)], axis=1)
    gam_b = jnp.concatenate([jnp.broadcast_to(gam[:, nh + hh:nh + hh + 1], (rows, dk)) for hh in range(nh)], axis=1)
    gl_b = jnp.broadcast_to(gam_b.reshape(n_chunks, c, hd)[:, c - 1:c, :], (n_chunks, c, hd)).reshape(rows, hd)
    eg = jnp.exp(gam_b)
    k = k_ref[...]
    qg_sc[...] = q_ref[...] * eg
    rk_sc[...] = beta_b * eg * k
    rv_sc[...] = beta_b * v_ref[...]
    kdec_sc[...] = k * jnp.exp(gl_b - gam_b)
    gam_sc[...] = gam_b
    beta_sc[...] = beta_b
    egl_sc[...] = jnp.exp(gl_b)

    ii = lax.broadcasted_iota(jnp.int32, (c, c), 0)
    jj = lax.broadcasted_iota(jnp.int32, (c, c), 1)
    incl = ii >= jj
    strict = ii > jj

    chains = [(slice(ci * c, (ci + 1) * c), slice(hh * dk, (hh + 1) * dk))
              for ci in range(n_chunks) for hh in range(nh)]
    kq, decay, pm = [], [], []
    for rs, hc in chains:
        kc = k_ref[rs, hc]
        kq.append(_dot_nt(jnp.concatenate([kc, q_ref[rs, hc]], axis=0).astype(BF), kc.astype(BF)))
    for (rs, hc), kq_i in zip(chains, kq):
        gcol = gam_sc[rs, hc]
        diff = gcol[:, :c] - _pad_transpose(gcol)[:c, :c]
        dec = jnp.where(incl, jnp.exp(jnp.where(incl, diff, 0.0)), 0.0)
        decay.append(dec)
        pm.append(jnp.where(strict, -(beta_sc[rs, hc][:, :c] * kq_i[:c] * dec), 0.0))
    mk = pm
    for _ in range(n_double):
        mk = [_dot(a.astype(BF), a.astype(BF)) for a in mk]
        pm = [p + a + _dot(p.astype(BF), a.astype(BF)) for p, a in zip(pm, mk)]
    sol, lhs2 = [], []
    for (rs, hc), p, kq_i, dec in zip(chains, pm, kq, decay):
        rhs = jnp.concatenate([rv_sc[rs, hc], rk_sc[rs, hc]], axis=1)
        sol.append(rhs + _dot(p.astype(BF), rhs.astype(BF)))
        kdt = _pad_transpose(kdec_sc[rs, hc])[:, :c]
        lhs2.append(jnp.concatenate([kq_i[c:] * dec, kdt], axis=0).astype(BF))

    def advance(idxs, states):
        t1 = []
        for idx, s in zip(idxs, states):
            rs, hc = chains[idx]
            t1.append(_dot(jnp.concatenate([sol[idx][:, dk:], qg_sc[rs, hc]], axis=0).astype(BF), s.astype(BF)))
        t2 = [_dot(lhs2[idx], (sol[idx][:, :dk] - t[:c]).astype(BF)) for idx, t in zip(idxs, t1)]
        new_states = []
        for idx, s, ta, tb in zip(idxs, states, t1, t2):
            rs, hc = chains[idx]
            o_sc[rs, hc] = ta[c:] + tb[:c]
            egl = jnp.broadcast_to(egl_sc[rs.start:rs.start + 1, hc], (dk, dk))
            new_states.append(egl * s + tb[c:])
        return new_states

    if per_chunk_state:
        idxs = list(range(len(chains)))
        new_states = advance(idxs, [s0_ref[idx // nh, idx % nh] for idx in idxs])
        for idx, s_new in zip(idxs, new_states):
            s_ref[idx // nh, idx % nh] = s_new
    else:
        state = [s_ref[0, hh] for hh in range(nh)]
        for ci in range(n_chunks):
            state = advance([ci * nh + hh for hh in range(nh)], state)
        for hh in range(nh):
            s_ref[0, hh] = state[hh]

    outs = []
    for hh in range(nh):
        hc = slice(hh * dk, (hh + 1) * dk)
        outs.append(_rms(o_sc[:, hc], gn_ref[...]) * _silu(z_ref[:, hc]))
    yv = jnp.concatenate(outs, axis=-1)
    if fuse_out:
        o_ref[...] = x_ref[...] + _dot(yv.astype(BF), wout_ref[...])
    else:
        o_ref[...] = yv


def _chunk_cumsum_matrix(rows, c):
    r = np.arange(rows)
    return jnp.asarray((r[:, None] >= r[None, :]) & (r[:, None] // c == r[None, :] // c), BF)


def gdn_scan(x, q, k, v, z, bg, s0, g_norm, w_out, *, c, rows, n_seq, per_chunk_state, name):
    t, hd = q.shape
    nt = t // n_seq // rows
    fuse_out = w_out is not None
    tri = _chunk_cumsum_matrix(rows, c)
    ns = rows // c if per_chunk_state else 1
    row = lambda b, i: (b * nt + i, 0)
    tile = lambda n: pl.BlockSpec((rows, n), row)
    st_spec = pl.BlockSpec((ns,) + s0.shape[1:], lambda b, i: (b, 0, 0, 0))
    ins, specs = [], []
    if fuse_out:
        ins.append(x)
        specs.append(tile(x.shape[1]))
    ins += [q, k, v, z, bg, tri, s0, g_norm.reshape(1, GDN_DK)]
    specs += [tile(hd)] * 4 + [tile(LANES), _const_spec(tri.shape), st_spec, _const_spec((1, GDN_DK))]
    d_out = hd
    if fuse_out:
        ins.append(_warr(w_out))
        specs.append(_wspec(w_out))
        d_out = w_out.shape[1]
    return pl.pallas_call(
        functools.partial(_gdn_scan_body, c=c, per_chunk_state=per_chunk_state, fuse_out=fuse_out),
        grid=(n_seq, nt),
        in_specs=specs,
        out_specs=[tile(d_out), st_spec],
        out_shape=[SDS((t, d_out), F32), SDS(s0.shape, F32)],
        scratch_shapes=[pltpu.VMEM((rows, hd), F32)] * 8,
        compiler_params=_params(2),
        name=name,
    )(*ins)


def _topk_mask(sb, valid, axis):
    n = sb.shape[axis]
    idx = lax.broadcasted_iota(jnp.int32, sb.shape, axis)
    cnt = jnp.zeros(sb.shape, F32)
    for m in range(n):
        sm = lax.slice_in_dim(sb, m, m + 1, axis=axis)
        ahead = (sm > sb) | ((sm == sb) & (m < idx))
        cnt = cnt + jnp.where(ahead, 1.0, 0.0)
    return jnp.where((cnt < MOBA_TOPK) & valid, 1.0, 0.0)


def _moba_p_body(q_ref, k_ref, v_ref, oh_ref, o_ref, *, blocks):
    length, dh = k_ref.shape
    blk = MOBA_BLOCK
    nb = length // blk
    scale = dh ** -0.5
    k = k_ref[...]
    kaug = jnp.concatenate([k.astype(BF), oh_ref[...]], axis=1)
    vb = v_ref[...].astype(BF)
    km = jnp.sum(k.reshape(nb, blk, dh), axis=1) * (1.0 / blk)
    brow = lax.broadcasted_iota(jnp.int32, (nb, blk), 0)
    rr = lax.broadcasted_iota(jnp.int32, (blk, blk), 0)
    cc = lax.broadcasted_iota(jnp.int32, (blk, blk), 1)
    causal = cc <= rr
    zero_bias = jnp.zeros((blk, dh), BF)
    pad_rows = jnp.zeros((LANES - nb, blk), F32)
    qaug = {}
    for i in blocks:
        q = q_ref[i * blk:(i + 1) * blk, :]
        qs = (q * scale).astype(BF)
        if i == 0:
            qaug[i] = jnp.concatenate([qs, zero_bias], axis=1)
            continue
        sbt = jnp.where(brow < i, _dot_hi(km, q, dot=_dot_nt), NEG_INF)
        selt = _topk_mask(sbt, sbt > NEG_INF / 2, axis=0)
        bias_t = jnp.where((selt > 0.5) | (brow == i), 0.0, NEG_INF)
        bias = jnp.concatenate([bias_t, pad_rows], axis=0).T
        qaug[i] = jnp.concatenate([qs, bias.astype(BF)], axis=1)
    for i in blocks:
        n_keys = (i + 1) * blk
        s = _dot_nt(qaug[i], kaug[:n_keys])
        s_own = jnp.where(causal, s[:, i * blk:], NEG_INF)
        s = jnp.concatenate([s[:, :i * blk], s_own], axis=-1) if i > 0 else s_own
        pe = jnp.exp(s - jnp.max(s, axis=-1, keepdims=True))
        l = jnp.sum(pe, axis=-1, keepdims=True)
        o_ref[i * blk:(i + 1) * blk, :] = _dot(pe.astype(BF), vb[:n_keys]) / l


def _balanced_block_groups(nb, parts):
    total = nb * (nb + 1) // 2
    groups, start, done = [], 0, 0
    for part in range(parts):
        end = start
        target = total * (part + 1) / parts
        while end < nb and (done + end + 1 <= target or end == start):
            done += end + 1
            end += 1
        if part == parts - 1:
            end = nb
        groups.append(tuple(range(start, end)))
        start = end
    return groups


def _lane_group_reduce(row, op, ncol):
    a = jnp.broadcast_to(row, (SUBLANES, LANES))
    shift = ncol
    while shift < LANES:
        a = op(a, pltpu.roll(a, shift, 1))
        shift *= 2
    return a[:1]


def _moba_s_body(pt_ref, qt_ref, hb_ref, kn_ref, vn_ref, *refs, n_pages, ncol):
    del pt_ref
    kp = refs[:n_pages]
    vp = refs[n_pages:2 * n_pages]
    o_ref, s_sc = refs[2 * n_pages:]
    page, nh, dh = kp[0].shape[1:]
    bp = MOBA_BLOCK // page
    nb = n_pages // bp
    prow = page * nh
    pack = LANES // ncol
    q_rep = jnp.concatenate([qt_ref[0]] * pack, axis=0)
    lane = lax.broadcasted_iota(jnp.int32, (1, LANES), 1)
    group = lane // ncol
    qs = q_rep * dh ** -0.5
    rg = lax.broadcasted_iota(jnp.int32, (LANES, pack * dh), 0) // ncol
    cg = lax.broadcasted_iota(jnp.int32, (LANES, pack * dh), 1) // dh
    rhs = jnp.where(rg == cg, jnp.concatenate([qs] * pack, axis=1), 0.0).astype(BF)

    km = []
    for n in range(nb):
        tot = kp[n * bp][0].sum(axis=0)
        for j in range(1, bp):
            tot = tot + kp[n * bp + j][0].sum(axis=0)
        km.append(tot * (1.0 / MOBA_BLOCK))
    km2 = jnp.concatenate(km, axis=0)
    sbm = _dot_hi(km2, q_rep, dot=_dot_nt)
    rh = lax.broadcasted_iota(jnp.int32, sbm.shape, 0) % nh
    ch = lax.broadcasted_iota(jnp.int32, sbm.shape, 1) % ncol % nh
    sb = jnp.sum(jnp.where(rh == ch, sbm, 0.0).reshape(nb, nh, LANES), axis=1)
    sel = _topk_mask(sb, jnp.full(sb.shape, True), axis=0)
    sel_bias = jnp.where(sel > 0.5, 0.0, NEG_INF)

    n_own = kn_ref.shape[1]
    ro = lax.broadcasted_iota(jnp.int32, (n_own, LANES), 0)
    co = lax.broadcasted_iota(jnp.int32, (n_own, LANES), 1)
    own_ok = (ro % nh == co % nh) & (ro // nh <= co // nh) & (co < ncol)
    s_own = jnp.where(own_ok, _dot_nt(kn_ref[0].astype(BF), qs.astype(BF)), NEG_INF)
    mx = jnp.max(s_own, axis=0, keepdims=True)

    n_groups = n_pages // pack
    for j in range(n_groups):
        pages = range(j * pack, (j + 1) * pack)
        k4 = jnp.concatenate([kp[p][0].reshape(prow, dh).astype(BF) for p in pages], axis=1)
        bias = sel_bias[(j * pack) // bp:(j * pack) // bp + 1, :]
        for g in range(1, pack):
            blk = (j * pack + g) // bp
            bias = jnp.where(group >= g, sel_bias[blk:blk + 1, :], bias)
        s = _dot_nt(k4, rhs) + (hb_ref[...] + bias)
        s_sc[j * prow:(j + 1) * prow, :] = s
        mx = jnp.maximum(mx, jnp.max(s, axis=0, keepdims=True))
    mx = _lane_group_reduce(mx, jnp.maximum, ncol)

    pe = jnp.exp(s_own - mx)
    l = jnp.sum(pe, axis=0, keepdims=True)
    acc = _dot_tn(pe.astype(BF), vn_ref[0].astype(BF))[:ncol]
    for j in range(n_groups):
        pages = range(j * pack, (j + 1) * pack)
        pe = jnp.exp(s_sc[j * prow:(j + 1) * prow, :] - mx)
        l = l + jnp.sum(pe, axis=0, keepdims=True)
        v4 = jnp.concatenate([vp[p][0].reshape(prow, dh).astype(BF) for p in pages], axis=1)
        r = _dot_tn(pe.astype(BF), v4)
        for g in range(pack):
            acc = acc + r[g * ncol:(g + 1) * ncol, g * dh:(g + 1) * dh]
    l = _lane_group_reduce(l, jnp.add, ncol)
    l_col = jnp.broadcast_to(l, (LANES, LANES)).T[:ncol, :1]
    o_ref[0] = acc / l_col


def _moba_both_body(pt_ref, q_ref, k_ref, v_ref, oh_ref, qt_ref, hb_ref, kn_ref, vn_ref, *refs,
                    n_pages, ncol, groups):
    pages = refs[:2 * n_pages]
    o_ref, os_ref, s_sc = refs[2 * n_pages:]
    _moba_s_body(pt_ref, qt_ref, hb_ref, kn_ref, vn_ref, *pages, os_ref, s_sc, n_pages=n_pages, ncol=ncol)
    part = pl.program_id(2)
    for gi, blocks in enumerate(groups):
        pl.when(part == gi)(functools.partial(_moba_p_body, q_ref, k_ref, v_ref, oh_ref, o_ref, blocks=blocks))


def moba_both(q, k, v, bsz, qt, kn, vn, pool_k, pool_v, page_table, name):
    t, w = q.shape
    dh = MOBA_DH
    nh = w // dh
    length = t // bsz
    nb = length // MOBA_BLOCK
    bs, ncol, _ = qt.shape
    assert bs % (bsz * nh) == 0, (bs, bsz, nh)
    parts = bs // (bsz * nh)
    assert 1 <= parts <= nb, (parts, nb)
    groups = _balanced_block_groups(nb, parts)
    n_pages = page_table.shape[1]
    page, nh_s = pool_k.shape[1:3]
    n_own = kn.shape[1]
    pack = LANES // ncol
    onehot = jnp.asarray(np.arange(length)[:, None] // MOBA_BLOCK == np.arange(dh)[None, :], BF)
    r = np.arange(page * nh_s)[:, None]
    c = np.arange(LANES)[None, :]
    head_bias = jnp.asarray(np.where(r % nh_s == c % ncol % nh_s, 0.0, NEG_INF), F32)

    seq = lambda b, h, p, pt: (b * nh + h) * parts + p
    p_spec = pl.BlockSpec((length, dh), lambda b, h, p, pt: (b, h))

    def page_spec(pg):
        return pl.BlockSpec((1, page, nh_s, dh), lambda b, h, p, pt: (pt[seq(b, h, p, pt), pg], 0, 0, 0))

    s_spec = lambda rows: pl.BlockSpec((1, rows, dh), lambda b, h, p, pt: (seq(b, h, p, pt), 0, 0))
    grid_spec = pltpu.PrefetchScalarGridSpec(
        num_scalar_prefetch=1,
        grid=(bsz, nh, parts),
        in_specs=[p_spec, p_spec, p_spec,
                  pl.BlockSpec(onehot.shape, lambda b, h, p, pt: (0, 0)),
                  s_spec(ncol),
                  pl.BlockSpec(head_bias.shape, lambda b, h, p, pt: (0, 0)),
                  s_spec(n_own), s_spec(n_own)]
                 + [page_spec(pg) for pg in range(n_pages)] * 2,
        out_specs=[p_spec, s_spec(ncol)],
        scratch_shapes=[pltpu.VMEM((n_pages // pack * page * nh_s, LANES), F32)],
    )
    return pl.pallas_call(
        functools.partial(_moba_both_body, n_pages=n_pages, ncol=ncol, groups=groups),
        grid_spec=grid_spec,
        out_shape=[SDS((t, w), F32), SDS((bs, ncol, dh), F32)],
        compiler_params=_params(3),
        name=name,
    )(page_table, q, k, v, onehot, qt, head_bias, kn, vn, *([pool_k] * n_pages), *([pool_v] * n_pages))


def _to_pm(a):
    b, l, c = a.shape
    return jnp.transpose(a, (1, 0, 2)).reshape(l * b, c)


def _from_pm(a, bsz):
    n, c = a.shape
    return jnp.transpose(a.reshape(n // bsz, bsz, c), (1, 0, 2))


def _pad_lanes(vec, offset):
    out = jnp.zeros((1, LANES), F32)
    return lax.dynamic_update_slice(out, vec.reshape(1, -1).astype(F32), (0, offset))


def kernel(x_prompt, x_sample, state_a_conv, state_b_conv, state_b_rec, cache_c_k, cache_c_v, cache_mem_k, cache_mem_v, page_table, mem_prompt, norm_mix, norm_mem, norm_xattn, norm_ffn, norm_final, a_w_in, a_w_conv, a_w_out, b_w_in, b_w_conv, b_a_log, b_dt_bias, b_norm, b_w_out, c_w_qkv, c_w_out, x_w_q, x_w_kv, x_w_o, f_w_up, f_w_down):
    bp, seq, d = x_prompt.shape
    bs, dseq, _ = x_sample.shape
    depth = norm_mix.shape[0]
    n_mixers = 3
    tm = ROW_TILE
    ts = bs * dseq

    bf = lambda w: w.astype(BF)
    a_w_in, a_w_out, b_w_out, c_w_qkv, c_w_out = map(bf, (a_w_in, a_w_out, b_w_out, c_w_qkv, c_w_out))
    x_w_q, x_w_kv, x_w_o, f_w_up, f_w_down = map(bf, (x_w_q, x_w_kv, x_w_o, f_w_up, f_w_down))
    cc = b_w_conv.shape[-1]
    hd = GDN_HEADS * GDN_DK
    b_w_qkvz = bf(b_w_in)
    b_w_ba = bf(jnp.pad(b_w_in[:, :, cc + hd:], ((0, 0), (0, 0), (0, LANES - 2 * GDN_HEADS))))

    xp = x_prompt.reshape(bp * seq, d)
    xs = _to_pm(x_sample)

    mem_len = mem_prompt.shape[1]
    mk32, mv32, mkbf, mvbf = memory_kv_all(mem_prompt.reshape(bp * mem_len, d), norm_mem, x_w_kv, XA_HEADS, tm)

    a_p, a_s, bc_p, bc_s, br_p, br_s = [], [], [], [], [], []
    ck_p, cv_p, ck_s, cv_s = [], [], [], []
    for i in range(depth):
        kind, j = i % n_mixers, i // n_mixers
        tag = f"l{i}"
        pre = None
        if kind == 0:
            xp, st = mixer_a_prompt(xp, norm_mix[i], _Layer(a_w_in, j), a_w_conv[j], _Layer(a_w_out, j), bp, tm, tag + "_mixa_p")
            a_p.append(st)
            st_pm = _to_pm(state_a_conv[j])
            xs, nst = mixer_a_sample(xs, st_pm, norm_mix[i], _Layer(a_w_in, j), a_w_conv[j], _Layer(a_w_out, j), bs, tag + "_mixa_s")
            a_s.append(_from_pm(nst, bs))
        elif kind == 1:
            alog = _pad_lanes(b_a_log[j], GDN_HEADS)
            dtb = _pad_lanes(b_dt_bias[j], GDN_HEADS)
            q, k, v, z, bg, st = gdn_proj_prompt(xp, norm_mix[i], _Layer(b_w_qkvz, j), _Layer(b_w_ba, j), b_w_conv[j], alog, dtb,
                                                 bp, GDN_ROW_TILE, tag + "_gdnproj_p")
            bc_p.append(st)
            s0 = jnp.zeros((bp, GDN_HEADS, GDN_DK, GDN_DK), F32)
            xp, s_fin = gdn_scan(xp, q, k, v, z, bg, s0, b_norm[j], _Layer(b_w_out, j), c=math.gcd(seq, GDN_CHUNK),
                                 rows=GDN_ROW_TILE, n_seq=bp, per_chunk_state=False, name=tag + "_gdnscan_p")
            br_p.append(s_fin)
            st_pm = _to_pm(state_b_conv[j])
            q, k, v, z, bg, nst = gdn_proj_sample(xs, st_pm, norm_mix[i], _Layer(b_w_qkvz, j), _Layer(b_w_ba, j), b_w_conv[j],
                                                  alog, dtb, bs, tag + "_gdnproj_s")
            bc_s.append(_from_pm(nst, bs))
            cpad = SUBLANES
            padded = lambda a: jnp.pad(_from_pm(a, bs), ((0, 0), (0, cpad - dseq), (0, 0))).reshape(bs * cpad, -1)
            nbatch = GDN_SAMPLE_SEQS_PER_STEP
            y, s_fin = gdn_scan(None, padded(q), padded(k), padded(v), padded(z), padded(bg), state_b_rec[j],
                                b_norm[j], None, c=cpad, rows=nbatch * cpad, n_seq=bs // nbatch,
                                per_chunk_state=True, name=tag + "_gdnscan_s")
            br_s.append(s_fin)
            y = _to_pm(y.reshape(bs, cpad, hd)[:, :dseq])
            xs = mm_res(y, _Layer(b_w_out, j), xs, ts, tag + "_gdnout_s")
        else:
            q, k, v = norm_mm(xp, norm_mix[i], _Layer(c_w_qkv, j), 3, tm, tag + "_mobaqkv_p")
            ck_p.append(k.reshape(bp, seq, MOBA_HEADS, MOBA_DH))
            cv_p.append(v.reshape(bp, seq, MOBA_HEADS, MOBA_DH))
            qs, ks, vs = norm_mm(xs, norm_mix[i], _Layer(c_w_qkv, j), 3, ts, tag + "_mobaqkv_s")
            kn = _from_pm(ks, bs)
            vn = _from_pm(vs, bs)
            ck_s.append(kn.reshape(bs, dseq, MOBA_HEADS, MOBA_DH))
            cv_s.append(vn.reshape(bs, dseq, MOBA_HEADS, MOBA_DH))
            qt = _from_pm(qs, bs).reshape(bs, dseq * MOBA_HEADS, MOBA_DH)
            rows_kv = lambda a: a.reshape(bs, dseq * MOBA_HEADS, MOBA_DH)
            o, os_ = moba_both(q, k, v, bp, qt, rows_kv(kn), rows_kv(vn), cache_c_k[j], cache_c_v[j], page_table,
                               tag + "_moba")
            pre = (o, _Layer(c_w_out, j))
            xs = mm_res(_to_pm(os_.reshape(bs, dseq, MOBA_HEADS * MOBA_DH)), _Layer(c_w_out, j), xs, ts, tag + "_mobaout_s")

        (qs,) = norm_mm(xs, norm_xattn[i], _Layer(x_w_q, i), 1, ts, tag + "_xattnq_s")
        dh = d // XA_HEADS
        qt = _from_pm(qs, bs).reshape(bs, dseq * XA_HEADS, dh)
        xp, o = xattn_both(xp, norm_xattn[i], _Layer(x_w_q, i), mkbf, mvbf, i, _Layer(x_w_o, i),
                           qt, cache_mem_k, cache_mem_v, bp, tm, tag + "_xattn", pre=pre)
        xs = mm_res(_to_pm(o.reshape(bs, dseq, d)), _Layer(x_w_o, i), xs, ts, tag + "_xattnout_s")

        last = i == depth - 1
        xp, xs = ffn(xp, xs, norm_ffn[i], _Layer(f_w_up, i), _Layer(f_w_down, i), norm_final, last, tm, tag + "_ffn")

    mem_shape = (depth, bp, mem_len, XA_HEADS, d // XA_HEADS)
    return (xp.reshape(bp, seq, d), _from_pm(xs, bs),
            jnp.stack(a_p), jnp.stack(a_s),
            jnp.stack(bc_p), jnp.stack(bc_s),
            jnp.stack(br_p), jnp.stack(br_s),
            jnp.stack(ck_p), jnp.stack(cv_p), jnp.stack(ck_s), jnp.stack(cv_s),
            mk32.reshape(mem_shape), mv32.reshape(mem_shape))
```

```python
import functools
import math

import numpy as np
import jax
import jax.numpy as jnp
from jax import lax
from jax.experimental import pallas as pl
from jax.experimental.pallas import tpu as pltpu

F32 = jnp.float32
BF = jnp.bfloat16
SDS = jax.ShapeDtypeStruct

EPS = 1e-6
NEG_INF = -1e30
LANES = 128
SUBLANES = 8
VMEM_LIMIT = 56 * 1024 * 1024

ROW_TILE = 512
GDN_ROW_TILE = 256
GDN_SAMPLE_SEQS_PER_STEP = 8

GDN_HEADS = 8
GDN_DK = 128
GDN_CHUNK = 64
MOBA_HEADS = 8
MOBA_DH = 128
MOBA_BLOCK = 256
MOBA_TOPK = 3
PAGE_SIZE = 128
XA_HEADS = 4


def _params(n_axes):
    return pltpu.CompilerParams(dimension_semantics=("arbitrary",) * n_axes,
                                vmem_limit_bytes=VMEM_LIMIT)


def _const_spec(shape):
    nd = len(shape)
    return pl.BlockSpec(shape, lambda *_: (0,) * nd, pipeline_mode=pl.Buffered(1))


class _Layer:
    def __init__(self, arr, j):
        self.arr, self.j = arr, j

    @property
    def shape(self):
        return self.arr.shape[1:]


def _wspec(w):
    if not isinstance(w, _Layer):
        return _const_spec(w.shape)
    nd = len(w.shape)
    j = w.j
    return pl.BlockSpec((None,) + w.shape, lambda *_: (j,) + (0,) * nd, pipeline_mode=pl.Buffered(1))


def _warr(w):
    return w.arr if isinstance(w, _Layer) else w


def _whole_spec(shape):
    nd = len(shape)
    return pl.BlockSpec(shape, lambda *_: (0,) * nd)


def _dot(a, b):
    return jnp.dot(a, b, preferred_element_type=F32)


def _dot_nt(a, b):
    return lax.dot_general(a, b, (((1,), (1,)), ((), ())), preferred_element_type=F32)


def _dot_tn(a, b):
    return lax.dot_general(a, b, (((0,), (0,)), ((), ())), preferred_element_type=F32)


def _split3(a):
    a0 = a.astype(BF)
    r = a - a0.astype(F32)
    a1 = r.astype(BF)
    a2 = (r - a1.astype(F32)).astype(BF)
    return a0, a1, a2


def _dot_exact_rhs(a, b_bf, dot=_dot):
    a0, a1, a2 = _split3(a)
    return dot(a0, b_bf) + dot(a1, b_bf) + dot(a2, b_bf)


def _dot_hi(a, b, dot=_dot):
    a0, a1, _ = _split3(a)
    b0, b1, _ = _split3(b)
    return dot(a0, b0) + (dot(a0, b1) + dot(a1, b0))


def _rms(x, g):
    ms = jnp.mean(x * x, axis=-1, keepdims=True)
    return x * lax.rsqrt(ms + EPS) * g


def _silu(x):
    return x * jax.nn.sigmoid(x)


def _softplus(x):
    return jnp.maximum(x, 0.0) + jnp.log1p(jnp.exp(-jnp.abs(x)))


def _norm_mm_body(x_ref, g_ref, w_ref, *o_refs):
    h = _rms(x_ref[...], g_ref[...]).astype(BF)
    n = o_refs[0].shape[-1]
    for j, o_ref in enumerate(o_refs):
        o_ref[...] = _dot(h, w_ref[:, j * n:(j + 1) * n])


def norm_mm(x, g, w, n_out, tm, name):
    t, k = x.shape
    n = w.shape[1] // n_out
    return pl.pallas_call(
        _norm_mm_body,
        grid=(t // tm,),
        in_specs=[pl.BlockSpec((tm, k), lambda i: (i, 0)), _const_spec((1, k)), _wspec(w)],
        out_specs=[pl.BlockSpec((tm, n), lambda i: (i, 0))] * n_out,
        out_shape=[SDS((t, n), F32)] * n_out,
        compiler_params=_params(1),
        name=name,
    )(x, g.reshape(1, k), _warr(w))


def _mm_res_body(a_ref, w_ref, r_ref, o_ref):
    o_ref[...] = r_ref[...] + _dot(a_ref[...].astype(BF), w_ref[...])


def mm_res(a, w, res, tm, name):
    t, k = a.shape
    n = w.shape[1]
    return pl.pallas_call(
        _mm_res_body,
        grid=(t // tm,),
        in_specs=[pl.BlockSpec((tm, k), lambda i: (i, 0)), _wspec(w),
                  pl.BlockSpec((tm, n), lambda i: (i, 0))],
        out_specs=pl.BlockSpec((tm, n), lambda i: (i, 0)),
        out_shape=SDS((t, n), F32),
        compiler_params=_params(1),
        name=name,
    )(a, _warr(w), res)


def _memkv_body(x_ref, g_ref, w_ref, k32_ref, v32_ref, kbf_ref, vbf_ref):
    h = _rms(x_ref[...], g_ref[0]).astype(BF)
    d = kbf_ref.shape[-1]
    nh, dh = k32_ref.shape[2:]
    for j, (split_ref, bf_ref) in enumerate(((k32_ref, kbf_ref), (v32_ref, vbf_ref))):
        r = _dot(h, w_ref[0, :, j * d:(j + 1) * d])
        bf_ref[0] = r.astype(BF)
        for hh in range(nh):
            split_ref[0, :, hh, :] = r[:, hh * dh:(hh + 1) * dh]


def memory_kv_all(mem2d, norm_mem, w_kv, nh, tm):
    depth, d, _ = w_kv.shape
    t = mem2d.shape[0]
    flat = pl.BlockSpec((1, tm, d), lambda l, i: (l, i, 0))
    split = pl.BlockSpec((1, tm, nh, d // nh), lambda l, i: (l, i, 0, 0))
    return pl.pallas_call(
        _memkv_body,
        grid=(depth, t // tm),
        in_specs=[pl.BlockSpec((tm, d), lambda l, i: (i, 0)),
                  pl.BlockSpec((1, 1, d), lambda l, i: (l, 0, 0)),
                  pl.BlockSpec((1, d, 2 * d), lambda l, i: (l, 0, 0))],
        out_specs=[split, split, flat, flat],
        out_shape=[SDS((depth, t, nh, d // nh), F32)] * 2 + [SDS((depth, t, d), BF)] * 2,
        compiler_params=_params(2),
        name="memory_kv",
    )(mem2d, norm_mem.reshape(depth, 1, d), w_kv)


def _ffn_body(xp_ref, xs_ref, as_ref, ws_ref, g_ref, wu_ref, wd_ref, gf_ref, op_ref, os_ref, *, ck, final, ntp):
    def tile(x, o_ref):
        h = _rms(x, g_ref[...]).astype(BF)
        dff = wd_ref.shape[0]
        acc = x
        for c0 in range(0, dff, ck):
            gate = _dot(h, wu_ref[:, c0:c0 + ck])
            up = _dot(h, wu_ref[:, dff + c0:dff + c0 + ck])
            a = (_silu(gate) * up).astype(BF)
            acc = acc + _dot(a, wd_ref[c0:c0 + ck, :])
        if final:
            acc = _rms(acc, gf_ref[...])
        o_ref[...] = acc

    i = pl.program_id(0)
    pl.when(i < ntp)(lambda: tile(xp_ref[...], op_ref))
    pl.when(i == ntp)(lambda: tile(xs_ref[...] + _dot(as_ref[...].astype(BF), ws_ref[...]), os_ref))


def ffn(xp, xs, a_s, w_s, g, w_up, w_down, g_final, final, tm, name):
    t, d = xp.shape
    ntp = t // tm
    body = functools.partial(_ffn_body, ck=2 * LANES, final=final, ntp=ntp)
    p_spec = pl.BlockSpec((tm, d), lambda i: (jnp.minimum(i, ntp - 1), 0))
    s_spec = pl.BlockSpec(xs.shape, lambda i: (0, 0))
    return pl.pallas_call(
        body,
        grid=(ntp + 1,),
        in_specs=[p_spec, s_spec, pl.BlockSpec(a_s.shape, lambda i: (0, 0)), _wspec(w_s),
                  _const_spec((1, d)), _wspec(w_up), _wspec(w_down), _const_spec((1, d))],
        out_specs=[p_spec, s_spec],
        out_shape=[SDS((t, d), F32), SDS(xs.shape, F32)],
        compiler_params=_params(1),
        name=name,
    )(xp, xs, a_s, _warr(w_s), g.reshape(1, d), _warr(w_up), _warr(w_down), g_final.reshape(1, d))


def _xattn_p_part(x_ref, g_ref, wq_ref, k_ref, v_ref, wo_ref, o_ref, *, nh, pre_refs=None):
    x = x_ref[...]
    if pre_refs is not None:
        x = x + _dot(pre_refs[0][...].astype(BF), pre_refs[1][...])
    d = x.shape[-1]
    dh = d // nh
    h = _rms(x, g_ref[...]).astype(BF)
    q = (_dot(h, wq_ref[...]) * dh ** -0.5).astype(BF)
    outs = []
    for hh in range(nh):
        cs = slice(hh * dh, (hh + 1) * dh)
        s = _dot_nt(q[:, cs], k_ref[0, :, cs])
        p = jnp.exp(s - jnp.max(s, axis=-1, keepdims=True))
        l = jnp.sum(p, axis=-1, keepdims=True)
        outs.append((_dot(p.astype(BF), v_ref[0, :, cs]) / l).astype(BF))
    o_ref[...] = x + _dot(jnp.concatenate(outs, axis=-1), wo_ref[...])


def _xattn_s_part(qt_ref, k_ref, v_ref, o_ref, *, nb, nh):
    m_len, _, dh = k_ref.shape[2:]
    rows = m_len * nh
    ncol = qt_ref.shape[1]
    rh = lax.broadcasted_iota(jnp.int32, (rows, ncol), 0) % nh
    ch = lax.broadcasted_iota(jnp.int32, (rows, ncol), 1) % nh
    head_bias = jnp.where(rh == ch, 0.0, NEG_INF)
    s = [_dot_nt(k_ref[0, j].reshape(rows, dh).astype(BF), (qt_ref[j] * dh ** -0.5).astype(BF)) + head_bias
         for j in range(nb)]
    p = [jnp.exp(a - jnp.max(a, axis=0, keepdims=True)) for a in s]
    p = [(a / jnp.sum(a, axis=0, keepdims=True)).astype(BF) for a in p]
    for j in range(nb):
        o_ref[j] = _dot_tn(p[j], v_ref[0, j].reshape(rows, dh).astype(BF))


def _xattn_both_body(*refs, nb, has_pre):
    pre_refs = refs[:2] if has_pre else None
    x_ref, g_ref, wq_ref, k_ref, v_ref, wo_ref, qt_ref, ck_ref, cv_ref, o_ref, os_ref = refs[2 if has_pre else 0:]
    _xattn_s_part(qt_ref, ck_ref, cv_ref, os_ref, nb=nb, nh=ck_ref.shape[3])
    _xattn_p_part(x_ref, g_ref, wq_ref, k_ref, v_ref, wo_ref, o_ref, nh=XA_HEADS, pre_refs=pre_refs)


def xattn_both(x, g, w_q, k_bf, v_bf, layer, w_o, qt, cache_k, cache_v, bsz, tm, name, pre=None):
    t, d = x.shape
    nt = t // bsz // tm
    m_len = k_bf.shape[1] // bsz
    bs, ncol, dh = qt.shape
    nh_s = cache_k.shape[3]
    steps = bsz * nt
    assert bs % steps == 0, (bs, steps)
    nb = bs // steps
    kv = lambda a: a.reshape(a.shape[0] * bsz, m_len, d)
    row = lambda b, i: (b * nt + i, 0)
    cache_spec = pl.BlockSpec((1, nb, m_len, nh_s, dh), lambda b, i: (layer, b * nt + i, 0, 0, 0))
    pre_ins, pre_specs = [], []
    if pre is not None:
        pre_ins = [pre[0], _warr(pre[1])]
        pre_specs = [pl.BlockSpec((tm, pre[0].shape[1]), row), _wspec(pre[1])]
    return pl.pallas_call(
        functools.partial(_xattn_both_body, nb=nb, has_pre=pre is not None),
        grid=(bsz, nt),
        in_specs=pre_specs + [
            pl.BlockSpec((tm, d), row), _const_spec((1, d)), _wspec(w_q),
            pl.BlockSpec((1, m_len, d), lambda b, i: (layer * bsz + b, 0, 0)),
            pl.BlockSpec((1, m_len, d), lambda b, i: (layer * bsz + b, 0, 0)),
            _wspec(w_o),
            pl.BlockSpec((nb, ncol, dh), lambda b, i: (b * nt + i, 0, 0)), cache_spec, cache_spec],
        out_specs=[pl.BlockSpec((tm, d), row), pl.BlockSpec((nb, ncol, dh), lambda b, i: (b * nt + i, 0, 0))],
        out_shape=[SDS((t, d), F32), SDS((bs, ncol, dh), F32)],
        compiler_params=_params(2),
        name=name,
    )(*pre_ins, x, g.reshape(1, d), _warr(w_q), kv(k_bf), kv(v_bf), _warr(w_o), qt, cache_k, cache_v)


def _mixa_p_body(x_ref, g_ref, win_ref, wc_ref, wout_ref, o_ref, st_ref, zbuf, *, nt):
    i = pl.program_id(1)
    tm, d = x_ref.shape
    width = wc_ref.shape[0]

    @pl.when(i == 0)
    def _():
        zbuf[0:SUBLANES, :] = jnp.zeros((SUBLANES, d), F32)

    x = x_ref[...]
    h = _rms(x, g_ref[...]).astype(BF)
    gate_b = _dot(h, win_ref[:, 0:d])
    zn = _dot(h, win_ref[:, d:2 * d]) * _dot(h, win_ref[:, 2 * d:3 * d])
    zbuf[SUBLANES:SUBLANES + tm, :] = zn
    conv = wc_ref[width - 1:width, :] * zn
    for j in range(width - 1):
        off = SUBLANES - (width - 1) + j
        conv = conv + wc_ref[j:j + 1, :] * zbuf[off:off + tm, :]
    o_ref[...] = x + _dot((gate_b * conv).astype(BF), wout_ref[...])
    zbuf[0:SUBLANES, :] = zbuf[tm:tm + SUBLANES, :]

    @pl.when(i == nt - 1)
    def _():
        st_ref[0] = zbuf[SUBLANES - (width - 1):SUBLANES, :]


def mixer_a_prompt(x, g, w_in, w_conv, w_out, bsz, tm, name):
    t, d = x.shape
    nt = t // bsz // tm
    width = w_conv.shape[0]
    row = lambda b, i: (b * nt + i, 0)
    return pl.pallas_call(
        functools.partial(_mixa_p_body, nt=nt),
        grid=(bsz, nt),
        in_specs=[pl.BlockSpec((tm, d), row), _const_spec((1, d)), _wspec(w_in),
                  _const_spec(w_conv.shape), _wspec(w_out)],
        out_specs=[pl.BlockSpec((tm, d), row), pl.BlockSpec((1, width - 1, d), lambda b, i: (b, 0, 0))],
        out_shape=[SDS((t, d), F32), SDS((bsz, width - 1, d), F32)],
        scratch_shapes=[pltpu.VMEM((tm + SUBLANES, d), F32)],
        compiler_params=_params(2),
        name=name,
    )(x, g.reshape(1, d), _warr(w_in), w_conv, _warr(w_out))


def _posmajor_conv(state, z, wc_ref, bsz):
    width = wc_ref.shape[0]
    n = z.shape[0]
    zpad = jnp.concatenate([state, z], axis=0)
    conv = wc_ref[0:1, :] * zpad[0:n]
    for j in range(1, width):
        conv = conv + wc_ref[j:j + 1, :] * zpad[j * bsz:j * bsz + n]
    return conv, zpad[n:]


def _mixa_s_body(x_ref, st_ref, g_ref, win_ref, wc_ref, wout_ref, o_ref, nst_ref, *, bsz):
    x = x_ref[...]
    d = x.shape[-1]
    h = _rms(x, g_ref[...]).astype(BF)
    gate_b = _dot(h, win_ref[:, 0:d])
    zn = _dot(h, win_ref[:, d:2 * d]) * _dot(h, win_ref[:, 2 * d:3 * d])
    conv, new_state = _posmajor_conv(st_ref[...], zn, wc_ref, bsz)
    o_ref[...] = x + _dot((gate_b * conv).astype(BF), wout_ref[...])
    nst_ref[...] = new_state


def mixer_a_sample(x, state_pm, g, w_in, w_conv, w_out, bsz, name):
    t, d = x.shape
    return pl.pallas_call(
        functools.partial(_mixa_s_body, bsz=bsz),
        grid=(1,),
        in_specs=[_const_spec(x.shape), _const_spec(state_pm.shape), _const_spec((1, d)),
                  _wspec(w_in), _const_spec(w_conv.shape), _wspec(w_out)],
        out_specs=[_whole_spec(x.shape), _whole_spec(state_pm.shape)],
        out_shape=[SDS(x.shape, F32), SDS(state_pm.shape, F32)],
        compiler_params=_params(1),
        name=name,
    )(x, state_pm, g.reshape(1, d), _warr(w_in), w_conv, _warr(w_out))


GDN_COL_CHUNK = 2 * LANES


def _gdn_qkv_chunk(conv, c0, q_ref, k_ref, v_ref):
    nh, dk = GDN_HEADS, GDN_DK
    hd = nh * dk
    act = _silu(conv)
    section, col = c0 // hd, c0 % hd
    if section == 2:
        v_ref[:, col:col + GDN_COL_CHUNK] = act
        return
    ref = q_ref if section == 0 else k_ref
    for j in range(GDN_COL_CHUNK // dk):
        t = act[:, j * dk:(j + 1) * dk]
        t = t * lax.rsqrt(jnp.sum(t * t, axis=-1, keepdims=True) + EPS)
        ref[:, col + j * dk:col + (j + 1) * dk] = t * dk ** -0.5 if section == 0 else t


def _gdn_gates(ba, alog_ref, dtb_ref, bg_ref):
    nh = GDN_HEADS
    lane = lax.broadcasted_iota(jnp.int32, ba.shape, 1)
    beta = jax.nn.sigmoid(ba)
    g = -jnp.exp(alog_ref[...]) * _softplus(ba + dtb_ref[...])
    bg_ref[...] = jnp.where(lane < nh, beta, jnp.where(lane < 2 * nh, g, 0.0))


def _gdn_proj_p_body(x_ref, g_ref, w_ref, wba_ref, wc_ref, alog_ref, dtb_ref,
                     q_ref, k_ref, v_ref, z_ref, bg_ref, st_ref, cbuf, *, nt):
    i = pl.program_id(1)
    tm = x_ref.shape[0]
    width, cc = wc_ref.shape
    ck = GDN_COL_CHUNK

    @pl.when(i == 0)
    def _():
        cbuf[0:SUBLANES, :] = jnp.zeros((SUBLANES, cc), F32)

    h = _rms(x_ref[...], g_ref[...]).astype(BF)
    for c0 in range(0, cc, ck):
        cols = slice(c0, c0 + ck)
        pre = _dot(h, w_ref[:, cols])
        cbuf[SUBLANES:SUBLANES + tm, cols] = pre
        conv = wc_ref[width - 1:width, cols] * pre
        for j in range(width - 1):
            off = SUBLANES - (width - 1) + j
            conv = conv + wc_ref[j:j + 1, cols] * cbuf[off:off + tm, cols]
        _gdn_qkv_chunk(conv, c0, q_ref, k_ref, v_ref)
    for c0 in range(0, z_ref.shape[1], ck):
        z_ref[:, c0:c0 + ck] = _dot(h, w_ref[:, cc + c0:cc + c0 + ck])
    _gdn_gates(_dot(h, wba_ref[...]), alog_ref, dtb_ref, bg_ref)
    cbuf[0:SUBLANES, :] = cbuf[tm:tm + SUBLANES, :]

    @pl.when(i == nt - 1)
    def _():
        st_ref[0] = cbuf[SUBLANES - (width - 1):SUBLANES, :]


def gdn_proj_prompt(x, g, w_qkvz, w_ba, w_conv, alog, dtb, bsz, tm, name):
    t, d = x.shape
    nt = t // bsz // tm
    width, cc = w_conv.shape
    hd = GDN_HEADS * GDN_DK
    row = lambda b, i: (b * nt + i, 0)
    tile = lambda n: pl.BlockSpec((tm, n), row)
    return pl.pallas_call(
        functools.partial(_gdn_proj_p_body, nt=nt),
        grid=(bsz, nt),
        in_specs=[tile(d), _const_spec((1, d)), _wspec(w_qkvz), _wspec(w_ba),
                  _const_spec(w_conv.shape), _const_spec(alog.shape), _const_spec(dtb.shape)],
        out_specs=[tile(hd), tile(hd), tile(hd), tile(hd), tile(LANES),
                   pl.BlockSpec((1, width - 1, cc), lambda b, i: (b, 0, 0))],
        out_shape=[SDS((t, hd), F32)] * 4 + [SDS((t, LANES), F32), SDS((bsz, width - 1, cc), F32)],
        scratch_shapes=[pltpu.VMEM((tm + SUBLANES, cc), F32)],
        compiler_params=_params(2),
        name=name,
    )(x, g.reshape(1, d), _warr(w_qkvz), _warr(w_ba), w_conv, alog, dtb)


def _gdn_proj_s_body(x_ref, st_ref, g_ref, w_ref, wba_ref, wc_ref, alog_ref, dtb_ref,
                     q_ref, k_ref, v_ref, z_ref, bg_ref, nst_ref, *, bsz):
    cc = wc_ref.shape[1]
    h = _rms(x_ref[...], g_ref[...]).astype(BF)
    qkv = _dot(h, w_ref[:, :cc])
    z_ref[...] = _dot(h, w_ref[:, cc:cc + z_ref.shape[1]])
    ba = _dot(h, wba_ref[...])
    conv, new_state = _posmajor_conv(st_ref[...], qkv, wc_ref, bsz)
    for c0 in range(0, cc, GDN_COL_CHUNK):
        _gdn_qkv_chunk(conv[:, c0:c0 + GDN_COL_CHUNK], c0, q_ref, k_ref, v_ref)
    _gdn_gates(ba, alog_ref, dtb_ref, bg_ref)
    nst_ref[...] = new_state


def gdn_proj_sample(x, state_pm, g, w_qkvz, w_ba, w_conv, alog, dtb, bsz, name):
    t, d = x.shape
    cc = w_conv.shape[1]
    hd = GDN_HEADS * GDN_DK
    ins = (x, state_pm, g.reshape(1, d), w_qkvz, w_ba, w_conv, alog, dtb)
    outs = [SDS((t, hd), F32)] * 4 + [SDS((t, LANES), F32), SDS(state_pm.shape, F32)]
    return pl.pallas_call(
        functools.partial(_gdn_proj_s_body, bsz=bsz),
        grid=(1,),
        in_specs=[_wspec(a) for a in ins],
        out_specs=[_whole_spec(o.shape) for o in outs],
        out_shape=outs,
        compiler_params=_params(1),
        name=name,
    )(*[_warr(a) for a in ins])


def _pad_transpose(a):
    c = a.shape[0]
    if c < LANES:
        a = jnp.concatenate([a, jnp.zeros((LANES - c, a.shape[1]), a.dtype)], axis=0)
    return a.T


def _gdn_scan_body(*refs, c, per_chunk_state, fuse_out):
    if fuse_out:
        x_ref, refs = refs[0], refs[1:]
    (q_ref, k_ref, v_ref, z_ref, bg_ref, tri_ref, s0_ref, gn_ref) = refs[:8]
    refs = refs[8:]
    if fuse_out:
        wout_ref, refs = refs[0], refs[1:]
    o_ref, s_ref, qg_sc, kdec_sc, rk_sc, rv_sc, gam_sc, beta_sc, egl_sc, o_sc = refs

    nh, dk = GDN_HEADS, GDN_DK
    rows, hd = q_ref.shape
    n_chunks = rows // c
    n_double = int(math.log2(c)) - 1

    if not per_chunk_state:
        @pl.when(pl.program_id(1) == 0)
        def _():
            s_ref[...] = s0_ref[...]

    bg = bg_ref[...]
    gam = _dot_exact_rhs(bg, tri_ref[...], dot=lambda a, b: _dot(b, a))
    beta_b = jnp.concatenate([jnp.broadcast_to(bg[:, hh:hh + 1], (rows, dk)) for hh in range(nh)], axis=1)
    gam_b = jnp.concatenate([jnp.broadcast_to(gam[:, nh + hh:nh + hh + 1], (rows, dk)) for hh in range(nh)], axis=1)
    gl_b = jnp.broadcast_to(gam_b.reshape(n_chunks, c, hd)[:, c - 1:c, :], (n_chunks, c, hd)).reshape(rows, hd)
    eg = jnp.exp(gam_b)
    k = k_ref[...]
    qg_sc[...] = q_ref[...] * eg
    rk_sc[...] = beta_b * eg * k
    rv_sc[...] = beta_b * v_ref[...]
    kdec_sc[...] = k * jnp.exp(gl_b - gam_b)
    gam_sc[...] = gam_b
    beta_sc[...] = beta_b
    egl_sc[...] = jnp.exp(gl_b)

    ii = lax.broadcasted_iota(jnp.int32, (c, c), 0)
    jj = lax.broadcasted_iota(jnp.int32, (c, c), 1)
    incl = ii >= jj
    strict = ii > jj

    chains = [(slice(ci * c, (ci + 1) * c), slice(hh * dk, (hh + 1) * dk))
              for ci in range(n_chunks) for hh in range(nh)]
    kq, decay, pm = [], [], []
    for rs, hc in chains:
        kc = k_ref[rs, hc]
        kq.append(_dot_nt(jnp.concatenate([kc, q_ref[rs, hc]], axis=0).astype(BF), kc.astype(BF)))
    for (rs, hc), kq_i in zip(chains, kq):
        gcol = gam_sc[rs, hc]
        diff = gcol[:, :c] - _pad_transpose(gcol)[:c, :c]
        dec = jnp.where(incl, jnp.exp(jnp.where(incl, diff, 0.0)), 0.0)
        decay.append(dec)
        pm.append(jnp.where(strict, -(beta_sc[rs, hc][:, :c] * kq_i[:c] * dec), 0.0))
    mk = pm
    for _ in range(n_double):
        mk = [_dot(a.astype(BF), a.astype(BF)) for a in mk]
        pm = [p + a + _dot(p.astype(BF), a.astype(BF)) for p, a in zip(pm, mk)]
    sol, lhs2 = [], []
    for (rs, hc), p, kq_i, dec in zip(chains, pm, kq, decay):
        rhs = jnp.concatenate([rv_sc[rs, hc], rk_sc[rs, hc]], axis=1)
        sol.append(rhs + _dot(p.astype(BF), rhs.astype(BF)))
        kdt = _pad_transpose(kdec_sc[rs, hc])[:, :c]
        lhs2.append(jnp.concatenate([kq_i[c:] * dec, kdt], axis=0).astype(BF))

    def advance(idxs, states):
        t1 = []
        for idx, s in zip(idxs, states):
            rs, hc = chains[idx]
            t1.append(_dot(jnp.concatenate([sol[idx][:, dk:], qg_sc[rs, hc]], axis=0).astype(BF), s.astype(BF)))
        t2 = [_dot(lhs2[idx], (sol[idx][:, :dk] - t[:c]).astype(BF)) for idx, t in zip(idxs, t1)]
        new_states = []
        for idx, s, ta, tb in zip(idxs, states, t1, t2):
            rs, hc = chains[idx]
            o_sc[rs, hc] = ta[c:] + tb[:c]
            egl = jnp.broadcast_to(egl_sc[rs.start:rs.start + 1, hc], (dk, dk))
            new_states.append(egl * s + tb[c:])
        return new_states

    if per_chunk_state:
        idxs = list(range(len(chains)))
        new_states = advance(idxs, [s0_ref[idx // nh, idx % nh] for idx in idxs])
        for idx, s_new in zip(idxs, new_states):
            s_ref[idx // nh, idx % nh] = s_new
    else:
        state = [s_ref[0, hh] for hh in range(nh)]
        for ci in range(n_chunks):
            state = advance([ci * nh + hh for hh in range(nh)], state)
        for hh in range(nh):
            s_ref[0, hh] = state[hh]

    outs = []
    for hh in range(nh):
        hc = slice(hh * dk, (hh + 1) * dk)
        outs.append(_rms(o_sc[:, hc], gn_ref[...]) * _silu(z_ref[:, hc]))
    yv = jnp.concatenate(outs, axis=-1)
    if fuse_out:
        o_ref[...] = x_ref[...] + _dot(yv.astype(BF), wout_ref[...])
    else:
        o_ref[...] = yv


def _chunk_cumsum_matrix(rows, c):
    r = np.arange(rows)
    return jnp.asarray((r[:, None] >= r[None, :]) & (r[:, None] // c == r[None, :] // c), BF)


def gdn_scan(x, q, k, v, z, bg, s0, g_norm, w_out, *, c, rows, n_seq, per_chunk_state, name):
    t, hd = q.shape
    nt = t // n_seq // rows
    fuse_out = w_out is not None
    tri = _chunk_cumsum_matrix(rows, c)
    ns = rows // c if per_chunk_state else 1
    row = lambda b, i: (b * nt + i, 0)
    tile = lambda n: pl.BlockSpec((rows, n), row)
    st_spec = pl.BlockSpec((ns,) + s0.shape[1:], lambda b, i: (b, 0, 0, 0))
    ins, specs = [], []
    if fuse_out:
        ins.append(x)
        specs.append(tile(x.shape[1]))
    ins += [q, k, v, z, bg, tri, s0, g_norm.reshape(1, GDN_DK)]
    specs += [tile(hd)] * 4 + [tile(LANES), _const_spec(tri.shape), st_spec, _const_spec((1, GDN_DK))]
    d_out = hd
    if fuse_out:
        ins.append(_warr(w_out))
        specs.append(_wspec(w_out))
        d_out = w_out.shape[1]
    return pl.pallas_call(
        functools.partial(_gdn_scan_body, c=c, per_chunk_state=per_chunk_state, fuse_out=fuse_out),
        grid=(n_seq, nt),
        in_specs=specs,
        out_specs=[tile(d_out), st_spec],
        out_shape=[SDS((t, d_out), F32), SDS(s0.shape, F32)],
        scratch_shapes=[pltpu.VMEM((rows, hd), F32)] * 8,
        compiler_params=_params(2),
        name=name,
    )(*ins)


def _topk_mask(sb, valid, axis):
    n = sb.shape[axis]
    idx = lax.broadcasted_iota(jnp.int32, sb.shape, axis)
    cnt = jnp.zeros(sb.shape, F32)
    for m in range(n):
        sm = lax.slice_in_dim(sb, m, m + 1, axis=axis)
        ahead = (sm > sb) | ((sm == sb) & (m < idx))
        cnt = cnt + jnp.where(ahead, 1.0, 0.0)
    return jnp.where((cnt < MOBA_TOPK) & valid, 1.0, 0.0)


def _moba_p_body(q_ref, k_ref, v_ref, oh_ref, o_ref, *, blocks):
    length, dh = k_ref.shape
    blk = MOBA_BLOCK
    nb = length // blk
    scale = dh ** -0.5
    k = k_ref[...]
    kaug = jnp.concatenate([k.astype(BF), oh_ref[...]], axis=1)
    vb = v_ref[...].astype(BF)
    km = jnp.sum(k.reshape(nb, blk, dh), axis=1) * (1.0 / blk)
    brow = lax.broadcasted_iota(jnp.int32, (nb, blk), 0)
    rr = lax.broadcasted_iota(jnp.int32, (blk, blk), 0)
    cc = lax.broadcasted_iota(jnp.int32, (blk, blk), 1)
    causal = cc <= rr
    zero_bias = jnp.zeros((blk, dh), BF)
    pad_rows = jnp.zeros((LANES - nb, blk), F32)
    qaug = {}
    for i in blocks:
        q = q_ref[i * blk:(i + 1) * blk, :]
        qs = (q * scale).astype(BF)
        if i == 0:
            qaug[i] = jnp.concatenate([qs, zero_bias], axis=1)
            continue
        sbt = jnp.where(brow < i, _dot_hi(km, q, dot=_dot_nt), NEG_INF)
        selt = _topk_mask(sbt, sbt > NEG_INF / 2, axis=0)
        bias_t = jnp.where((selt > 0.5) | (brow == i), 0.0, NEG_INF)
        bias = jnp.concatenate([bias_t, pad_rows], axis=0).T
        qaug[i] = jnp.concatenate([qs, bias.astype(BF)], axis=1)
    for i in blocks:
        n_keys = (i + 1) * blk
        s = _dot_nt(qaug[i], kaug[:n_keys])
        s_own = jnp.where(causal, s[:, i * blk:], NEG_INF)
        s = jnp.concatenate([s[:, :i * blk], s_own], axis=-1) if i > 0 else s_own
        pe = jnp.exp(s - jnp.max(s, axis=-1, keepdims=True))
        l = jnp.sum(pe, axis=-1, keepdims=True)
        o_ref[i * blk:(i + 1) * blk, :] = _dot(pe.astype(BF), vb[:n_keys]) / l


def _balanced_block_groups(nb, parts):
    total = nb * (nb + 1) // 2
    groups, start, done = [], 0, 0
    for part in range(parts):
        end = start
        target = total * (part + 1) / parts
        while end < nb and (done + end + 1 <= target or end == start):
            done += end + 1
            end += 1
        if part == parts - 1:
            end = nb
        groups.append(tuple(range(start, end)))
        start = end
    return groups


def _lane_group_reduce(row, op, ncol):
    a = jnp.broadcast_to(row, (SUBLANES, LANES))
    shift = ncol
    while shift < LANES:
        a = op(a, pltpu.roll(a, shift, 1))
        shift *= 2
    return a[:1]


def _moba_s_body(pt_ref, qt_ref, hb_ref, kn_ref, vn_ref, *refs, n_pages, ncol):
    del pt_ref
    kp = refs[:n_pages]
    vp = refs[n_pages:2 * n_pages]
    o_ref, s_sc = refs[2 * n_pages:]
    page, nh, dh = kp[0].shape[1:]
    bp = MOBA_BLOCK // page
    nb = n_pages // bp
    prow = page * nh
    pack = LANES // ncol
    q_rep = jnp.concatenate([qt_ref[0]] * pack, axis=0)
    lane = lax.broadcasted_iota(jnp.int32, (1, LANES), 1)
    group = lane // ncol
    qs = q_rep * dh ** -0.5
    rg = lax.broadcasted_iota(jnp.int32, (LANES, pack * dh), 0) // ncol
    cg = lax.broadcasted_iota(jnp.int32, (LANES, pack * dh), 1) // dh
    rhs = jnp.where(rg == cg, jnp.concatenate([qs] * pack, axis=1), 0.0).astype(BF)

    km = []
    for n in range(nb):
        tot = kp[n * bp][0].sum(axis=0)
        for j in range(1, bp):
            tot = tot + kp[n * bp + j][0].sum(axis=0)
        km.append(tot * (1.0 / MOBA_BLOCK))
    km2 = jnp.concatenate(km, axis=0)
    sbm = _dot_hi(km2, q_rep, dot=_dot_nt)
    rh = lax.broadcasted_iota(jnp.int32, sbm.shape, 0) % nh
    ch = lax.broadcasted_iota(jnp.int32, sbm.shape, 1) % ncol % nh
    sb = jnp.sum(jnp.where(rh == ch, sbm, 0.0).reshape(nb, nh, LANES), axis=1)
    sel = _topk_mask(sb, jnp.full(sb.shape, True), axis=0)
    sel_bias = jnp.where(sel > 0.5, 0.0, NEG_INF)

    n_own = kn_ref.shape[1]
    ro = lax.broadcasted_iota(jnp.int32, (n_own, LANES), 0)
    co = lax.broadcasted_iota(jnp.int32, (n_own, LANES), 1)
    own_ok = (ro % nh == co % nh) & (ro // nh <= co // nh) & (co < ncol)
    s_own = jnp.where(own_ok, _dot_nt(kn_ref[0].astype(BF), qs.astype(BF)), NEG_INF)
    mx = jnp.max(s_own, axis=0, keepdims=True)

    n_groups = n_pages // pack
    for j in range(n_groups):
        pages = range(j * pack, (j + 1) * pack)
        k4 = jnp.concatenate([kp[p][0].reshape(prow, dh).astype(BF) for p in pages], axis=1)
        bias = sel_bias[(j * pack) // bp:(j * pack) // bp + 1, :]
        for g in range(1, pack):
            blk = (j * pack + g) // bp
            bias = jnp.where(group >= g, sel_bias[blk:blk + 1, :], bias)
        s = _dot_nt(k4, rhs) + (hb_ref[...] + bias)
        s_sc[j * prow:(j + 1) * prow, :] = s
        mx = jnp.maximum(mx, jnp.max(s, axis=0, keepdims=True))
    mx = _lane_group_reduce(mx, jnp.maximum, ncol)

    pe = jnp.exp(s_own - mx)
    l = jnp.sum(pe, axis=0, keepdims=True)
    acc = _dot_tn(pe.astype(BF), vn_ref[0].astype(BF))[:ncol]
    for j in range(n_groups):
        pages = range(j * pack, (j + 1) * pack)
        pe = jnp.exp(s_sc[j * prow:(j + 1) * prow, :] - mx)
        l = l + jnp.sum(pe, axis=0, keepdims=True)
        v4 = jnp.concatenate([vp[p][0].reshape(prow, dh).astype(BF) for p in pages], axis=1)
        r = _dot_tn(pe.astype(BF), v4)
        for g in range(pack):
            acc = acc + r[g * ncol:(g + 1) * ncol, g * dh:(g + 1) * dh]
    l = _lane_group_reduce(l, jnp.add, ncol)
    l_col = jnp.broadcast_to(l, (LANES, LANES)).T[:ncol, :1]
    o_ref[0] = acc / l_col


def _moba_both_body(pt_ref, q_ref, k_ref, v_ref, oh_ref, qt_ref, hb_ref, kn_ref, vn_ref, *refs,
                    n_pages, ncol, groups):
    pages = refs[:2 * n_pages]
    o_ref, os_ref, s_sc = refs[2 * n_pages:]
    _moba_s_body(pt_ref, qt_ref, hb_ref, kn_ref, vn_ref, *pages, os_ref, s_sc, n_pages=n_pages, ncol=ncol)
    part = pl.program_id(2)
    for gi, blocks in enumerate(groups):
        pl.when(part == gi)(functools.partial(_moba_p_body, q_ref, k_ref, v_ref, oh_ref, o_ref, blocks=blocks))


def moba_both(q, k, v, bsz, qt, kn, vn, pool_k, pool_v, page_table, name):
    t, w = q.shape
    dh = MOBA_DH
    nh = w // dh
    length = t // bsz
    nb = length // MOBA_BLOCK
    bs, ncol, _ = qt.shape
    assert bs % (bsz * nh) == 0, (bs, bsz, nh)
    parts = bs // (bsz * nh)
    assert 1 <= parts <= nb, (parts, nb)
    groups = _balanced_block_groups(nb, parts)
    n_pages = page_table.shape[1]
    page, nh_s = pool_k.shape[1:3]
    n_own = kn.shape[1]
    pack = LANES // ncol
    onehot = jnp.asarray(np.arange(length)[:, None] // MOBA_BLOCK == np.arange(dh)[None, :], BF)
    r = np.arange(page * nh_s)[:, None]
    c = np.arange(LANES)[None, :]
    head_bias = jnp.asarray(np.where(r % nh_s == c % ncol % nh_s, 0.0, NEG_INF), F32)

    seq = lambda b, h, p, pt: (b * nh + h) * parts + p
    p_spec = pl.BlockSpec((length, dh), lambda b, h, p, pt: (b, h))

    def page_spec(pg):
        return pl.BlockSpec((1, page, nh_s, dh), lambda b, h, p, pt: (pt[seq(b, h, p, pt), pg], 0, 0, 0))

    s_spec = lambda rows: pl.BlockSpec((1, rows, dh), lambda b, h, p, pt: (seq(b, h, p, pt), 0, 0))
    grid_spec = pltpu.PrefetchScalarGridSpec(
        num_scalar_prefetch=1,
        grid=(bsz, nh, parts),
        in_specs=[p_spec, p_spec, p_spec,
                  pl.BlockSpec(onehot.shape, lambda b, h, p, pt: (0, 0)),
                  s_spec(ncol),
                  pl.BlockSpec(head_bias.shape, lambda b, h, p, pt: (0, 0)),
                  s_spec(n_own), s_spec(n_own)]
                 + [page_spec(pg) for pg in range(n_pages)] * 2,
        out_specs=[p_spec, s_spec(ncol)],
        scratch_shapes=[pltpu.VMEM((n_pages // pack * page * nh_s, LANES), F32)],
    )
    return pl.pallas_call(
        functools.partial(_moba_both_body, n_pages=n_pages, ncol=ncol, groups=groups),
        grid_spec=grid_spec,
        out_shape=[SDS((t, w), F32), SDS((bs, ncol, dh), F32)],
        compiler_params=_params(3),
        name=name,
    )(page_table, q, k, v, onehot, qt, head_bias, kn, vn, *([pool_k] * n_pages), *([pool_v] * n_pages))


def _to_pm(a):
    b, l, c = a.shape
    return jnp.transpose(a, (1, 0, 2)).reshape(l * b, c)


def _from_pm(a, bsz):
    n, c = a.shape
    return jnp.transpose(a.reshape(n // bsz, bsz, c), (1, 0, 2))


def _pad_lanes(vec, offset):
    out = jnp.zeros((1, LANES), F32)
    return lax.dynamic_update_slice(out, vec.reshape(1, -1).astype(F32), (0, offset))


def kernel(x_prompt, x_sample, state_a_conv, state_b_conv, state_b_rec, cache_c_k, cache_c_v, cache_mem_k, cache_mem_v, page_table, mem_prompt, norm_mix, norm_mem, norm_xattn, norm_ffn, norm_final, a_w_in, a_w_conv, a_w_out, b_w_in, b_w_conv, b_a_log, b_dt_bias, b_norm, b_w_out, c_w_qkv, c_w_out, x_w_q, x_w_kv, x_w_o, f_w_up, f_w_down):
    bp, seq, d = x_prompt.shape
    bs, dseq, _ = x_sample.shape
    depth = norm_mix.shape[0]
    n_mixers = 3
    tm = ROW_TILE
    ts = bs * dseq

    bf = lambda w: w.astype(BF)
    a_w_in, a_w_out, b_w_out, c_w_qkv, c_w_out = map(bf, (a_w_in, a_w_out, b_w_out, c_w_qkv, c_w_out))
    x_w_q, x_w_kv, x_w_o, f_w_up, f_w_down = map(bf, (x_w_q, x_w_kv, x_w_o, f_w_up, f_w_down))
    cc = b_w_conv.shape[-1]
    hd = GDN_HEADS * GDN_DK
    b_w_qkvz = bf(b_w_in)
    b_w_ba = bf(jnp.pad(b_w_in[:, :, cc + hd:], ((0, 0), (0, 0), (0, LANES - 2 * GDN_HEADS))))

    xp = x_prompt.reshape(bp * seq, d)
    xs = _to_pm(x_sample)

    mem_len = mem_prompt.shape[1]
    mk32, mv32, mkbf, mvbf = memory_kv_all(mem_prompt.reshape(bp * mem_len, d), norm_mem, x_w_kv, XA_HEADS, tm)

    a_p, a_s, bc_p, bc_s, br_p, br_s = [], [], [], [], [], []
    ck_p, cv_p, ck_s, cv_s = [], [], [], []
    for i in range(depth):
        kind, j = i % n_mixers, i // n_mixers
        tag = f"l{i}"
        pre = None
        if kind == 0:
            xp, st = mixer_a_prompt(xp, norm_mix[i], _Layer(a_w_in, j), a_w_conv[j], _Layer(a_w_out, j), bp, tm, tag + "_mixa_p")
            a_p.append(st)
            st_pm = _to_pm(state_a_conv[j])
            xs, nst = mixer_a_sample(xs, st_pm, norm_mix[i], _Layer(a_w_in, j), a_w_conv[j], _Layer(a_w_out, j), bs, tag + "_mixa_s")
            a_s.append(_from_pm(nst, bs))
        elif kind == 1:
            alog = _pad_lanes(b_a_log[j], GDN_HEADS)
            dtb = _pad_lanes(b_dt_bias[j], GDN_HEADS)
            q, k, v, z, bg, st = gdn_proj_prompt(xp, norm_mix[i], _Layer(b_w_qkvz, j), _Layer(b_w_ba, j), b_w_conv[j], alog, dtb,
                                                 bp, GDN_ROW_TILE, tag + "_gdnproj_p")
            bc_p.append(st)
            s0 = jnp.zeros((bp, GDN_HEADS, GDN_DK, GDN_DK), F32)
            xp, s_fin = gdn_scan(xp, q, k, v, z, bg, s0, b_norm[j], _Layer(b_w_out, j), c=math.gcd(seq, GDN_CHUNK),
                                 rows=GDN_ROW_TILE, n_seq=bp, per_chunk_state=False, name=tag + "_gdnscan_p")
            br_p.append(s_fin)
            st_pm = _to_pm(state_b_conv[j])
            q, k, v, z, bg, nst = gdn_proj_sample(xs, st_pm, norm_mix[i], _Layer(b_w_qkvz, j), _Layer(b_w_ba, j), b_w_conv[j],
                                                  alog, dtb, bs, tag + "_gdnproj_s")
            bc_s.append(_from_pm(nst, bs))
            cpad = SUBLANES
            padded = lambda a: jnp.pad(_from_pm(a, bs), ((0, 0), (0, cpad - dseq), (0, 0))).reshape(bs * cpad, -1)
            nbatch = GDN_SAMPLE_SEQS_PER_STEP
            y, s_fin = gdn_scan(None, padded(q), padded(k), padded(v), padded(z), padded(bg), state_b_rec[j],
                                b_norm[j], None, c=cpad, rows=nbatch * cpad, n_seq=bs // nbatch,
                                per_chunk_state=True, name=tag + "_gdnscan_s")
            br_s.append(s_fin)
            y = _to_pm(y.reshape(bs, cpad, hd)[:, :dseq])
            xs = mm_res(y, _Layer(b_w_out, j), xs, ts, tag + "_gdnout_s")
        else:
            q, k, v = norm_mm(xp, norm_mix[i], _Layer(c_w_qkv, j), 3, tm, tag + "_mobaqkv_p")
            ck_p.append(k.reshape(bp, seq, MOBA_HEADS, MOBA_DH))
            cv_p.append(v.reshape(bp, seq, MOBA_HEADS, MOBA_DH))
            qs, ks, vs = norm_mm(xs, norm_mix[i], _Layer(c_w_qkv, j), 3, ts, tag + "_mobaqkv_s")
            kn = _from_pm(ks, bs)
            vn = _from_pm(vs, bs)
            ck_s.append(kn.reshape(bs, dseq, MOBA_HEADS, MOBA_DH))
            cv_s.append(vn.reshape(bs, dseq, MOBA_HEADS, MOBA_DH))
            qt = _from_pm(qs, bs).reshape(bs, dseq * MOBA_HEADS, MOBA_DH)
            rows_kv = lambda a: a.reshape(bs, dseq * MOBA_HEADS, MOBA_DH)
            o, os_ = moba_both(q, k, v, bp, qt, rows_kv(kn), rows_kv(vn), cache_c_k[j], cache_c_v[j], page_table,
                               tag + "_moba")
            pre = (o, _Layer(c_w_out, j))
            xs = mm_res(_to_pm(os_.reshape(bs, dseq, MOBA_HEADS * MOBA_DH)), _Layer(c_w_out, j), xs, ts, tag + "_mobaout_s")

        (qs,) = norm_mm(xs, norm_xattn[i], _Layer(x_w_q, i), 1, ts, tag + "_xattnq_s")
        dh = d // XA_HEADS
        qt = _from_pm(qs, bs).reshape(bs, dseq * XA_HEADS, dh)
        xp, o = xattn_both(xp, norm_xattn[i], _Layer(x_w_q, i), mkbf, mvbf, i, _Layer(x_w_o, i),
                           qt, cache_mem_k, cache_mem_v, bp, tm, tag + "_xattn", pre=pre)

        last = i == depth - 1
        xp, xs = ffn(xp, xs, _to_pm(o.reshape(bs, dseq, d)), _Layer(x_w_o, i), norm_ffn[i],
                     _Layer(f_w_up, i), _Layer(f_w_down, i), norm_final, last, tm, tag + "_ffn")

    mem_shape = (depth, bp, mem_len, XA_HEADS, d // XA_HEADS)
    return (xp.reshape(bp, seq, d), _from_pm(xs, bs),
            jnp.stack(a_p), jnp.stack(a_s),
            jnp.stack(bc_p), jnp.stack(bc_s),
            jnp.stack(br_p), jnp.stack(br_s),
            jnp.stack(ck_p), jnp.stack(cv_p), jnp.stack(ck_s), jnp.stack(cv_s),
            mk32.reshape(mem_shape), mv32.reshape(mem_shape))
```

```python
import functools
import math

import numpy as np
import jax
import jax.numpy as jnp
from jax import lax
from jax.experimental import pallas as pl
from jax.experimental.pallas import tpu as pltpu

F32 = jnp.float32
BF = jnp.bfloat16
SDS = jax.ShapeDtypeStruct

EPS = 1e-6
NEG_INF = -1e30
LANES = 128
SUBLANES = 8
VMEM_LIMIT = 56 * 1024 * 1024

ROW_TILE = 512
GDN_ROW_TILE = 256
GDN_SAMPLE_SEQS_PER_STEP = 8

GDN_HEADS = 8
GDN_DK = 128
GDN_CHUNK = 64
MOBA_HEADS = 8
MOBA_DH = 128
MOBA_BLOCK = 256
MOBA_TOPK = 3
PAGE_SIZE = 128
XA_HEADS = 4


def _params(n_axes):
    return pltpu.CompilerParams(dimension_semantics=("arbitrary",) * n_axes,
                                vmem_limit_bytes=VMEM_LIMIT)


def _const_spec(shape):
    nd = len(shape)
    return pl.BlockSpec(shape, lambda *_: (0,) * nd, pipeline_mode=pl.Buffered(1))


class _Layer:
    def __init__(self, arr, j):
        self.arr, self.j = arr, j

    @property
    def shape(self):
        return self.arr.shape[1:]


def _wspec(w):
    if not isinstance(w, _Layer):
        return _const_spec(w.shape)
    nd = len(w.shape)
    j = w.j
    return pl.BlockSpec((None,) + w.shape, lambda *_: (j,) + (0,) * nd, pipeline_mode=pl.Buffered(1))


def _warr(w):
    return w.arr if isinstance(w, _Layer) else w


def _whole_spec(shape):
    nd = len(shape)
    return pl.BlockSpec(shape, lambda *_: (0,) * nd)


def _dot(a, b):
    return jnp.dot(a, b, preferred_element_type=F32)


def _dot_nt(a, b):
    return lax.dot_general(a, b, (((1,), (1,)), ((), ())), preferred_element_type=F32)


def _dot_tn(a, b):
    return lax.dot_general(a, b, (((0,), (0,)), ((), ())), preferred_element_type=F32)


def _split3(a):
    a0 = a.astype(BF)
    r = a - a0.astype(F32)
    a1 = r.astype(BF)
    a2 = (r - a1.astype(F32)).astype(BF)
    return a0, a1, a2


def _dot_exact_rhs(a, b_bf, dot=_dot):
    a0, a1, a2 = _split3(a)
    return dot(a0, b_bf) + dot(a1, b_bf) + dot(a2, b_bf)


def _dot_hi(a, b, dot=_dot):
    a0, a1, _ = _split3(a)
    b0, b1, _ = _split3(b)
    return dot(a0, b0) + (dot(a0, b1) + dot(a1, b0))


def _rms(x, g):
    ms = jnp.mean(x * x, axis=-1, keepdims=True)
    return x * lax.rsqrt(ms + EPS) * g


def _silu(x):
    return x * jax.nn.sigmoid(x)


def _softplus(x):
    return jnp.maximum(x, 0.0) + jnp.log1p(jnp.exp(-jnp.abs(x)))


def _norm_mm_body(x_ref, g_ref, w_ref, *o_refs):
    h = _rms(x_ref[...], g_ref[...]).astype(BF)
    n = o_refs[0].shape[-1]
    for j, o_ref in enumerate(o_refs):
        o_ref[...] = _dot(h, w_ref[:, j * n:(j + 1) * n])


def norm_mm(x, g, w, n_out, tm, name):
    t, k = x.shape
    n = w.shape[1] // n_out
    return pl.pallas_call(
        _norm_mm_body,
        grid=(t // tm,),
        in_specs=[pl.BlockSpec((tm, k), lambda i: (i, 0)), _const_spec((1, k)), _wspec(w)],
        out_specs=[pl.BlockSpec((tm, n), lambda i: (i, 0))] * n_out,
        out_shape=[SDS((t, n), F32)] * n_out,
        compiler_params=_params(1),
        name=name,
    )(x, g.reshape(1, k), _warr(w))


def _mm_res_body(a_ref, w_ref, r_ref, o_ref):
    o_ref[...] = r_ref[...] + _dot(a_ref[...].astype(BF), w_ref[...])


def mm_res(a, w, res, tm, name):
    t, k = a.shape
    n = w.shape[1]
    return pl.pallas_call(
        _mm_res_body,
        grid=(t // tm,),
        in_specs=[pl.BlockSpec((tm, k), lambda i: (i, 0)), _wspec(w),
                  pl.BlockSpec((tm, n), lambda i: (i, 0))],
        out_specs=pl.BlockSpec((tm, n), lambda i: (i, 0)),
        out_shape=SDS((t, n), F32),
        compiler_params=_params(1),
        name=name,
    )(a, _warr(w), res)


def _memkv_body(x_ref, g_ref, w_ref, k32_ref, v32_ref, kbf_ref, vbf_ref):
    h = _rms(x_ref[...], g_ref[0]).astype(BF)
    d = kbf_ref.shape[-1]
    nh, dh = k32_ref.shape[2:]
    for j, (split_ref, bf_ref) in enumerate(((k32_ref, kbf_ref), (v32_ref, vbf_ref))):
        r = _dot(h, w_ref[0, :, j * d:(j + 1) * d])
        bf_ref[0] = r.astype(BF)
        for hh in range(nh):
            split_ref[0, :, hh, :] = r[:, hh * dh:(hh + 1) * dh]


def memory_kv_all(mem2d, norm_mem, w_kv, nh, tm):
    depth, d, _ = w_kv.shape
    t = mem2d.shape[0]
    flat = pl.BlockSpec((1, tm, d), lambda l, i: (l, i, 0))
    split = pl.BlockSpec((1, tm, nh, d // nh), lambda l, i: (l, i, 0, 0))
    return pl.pallas_call(
        _memkv_body,
        grid=(depth, t // tm),
        in_specs=[pl.BlockSpec((tm, d), lambda l, i: (i, 0)),
                  pl.BlockSpec((1, 1, d), lambda l, i: (l, 0, 0)),
                  pl.BlockSpec((1, d, 2 * d), lambda l, i: (l, 0, 0))],
        out_specs=[split, split, flat, flat],
        out_shape=[SDS((depth, t, nh, d // nh), F32)] * 2 + [SDS((depth, t, d), BF)] * 2,
        compiler_params=_params(2),
        name="memory_kv",
    )(mem2d, norm_mem.reshape(depth, 1, d), w_kv)


def _ffn_body(xp_ref, xs_ref, as_ref, ws_ref, g_ref, wu_ref, wd_ref, gf_ref, op_ref, os_ref, *, ck, final, ntp):
    def tile(x, o_ref):
        h = _rms(x, g_ref[...]).astype(BF)
        dff = wd_ref.shape[0]
        acc = x
        for c0 in range(0, dff, ck):
            gate = _dot(h, wu_ref[:, c0:c0 + ck])
            up = _dot(h, wu_ref[:, dff + c0:dff + c0 + ck])
            a = (_silu(gate) * up).astype(BF)
            acc = acc + _dot(a, wd_ref[c0:c0 + ck, :])
        if final:
            acc = _rms(acc, gf_ref[...])
        o_ref[...] = acc

    i = pl.program_id(0)
    pl.when(i < ntp)(lambda: tile(xp_ref[...], op_ref))
    pl.when(i == ntp)(lambda: tile(xs_ref[...] + _dot(as_ref[...].astype(BF), ws_ref[...]), os_ref))


def ffn(xp, xs, a_s, w_s, g, w_up, w_down, g_final, final, tm, name):
    t, d = xp.shape
    ntp = t // tm
    body = functools.partial(_ffn_body, ck=2 * LANES, final=final, ntp=ntp)
    p_spec = pl.BlockSpec((tm, d), lambda i: (jnp.minimum(i, ntp - 1), 0))
    s_spec = pl.BlockSpec(xs.shape, lambda i: (0, 0))
    return pl.pallas_call(
        body,
        grid=(ntp + 1,),
        in_specs=[p_spec, s_spec, pl.BlockSpec(a_s.shape, lambda i: (0, 0)), _wspec(w_s),
                  _const_spec((1, d)), _wspec(w_up), _wspec(w_down), _const_spec((1, d))],
        out_specs=[p_spec, s_spec],
        out_shape=[SDS((t, d), F32), SDS(xs.shape, F32)],
        compiler_params=_params(1),
        name=name,
    )(xp, xs, a_s, _warr(w_s), g.reshape(1, d), _warr(w_up), _warr(w_down), g_final.reshape(1, d))


def _xattn_p_part(x_ref, g_ref, wq_ref, k_ref, v_ref, wo_ref, o_ref, *, nh, pre_refs=None):
    x = x_ref[...]
    if pre_refs is not None:
        x = x + _dot(pre_refs[0][...].astype(BF), pre_refs[1][...])
    d = x.shape[-1]
    dh = d // nh
    h = _rms(x, g_ref[...]).astype(BF)
    q = (_dot(h, wq_ref[...]) * dh ** -0.5).astype(BF)
    outs = []
    for hh in range(nh):
        cs = slice(hh * dh, (hh + 1) * dh)
        s = _dot_nt(q[:, cs], k_ref[0, :, cs])
        p = jnp.exp(s - jnp.max(s, axis=-1, keepdims=True))
        l = jnp.sum(p, axis=-1, keepdims=True)
        outs.append((_dot(p.astype(BF), v_ref[0, :, cs]) / l).astype(BF))
    o_ref[...] = x + _dot(jnp.concatenate(outs, axis=-1), wo_ref[...])


def _xattn_s_part(qt_ref, k_ref, v_ref, o_ref, *, nb, nh):
    m_len, _, dh = k_ref.shape[2:]
    rows = m_len * nh
    ncol = qt_ref.shape[1]
    rh = lax.broadcasted_iota(jnp.int32, (rows, ncol), 0) % nh
    ch = lax.broadcasted_iota(jnp.int32, (rows, ncol), 1) % nh
    head_bias = jnp.where(rh == ch, 0.0, NEG_INF)
    s = [_dot_nt(k_ref[0, j].reshape(rows, dh).astype(BF), (qt_ref[j] * dh ** -0.5).astype(BF)) + head_bias
         for j in range(nb)]
    p = [jnp.exp(a - jnp.max(a, axis=0, keepdims=True)) for a in s]
    p = [(a / jnp.sum(a, axis=0, keepdims=True)).astype(BF) for a in p]
    for j in range(nb):
        o_ref[j] = _dot_tn(p[j], v_ref[0, j].reshape(rows, dh).astype(BF))


def _xattn_both_body(*refs, nb, has_pre):
    pre_refs = refs[:2] if has_pre else None
    x_ref, g_ref, wq_ref, k_ref, v_ref, wo_ref, qt_ref, ck_ref, cv_ref, o_ref, os_ref = refs[2 if has_pre else 0:]
    _xattn_s_part(qt_ref, ck_ref, cv_ref, os_ref, nb=nb, nh=ck_ref.shape[3])
    _xattn_p_part(x_ref, g_ref, wq_ref, k_ref, v_ref, wo_ref, o_ref, nh=XA_HEADS, pre_refs=pre_refs)


def xattn_both(x, g, w_q, k_bf, v_bf, layer, w_o, qt, cache_k, cache_v, bsz, tm, name, pre=None):
    t, d = x.shape
    nt = t // bsz // tm
    m_len = k_bf.shape[1] // bsz
    bs, ncol, dh = qt.shape
    nh_s = cache_k.shape[3]
    steps = bsz * nt
    assert bs % steps == 0, (bs, steps)
    nb = bs // steps
    kv = lambda a: a.reshape(a.shape[0] * bsz, m_len, d)
    row = lambda b, i: (b * nt + i, 0)
    cache_spec = pl.BlockSpec((1, nb, m_len, nh_s, dh), lambda b, i: (layer, b * nt + i, 0, 0, 0))
    pre_ins, pre_specs = [], []
    if pre is not None:
        pre_ins = [pre[0], _warr(pre[1])]
        pre_specs = [pl.BlockSpec((tm, pre[0].shape[1]), row), _wspec(pre[1])]
    return pl.pallas_call(
        functools.partial(_xattn_both_body, nb=nb, has_pre=pre is not None),
        grid=(bsz, nt),
        in_specs=pre_specs + [
            pl.BlockSpec((tm, d), row), _const_spec((1, d)), _wspec(w_q),
            pl.BlockSpec((1, m_len, d), lambda b, i: (layer * bsz + b, 0, 0)),
            pl.BlockSpec((1, m_len, d), lambda b, i: (layer * bsz + b, 0, 0)),
            _wspec(w_o),
            pl.BlockSpec((nb, ncol, dh), lambda b, i: (b * nt + i, 0, 0)), cache_spec, cache_spec],
        out_specs=[pl.BlockSpec((tm, d), row), pl.BlockSpec((nb, ncol, dh), lambda b, i: (b * nt + i, 0, 0))],
        out_shape=[SDS((t, d), F32), SDS((bs, ncol, dh), F32)],
        compiler_params=_params(2),
        name=name,
    )(*pre_ins, x, g.reshape(1, d), _warr(w_q), kv(k_bf), kv(v_bf), _warr(w_o), qt, cache_k, cache_v)


def _mixa_p_body(x_ref, g_ref, win_ref, wc_ref, wout_ref, o_ref, st_ref, zbuf, *, nt):
    i = pl.program_id(1)
    tm, d = x_ref.shape
    width = wc_ref.shape[0]

    @pl.when(i == 0)
    def _():
        zbuf[0:SUBLANES, :] = jnp.zeros((SUBLANES, d), F32)

    x = x_ref[...]
    h = _rms(x, g_ref[...]).astype(BF)
    gate_b = _dot(h, win_ref[:, 0:d])
    zn = _dot(h, win_ref[:, d:2 * d]) * _dot(h, win_ref[:, 2 * d:3 * d])
    zbuf[SUBLANES:SUBLANES + tm, :] = zn
    conv = wc_ref[width - 1:width, :] * zn
    for j in range(width - 1):
        off = SUBLANES - (width - 1) + j
        conv = conv + wc_ref[j:j + 1, :] * zbuf[off:off + tm, :]
    o_ref[...] = x + _dot((gate_b * conv).astype(BF), wout_ref[...])
    zbuf[0:SUBLANES, :] = zbuf[tm:tm + SUBLANES, :]

    @pl.when(i == nt - 1)
    def _():
        st_ref[0] = zbuf[SUBLANES - (width - 1):SUBLANES, :]


def mixer_a_prompt(x, g, w_in, w_conv, w_out, bsz, tm, name):
    t, d = x.shape
    nt = t // bsz // tm
    width = w_conv.shape[0]
    row = lambda b, i: (b * nt + i, 0)
    return pl.pallas_call(
        functools.partial(_mixa_p_body, nt=nt),
        grid=(bsz, nt),
        in_specs=[pl.BlockSpec((tm, d), row), _const_spec((1, d)), _wspec(w_in),
                  _const_spec(w_conv.shape), _wspec(w_out)],
        out_specs=[pl.BlockSpec((tm, d), row), pl.BlockSpec((1, width - 1, d), lambda b, i: (b, 0, 0))],
        out_shape=[SDS((t, d), F32), SDS((bsz, width - 1, d), F32)],
        scratch_shapes=[pltpu.VMEM((tm + SUBLANES, d), F32)],
        compiler_params=_params(2),
        name=name,
    )(x, g.reshape(1, d), _warr(w_in), w_conv, _warr(w_out))


def _posmajor_conv(state, z, wc_ref, bsz):
    width = wc_ref.shape[0]
    n = z.shape[0]
    zpad = jnp.concatenate([state, z], axis=0)
    conv = wc_ref[0:1, :] * zpad[0:n]
    for j in range(1, width):
        conv = conv + wc_ref[j:j + 1, :] * zpad[j * bsz:j * bsz + n]
    return conv, zpad[n:]


def _mixa_s_body(x_ref, st_ref, g_ref, win_ref, wc_ref, wout_ref, o_ref, nst_ref, *, bsz):
    x = x_ref[...]
    d = x.shape[-1]
    h = _rms(x, g_ref[...]).astype(BF)
    gate_b = _dot(h, win_ref[:, 0:d])
    zn = _dot(h, win_ref[:, d:2 * d]) * _dot(h, win_ref[:, 2 * d:3 * d])
    conv, new_state = _posmajor_conv(st_ref[...], zn, wc_ref, bsz)
    o_ref[...] = x + _dot((gate_b * conv).astype(BF), wout_ref[...])
    nst_ref[...] = new_state


def mixer_a_sample(x, state_pm, g, w_in, w_conv, w_out, bsz, name):
    t, d = x.shape
    return pl.pallas_call(
        functools.partial(_mixa_s_body, bsz=bsz),
        grid=(1,),
        in_specs=[_const_spec(x.shape), _const_spec(state_pm.shape), _const_spec((1, d)),
                  _wspec(w_in), _const_spec(w_conv.shape), _wspec(w_out)],
        out_specs=[_whole_spec(x.shape), _whole_spec(state_pm.shape)],
        out_shape=[SDS(x.shape, F32), SDS(state_pm.shape, F32)],
        compiler_params=_params(1),
        name=name,
    )(x, state_pm, g.reshape(1, d), _warr(w_in), w_conv, _warr(w_out))


GDN_COL_CHUNK = 2 * LANES


def _gdn_qkv_chunk(conv, c0, q_ref, k_ref, v_ref):
    nh, dk = GDN_HEADS, GDN_DK
    hd = nh * dk
    act = _silu(conv)
    section, col = c0 // hd, c0 % hd
    if section == 2:
        v_ref[:, col:col + GDN_COL_CHUNK] = act
        return
    ref = q_ref if section == 0 else k_ref
    for j in range(GDN_COL_CHUNK // dk):
        t = act[:, j * dk:(j + 1) * dk]
        t = t * lax.rsqrt(jnp.sum(t * t, axis=-1, keepdims=True) + EPS)
        ref[:, col + j * dk:col + (j + 1) * dk] = t * dk ** -0.5 if section == 0 else t


def _gdn_gates(ba, alog_ref, dtb_ref, bg_ref):
    nh = GDN_HEADS
    lane = lax.broadcasted_iota(jnp.int32, ba.shape, 1)
    beta = jax.nn.sigmoid(ba)
    g = -jnp.exp(alog_ref[...]) * _softplus(ba + dtb_ref[...])
    bg_ref[...] = jnp.where(lane < nh, beta, jnp.where(lane < 2 * nh, g, 0.0))


def _gdn_proj_p_body(x_ref, g_ref, w_ref, wba_ref, wc_ref, alog_ref, dtb_ref,
                     q_ref, k_ref, v_ref, z_ref, bg_ref, st_ref, cbuf, *, nt):
    i = pl.program_id(1)
    tm = x_ref.shape[0]
    width, cc = wc_ref.shape
    ck = GDN_COL_CHUNK

    @pl.when(i == 0)
    def _():
        cbuf[0:SUBLANES, :] = jnp.zeros((SUBLANES, cc), F32)

    h = _rms(x_ref[...], g_ref[...]).astype(BF)
    for c0 in range(0, cc, ck):
        cols = slice(c0, c0 + ck)
        pre = _dot(h, w_ref[:, cols])
        cbuf[SUBLANES:SUBLANES + tm, cols] = pre
        conv = wc_ref[width - 1:width, cols] * pre
        for j in range(width - 1):
            off = SUBLANES - (width - 1) + j
            conv = conv + wc_ref[j:j + 1, cols] * cbuf[off:off + tm, cols]
        _gdn_qkv_chunk(conv, c0, q_ref, k_ref, v_ref)
    for c0 in range(0, z_ref.shape[1], ck):
        z_ref[:, c0:c0 + ck] = _dot(h, w_ref[:, cc + c0:cc + c0 + ck])
    _gdn_gates(_dot(h, wba_ref[...]), alog_ref, dtb_ref, bg_ref)
    cbuf[0:SUBLANES, :] = cbuf[tm:tm + SUBLANES, :]

    @pl.when(i == nt - 1)
    def _():
        st_ref[0] = cbuf[SUBLANES - (width - 1):SUBLANES, :]


def gdn_proj_prompt(x, g, w_qkvz, w_ba, w_conv, alog, dtb, bsz, tm, name):
    t, d = x.shape
    nt = t // bsz // tm
    width, cc = w_conv.shape
    hd = GDN_HEADS * GDN_DK
    row = lambda b, i: (b * nt + i, 0)
    tile = lambda n: pl.BlockSpec((tm, n), row)
    return pl.pallas_call(
        functools.partial(_gdn_proj_p_body, nt=nt),
        grid=(bsz, nt),
        in_specs=[tile(d), _const_spec((1, d)), _wspec(w_qkvz), _wspec(w_ba),
                  _const_spec(w_conv.shape), _const_spec(alog.shape), _const_spec(dtb.shape)],
        out_specs=[tile(hd), tile(hd), tile(hd), tile(hd), tile(LANES),
                   pl.BlockSpec((1, width - 1, cc), lambda b, i: (b, 0, 0))],
        out_shape=[SDS((t, hd), F32)] * 4 + [SDS((t, LANES), F32), SDS((bsz, width - 1, cc), F32)],
        scratch_shapes=[pltpu.VMEM((tm + SUBLANES, cc), F32)],
        compiler_params=_params(2),
        name=name,
    )(x, g.reshape(1, d), _warr(w_qkvz), _warr(w_ba), w_conv, alog, dtb)


def _gdn_proj_s_body(x_ref, st_ref, g_ref, w_ref, wba_ref, wc_ref, alog_ref, dtb_ref,
                     q_ref, k_ref, v_ref, z_ref, bg_ref, nst_ref, *, bsz):
    cc = wc_ref.shape[1]
    h = _rms(x_ref[...], g_ref[...]).astype(BF)
    qkv = _dot(h, w_ref[:, :cc])
    z_ref[...] = _dot(h, w_ref[:, cc:cc + z_ref.shape[1]])
    ba = _dot(h, wba_ref[...])
    conv, new_state = _posmajor_conv(st_ref[...], qkv, wc_ref, bsz)
    for c0 in range(0, cc, GDN_COL_CHUNK):
        _gdn_qkv_chunk(conv[:, c0:c0 + GDN_COL_CHUNK], c0, q_ref, k_ref, v_ref)
    _gdn_gates(ba, alog_ref, dtb_ref, bg_ref)
    nst_ref[...] = new_state


def gdn_proj_sample(x, state_pm, g, w_qkvz, w_ba, w_conv, alog, dtb, bsz, name):
    t, d = x.shape
    cc = w_conv.shape[1]
    hd = GDN_HEADS * GDN_DK
    ins = (x, state_pm, g.reshape(1, d), w_qkvz, w_ba, w_conv, alog, dtb)
    outs = [SDS((t, hd), F32)] * 4 + [SDS((t, LANES), F32), SDS(state_pm.shape, F32)]
    return pl.pallas_call(
        functools.partial(_gdn_proj_s_body, bsz=bsz),
        grid=(1,),
        in_specs=[_wspec(a) for a in ins],
        out_specs=[_whole_spec(o.shape) for o in outs],
        out_shape=outs,
        compiler_params=_params(1),
        name=name,
    )(*[_warr(a) for a in ins])


def _pad_transpose(a):
    c = a.shape[0]
    if c < LANES:
        a = jnp.concatenate([a, jnp.zeros((LANES - c, a.shape[1]), a.dtype)], axis=0)
    return a.T


def _gdn_scan_body(*refs, c, per_chunk_state, fuse_out):
    if fuse_out:
        x_ref, refs = refs[0], refs[1:]
    (q_ref, k_ref, v_ref, z_ref, bg_ref, tri_ref, s0_ref, gn_ref) = refs[:8]
    refs = refs[8:]
    if fuse_out:
        wout_ref, refs = refs[0], refs[1:]
    o_ref, s_ref, qg_sc, kdec_sc, rk_sc, rv_sc, gam_sc, beta_sc, egl_sc, o_sc = refs

    nh, dk = GDN_HEADS, GDN_DK
    rows, hd = q_ref.shape
    n_chunks = rows // c
    n_double = int(math.log2(c)) - 1

    if not per_chunk_state:
        @pl.when(pl.program_id(1) == 0)
        def _():
            s_ref[...] = s0_ref[...]

    bg = bg_ref[...]
    gam = _dot_exact_rhs(bg, tri_ref[...], dot=lambda a, b: _dot(b, a))
    beta_b = jnp.concatenate([jnp.broadcast_to(bg[:, hh:hh + 1], (rows, dk)) for hh in range(nh)], axis=1)
    gam_b = jnp.concatenate([jnp.broadcast_to(gam[:, nh + hh:nh + hh + 1], (rows, dk)) for hh in range(nh)], axis=1)
    gl_b = jnp.broadcast_to(gam_b.reshape(n_chunks, c, hd)[:, c - 1:c, :], (n_chunks, c, hd)).reshape(rows, hd)
    eg = jnp.exp(gam_b)
    k = k_ref[...]
    qg_sc[...] = q_ref[...] * eg
    rk_sc[...] = beta_b * eg * k
    rv_sc[...] = beta_b * v_ref[...]
    kdec_sc[...] = k * jnp.exp(gl_b - gam_b)
    gam_sc[...] = gam_b
    beta_sc[...] = beta_b
    egl_sc[...] = jnp.exp(gl_b)

    ii = lax.broadcasted_iota(jnp.int32, (c, c), 0)
    jj = lax.broadcasted_iota(jnp.int32, (c, c), 1)
    incl = ii >= jj
    strict = ii > jj

    chains = [(slice(ci * c, (ci + 1) * c), slice(hh * dk, (hh + 1) * dk))
              for ci in range(n_chunks) for hh in range(nh)]
    kq, decay, pm = [], [], []
    for rs, hc in chains:
        kc = k_ref[rs, hc]
        kq.append(_dot_nt(jnp.concatenate([kc, q_ref[rs, hc]], axis=0).astype(BF), kc.astype(BF)))
    for (rs, hc), kq_i in zip(chains, kq):
        gcol = gam_sc[rs, hc]
        diff = gcol[:, :c] - _pad_transpose(gcol)[:c, :c]
        dec = jnp.where(incl, jnp.exp(jnp.where(incl, diff, 0.0)), 0.0)
        decay.append(dec)
        pm.append(jnp.where(strict, -(beta_sc[rs, hc][:, :c] * kq_i[:c] * dec), 0.0))
    mk = pm
    for _ in range(n_double):
        mk = [_dot(a.astype(BF), a.astype(BF)) for a in mk]
        pm = [p + a + _dot(p.astype(BF), a.astype(BF)) for p, a in zip(pm, mk)]
    sol, lhs2 = [], []
    for (rs, hc), p, kq_i, dec in zip(chains, pm, kq, decay):
        rhs = jnp.concatenate([rv_sc[rs, hc], rk_sc[rs, hc]], axis=1)
        sol.append(rhs + _dot(p.astype(BF), rhs.astype(BF)))
        kdt = _pad_transpose(kdec_sc[rs, hc])[:, :c]
        lhs2.append(jnp.concatenate([kq_i[c:] * dec, kdt], axis=0).astype(BF))

    def advance(idxs, states):
        t1 = []
        for idx, s in zip(idxs, states):
            rs, hc = chains[idx]
            t1.append(_dot(jnp.concatenate([sol[idx][:, dk:], qg_sc[rs, hc]], axis=0).astype(BF), s.astype(BF)))
        t2 = [_dot(lhs2[idx], (sol[idx][:, :dk] - t[:c]).astype(BF)) for idx, t in zip(idxs, t1)]
        new_states = []
        for idx, s, ta, tb in zip(idxs, states, t1, t2):
            rs, hc = chains[idx]
            o_sc[rs, hc] = ta[c:] + tb[:c]
            egl = jnp.broadcast_to(egl_sc[rs.start:rs.start + 1, hc], (dk, dk))
            new_states.append(egl * s + tb[c:])
        return new_states

    if per_chunk_state:
        idxs = list(range(len(chains)))
        new_states = advance(idxs, [s0_ref[idx // nh, idx % nh] for idx in idxs])
        for idx, s_new in zip(idxs, new_states):
            s_ref[idx // nh, idx % nh] = s_new
    else:
        state = [s_ref[0, hh] for hh in range(nh)]
        for ci in range(n_chunks):
            state = advance([ci * nh + hh for hh in range(nh)], state)
        for hh in range(nh):
            s_ref[0, hh] = state[hh]

    outs = []
    for hh in range(nh):
        hc = slice(hh * dk, (hh + 1) * dk)
        outs.append(_rms(o_sc[:, hc], gn_ref[...]) * _silu(z_ref[:, hc]))
    yv = jnp.concatenate(outs, axis=-1)
    if fuse_out:
        o_ref[...] = x_ref[...] + _dot(yv.astype(BF), wout_ref[...])
    else:
        o_ref[...] = yv


def _chunk_cumsum_matrix(rows, c):
    r = np.arange(rows)
    return jnp.asarray((r[:, None] >= r[None, :]) & (r[:, None] // c == r[None, :] // c), BF)


def gdn_scan(x, q, k, v, z, bg, s0, g_norm, w_out, *, c, rows, n_seq, per_chunk_state, name):
    t, hd = q.shape
    nt = t // n_seq // rows
    fuse_out = w_out is not None
    tri = _chunk_cumsum_matrix(rows, c)
    ns = rows // c if per_chunk_state else 1
    row = lambda b, i: (b * nt + i, 0)
    tile = lambda n: pl.BlockSpec((rows, n), row)
    st_spec = pl.BlockSpec((ns,) + s0.shape[1:], lambda b, i: (b, 0, 0, 0))
    ins, specs = [], []
    if fuse_out:
        ins.append(x)
        specs.append(tile(x.shape[1]))
    ins += [q, k, v, z, bg, tri, s0, g_norm.reshape(1, GDN_DK)]
    specs += [tile(hd)] * 4 + [tile(LANES), _const_spec(tri.shape), st_spec, _const_spec((1, GDN_DK))]
    d_out = hd
    if fuse_out:
        ins.append(_warr(w_out))
        specs.append(_wspec(w_out))
        d_out = w_out.shape[1]
    return pl.pallas_call(
        functools.partial(_gdn_scan_body, c=c, per_chunk_state=per_chunk_state, fuse_out=fuse_out),
        grid=(n_seq, nt),
        in_specs=specs,
        out_specs=[tile(d_out), st_spec],
        out_shape=[SDS((t, d_out), F32), SDS(s0.shape, F32)],
        scratch_shapes=[pltpu.VMEM((rows, hd), F32)] * 8,
        compiler_params=_params(2),
        name=name,
    )(*ins)


def _topk_mask(sb, valid, axis):
    n = sb.shape[axis]
    idx = lax.broadcasted_iota(jnp.int32, sb.shape, axis)
    cnt = jnp.zeros(sb.shape, F32)
    for m in range(n):
        sm = lax.slice_in_dim(sb, m, m + 1, axis=axis)
        ahead = (sm > sb) | ((sm == sb) & (m < idx))
        cnt = cnt + jnp.where(ahead, 1.0, 0.0)
    return jnp.where((cnt < MOBA_TOPK) & valid, 1.0, 0.0)


def _moba_p_body(q_ref, k_ref, v_ref, oh_ref, o_ref, *, part, parts):
    length, dh = k_ref.shape
    blk = MOBA_BLOCK
    nb = length // blk
    scale = dh ** -0.5
    k = k_ref[...]
    kaug = jnp.concatenate([k.astype(BF), oh_ref[...]], axis=1)
    vb = v_ref[...].astype(BF)
    km = jnp.sum(k.reshape(nb, blk, dh), axis=1) * (1.0 / blk)
    brow = lax.broadcasted_iota(jnp.int32, (nb, blk), 0)
    pad_rows = jnp.zeros((LANES - nb, blk), F32)
    slots = []
    for j in range(nb // parts):
        i = parts * j + part
        rows = pl.ds(pl.multiple_of(i * blk, blk), blk)
        q = q_ref[rows, :]
        sbt = jnp.where(brow < i, _dot_hi(km, q, dot=_dot_nt), NEG_INF)
        selt = _topk_mask(sbt, sbt > NEG_INF / 2, axis=0)
        bias_t = jnp.where((selt > 0.5) | (brow == i), 0.0, NEG_INF)
        bias = jnp.concatenate([bias_t, pad_rows], axis=0).T
        slots.append((i, rows, jnp.concatenate([(q * scale).astype(BF), bias.astype(BF)], axis=1)))
    for j, (i, rows, qaug) in enumerate(slots):
        n_keys = parts * (j + 1) * blk
        s = _dot_nt(qaug, kaug[:n_keys])
        key_pos = lax.broadcasted_iota(jnp.int32, (blk, n_keys), 1)
        query_pos = lax.broadcasted_iota(jnp.int32, (blk, n_keys), 0) + i * blk
        s = jnp.where(key_pos <= query_pos, s, NEG_INF)
        pe = jnp.exp(s - jnp.max(s, axis=-1, keepdims=True))
        l = jnp.sum(pe, axis=-1, keepdims=True)
        o_ref[rows, :] = _dot(pe.astype(BF), vb[:n_keys]) / l


def _lane_group_reduce(row, op, ncol):
    a = jnp.broadcast_to(row, (SUBLANES, LANES))
    shift = ncol
    while shift < LANES:
        a = op(a, pltpu.roll(a, shift, 1))
        shift *= 2
    return a[:1]


def _moba_s_body(pt_ref, qt_ref, hb_ref, kn_ref, vn_ref, *refs, n_pages, ncol):
    del pt_ref
    kp = refs[:n_pages]
    vp = refs[n_pages:2 * n_pages]
    o_ref, s_sc = refs[2 * n_pages:]
    page, nh, dh = kp[0].shape[1:]
    bp = MOBA_BLOCK // page
    nb = n_pages // bp
    prow = page * nh
    pack = LANES // ncol
    q_rep = jnp.concatenate([qt_ref[0]] * pack, axis=0)
    lane = lax.broadcasted_iota(jnp.int32, (1, LANES), 1)
    group = lane // ncol
    qs = q_rep * dh ** -0.5
    rg = lax.broadcasted_iota(jnp.int32, (LANES, pack * dh), 0) // ncol
    cg = lax.broadcasted_iota(jnp.int32, (LANES, pack * dh), 1) // dh
    rhs = jnp.where(rg == cg, jnp.concatenate([qs] * pack, axis=1), 0.0).astype(BF)

    km = []
    for n in range(nb):
        tot = kp[n * bp][0].sum(axis=0)
        for j in range(1, bp):
            tot = tot + kp[n * bp + j][0].sum(axis=0)
        km.append(tot * (1.0 / MOBA_BLOCK))
    km2 = jnp.concatenate(km, axis=0)
    sbm = _dot_hi(km2, q_rep, dot=_dot_nt)
    rh = lax.broadcasted_iota(jnp.int32, sbm.shape, 0) % nh
    ch = lax.broadcasted_iota(jnp.int32, sbm.shape, 1) % ncol % nh
    sb = jnp.sum(jnp.where(rh == ch, sbm, 0.0).reshape(nb, nh, LANES), axis=1)
    sel = _topk_mask(sb, jnp.full(sb.shape, True), axis=0)
    sel_bias = jnp.where(sel > 0.5, 0.0, NEG_INF)

    n_own = kn_ref.shape[1]
    ro = lax.broadcasted_iota(jnp.int32, (n_own, LANES), 0)
    co = lax.broadcasted_iota(jnp.int32, (n_own, LANES), 1)
    own_ok = (ro % nh == co % nh) & (ro // nh <= co // nh) & (co < ncol)
    s_own = jnp.where(own_ok, _dot_nt(kn_ref[0].astype(BF), qs.astype(BF)), NEG_INF)
    mx = jnp.max(s_own, axis=0, keepdims=True)

    n_groups = n_pages // pack
    for j in range(n_groups):
        pages = range(j * pack, (j + 1) * pack)
        k4 = jnp.concatenate([kp[p][0].reshape(prow, dh).astype(BF) for p in pages], axis=1)
        bias = sel_bias[(j * pack) // bp:(j * pack) // bp + 1, :]
        for g in range(1, pack):
            blk = (j * pack + g) // bp
            bias = jnp.where(group >= g, sel_bias[blk:blk + 1, :], bias)
        s = _dot_nt(k4, rhs) + (hb_ref[...] + bias)
        s_sc[j * prow:(j + 1) * prow, :] = s
        mx = jnp.maximum(mx, jnp.max(s, axis=0, keepdims=True))
    mx = _lane_group_reduce(mx, jnp.maximum, ncol)

    pe = jnp.exp(s_own - mx)
    l = jnp.sum(pe, axis=0, keepdims=True)
    acc = _dot_tn(pe.astype(BF), vn_ref[0].astype(BF))[:ncol]
    for j in range(n_groups):
        pages = range(j * pack, (j + 1) * pack)
        pe = jnp.exp(s_sc[j * prow:(j + 1) * prow, :] - mx)
        l = l + jnp.sum(pe, axis=0, keepdims=True)
        v4 = jnp.concatenate([vp[p][0].reshape(prow, dh).astype(BF) for p in pages], axis=1)
        r = _dot_tn(pe.astype(BF), v4)
        for g in range(pack):
            acc = acc + r[g * ncol:(g + 1) * ncol, g * dh:(g + 1) * dh]
    l = _lane_group_reduce(l, jnp.add, ncol)
    l_col = jnp.broadcast_to(l, (LANES, LANES)).T[:ncol, :1]
    o_ref[0] = acc / l_col


def _moba_both_body(pt_ref, q_ref, k_ref, v_ref, oh_ref, qt_ref, hb_ref, kn_ref, vn_ref, *refs,
                    n_pages, ncol, parts):
    pages = refs[:2 * n_pages]
    o_ref, os_ref, s_sc = refs[2 * n_pages:]
    _moba_s_body(pt_ref, qt_ref, hb_ref, kn_ref, vn_ref, *pages, os_ref, s_sc, n_pages=n_pages, ncol=ncol)
    _moba_p_body(q_ref, k_ref, v_ref, oh_ref, o_ref, part=pl.program_id(2), parts=parts)


def moba_both(q, k, v, bsz, qt, kn, vn, pool_k, pool_v, page_table, name):
    t, w = q.shape
    dh = MOBA_DH
    nh = w // dh
    length = t // bsz
    nb = length // MOBA_BLOCK
    bs, ncol, _ = qt.shape
    assert bs % (bsz * nh) == 0, (bs, bsz, nh)
    parts = bs // (bsz * nh)
    assert parts >= 1 and nb % parts == 0, (parts, nb)
    n_pages = page_table.shape[1]
    page, nh_s = pool_k.shape[1:3]
    n_own = kn.shape[1]
    pack = LANES // ncol
    onehot = jnp.asarray(np.arange(length)[:, None] // MOBA_BLOCK == np.arange(dh)[None, :], BF)
    r = np.arange(page * nh_s)[:, None]
    c = np.arange(LANES)[None, :]
    head_bias = jnp.asarray(np.where(r % nh_s == c % ncol % nh_s, 0.0, NEG_INF), F32)

    seq = lambda b, h, p, pt: (b * nh + h) * parts + p
    p_spec = pl.BlockSpec((length, dh), lambda b, h, p, pt: (b, h))

    def page_spec(pg):
        return pl.BlockSpec((1, page, nh_s, dh), lambda b, h, p, pt: (pt[seq(b, h, p, pt), pg], 0, 0, 0))

    s_spec = lambda rows: pl.BlockSpec((1, rows, dh), lambda b, h, p, pt: (seq(b, h, p, pt), 0, 0))
    grid_spec = pltpu.PrefetchScalarGridSpec(
        num_scalar_prefetch=1,
        grid=(bsz, nh, parts),
        in_specs=[p_spec, p_spec, p_spec,
                  pl.BlockSpec(onehot.shape, lambda b, h, p, pt: (0, 0)),
                  s_spec(ncol),
                  pl.BlockSpec(head_bias.shape, lambda b, h, p, pt: (0, 0)),
                  s_spec(n_own), s_spec(n_own)]
                 + [page_spec(pg) for pg in range(n_pages)] * 2,
        out_specs=[p_spec, s_spec(ncol)],
        scratch_shapes=[pltpu.VMEM((n_pages // pack * page * nh_s, LANES), F32)],
    )
    return pl.pallas_call(
        functools.partial(_moba_both_body, n_pages=n_pages, ncol=ncol, parts=parts),
        grid_spec=grid_spec,
        out_shape=[SDS((t, w), F32), SDS((bs, ncol, dh), F32)],
        compiler_params=_params(3),
        name=name,
    )(page_table, q, k, v, onehot, qt, head_bias, kn, vn, *([pool_k] * n_pages), *([pool_v] * n_pages))


def _to_pm(a):
    b, l, c = a.shape
    return jnp.transpose(a, (1, 0, 2)).reshape(l * b, c)


def _from_pm(a, bsz):
    n, c = a.shape
    return jnp.transpose(a.reshape(n // bsz, bsz, c), (1, 0, 2))


def _pad_lanes(vec, offset):
    out = jnp.zeros((1, LANES), F32)
    return lax.dynamic_update_slice(out, vec.reshape(1, -1).astype(F32), (0, offset))


def kernel(x_prompt, x_sample, state_a_conv, state_b_conv, state_b_rec, cache_c_k, cache_c_v, cache_mem_k, cache_mem_v, page_table, mem_prompt, norm_mix, norm_mem, norm_xattn, norm_ffn, norm_final, a_w_in, a_w_conv, a_w_out, b_w_in, b_w_conv, b_a_log, b_dt_bias, b_norm, b_w_out, c_w_qkv, c_w_out, x_w_q, x_w_kv, x_w_o, f_w_up, f_w_down):
    bp, seq, d = x_prompt.shape
    bs, dseq, _ = x_sample.shape
    depth = norm_mix.shape[0]
    n_mixers = 3
    tm = ROW_TILE
    ts = bs * dseq

    bf = lambda w: w.astype(BF)
    a_w_in, a_w_out, b_w_out, c_w_qkv, c_w_out = map(bf, (a_w_in, a_w_out, b_w_out, c_w_qkv, c_w_out))
    x_w_q, x_w_kv, x_w_o, f_w_up, f_w_down = map(bf, (x_w_q, x_w_kv, x_w_o, f_w_up, f_w_down))
    cc = b_w_conv.shape[-1]
    hd = GDN_HEADS * GDN_DK
    b_w_qkvz = bf(b_w_in)
    b_w_ba = bf(jnp.pad(b_w_in[:, :, cc + hd:], ((0, 0), (0, 0), (0, LANES - 2 * GDN_HEADS))))

    xp = x_prompt.reshape(bp * seq, d)
    xs = _to_pm(x_sample)

    mem_len = mem_prompt.shape[1]
    mk32, mv32, mkbf, mvbf = memory_kv_all(mem_prompt.reshape(bp * mem_len, d), norm_mem, x_w_kv, XA_HEADS, tm)

    a_p, a_s, bc_p, bc_s, br_p, br_s = [], [], [], [], [], []
    ck_p, cv_p, ck_s, cv_s = [], [], [], []
    for i in range(depth):
        kind, j = i % n_mixers, i // n_mixers
        tag = f"l{i}"
        pre = None
        if kind == 0:
            xp, st = mixer_a_prompt(xp, norm_mix[i], _Layer(a_w_in, j), a_w_conv[j], _Layer(a_w_out, j), bp, tm, tag + "_mixa_p")
            a_p.append(st)
            st_pm = _to_pm(state_a_conv[j])
            xs, nst = mixer_a_sample(xs, st_pm, norm_mix[i], _Layer(a_w_in, j), a_w_conv[j], _Layer(a_w_out, j), bs, tag + "_mixa_s")
            a_s.append(_from_pm(nst, bs))
        elif kind == 1:
            alog = _pad_lanes(b_a_log[j], GDN_HEADS)
            dtb = _pad_lanes(b_dt_bias[j], GDN_HEADS)
            q, k, v, z, bg, st = gdn_proj_prompt(xp, norm_mix[i], _Layer(b_w_qkvz, j), _Layer(b_w_ba, j), b_w_conv[j], alog, dtb,
                                                 bp, GDN_ROW_TILE, tag + "_gdnproj_p")
            bc_p.append(st)
            s0 = jnp.zeros((bp, GDN_HEADS, GDN_DK, GDN_DK), F32)
            xp, s_fin = gdn_scan(xp, q, k, v, z, bg, s0, b_norm[j], _Layer(b_w_out, j), c=math.gcd(seq, GDN_CHUNK),
                                 rows=GDN_ROW_TILE, n_seq=bp, per_chunk_state=False, name=tag + "_gdnscan_p")
            br_p.append(s_fin)
            st_pm = _to_pm(state_b_conv[j])
            q, k, v, z, bg, nst = gdn_proj_sample(xs, st_pm, norm_mix[i], _Layer(b_w_qkvz, j), _Layer(b_w_ba, j), b_w_conv[j],
                                                  alog, dtb, bs, tag + "_gdnproj_s")
            bc_s.append(_from_pm(nst, bs))
            cpad = SUBLANES
            padded = lambda a: jnp.pad(_from_pm(a, bs), ((0, 0), (0, cpad - dseq), (0, 0))).reshape(bs * cpad, -1)
            nbatch = GDN_SAMPLE_SEQS_PER_STEP
            y, s_fin = gdn_scan(None, padded(q), padded(k), padded(v), padded(z), padded(bg), state_b_rec[j],
                                b_norm[j], None, c=cpad, rows=nbatch * cpad, n_seq=bs // nbatch,
                                per_chunk_state=True, name=tag + "_gdnscan_s")
            br_s.append(s_fin)
            y = _to_pm(y.reshape(bs, cpad, hd)[:, :dseq])
            xs = mm_res(y, _Layer(b_w_out, j), xs, ts, tag + "_gdnout_s")
        else:
            q, k, v = norm_mm(xp, norm_mix[i], _Layer(c_w_qkv, j), 3, tm, tag + "_mobaqkv_p")
            ck_p.append(k.reshape(bp, seq, MOBA_HEADS, MOBA_DH))
            cv_p.append(v.reshape(bp, seq, MOBA_HEADS, MOBA_DH))
            qs, ks, vs = norm_mm(xs, norm_mix[i], _Layer(c_w_qkv, j), 3, ts, tag + "_mobaqkv_s")
            kn = _from_pm(ks, bs)
            vn = _from_pm(vs, bs)
            ck_s.append(kn.reshape(bs, dseq, MOBA_HEADS, MOBA_DH))
            cv_s.append(vn.reshape(bs, dseq, MOBA_HEADS, MOBA_DH))
            qt = _from_pm(qs, bs).reshape(bs, dseq * MOBA_HEADS, MOBA_DH)
            rows_kv = lambda a: a.reshape(bs, dseq * MOBA_HEADS, MOBA_DH)
            o, os_ = moba_both(q, k, v, bp, qt, rows_kv(kn), rows_kv(vn), cache_c_k[j], cache_c_v[j], page_table,
                               tag + "_moba")
            pre = (o, _Layer(c_w_out, j))
            xs = mm_res(_to_pm(os_.reshape(bs, dseq, MOBA_HEADS * MOBA_DH)), _Layer(c_w_out, j), xs, ts, tag + "_mobaout_s")

        (qs,) = norm_mm(xs, norm_xattn[i], _Layer(x_w_q, i), 1, ts, tag + "_xattnq_s")
        dh = d // XA_HEADS
        qt = _from_pm(qs, bs).reshape(bs, dseq * XA_HEADS, dh)
        xp, o = xattn_both(xp, norm_xattn[i], _Layer(x_w_q, i), mkbf, mvbf, i, _Layer(x_w_o, i),
                           qt, cache_mem_k, cache_mem_v, bp, tm, tag + "_xattn", pre=pre)

        last = i == depth - 1
        xp, xs = ffn(xp, xs, _to_pm(o.reshape(bs, dseq, d)), _Layer(x_w_o, i), norm_ffn[i],
                     _Layer(f_w_up, i), _Layer(f_w_down, i), norm_final, last, tm, tag + "_ffn")

    mem_shape = (depth, bp, mem_len, XA_HEADS, d // XA_HEADS)
    return (xp.reshape(bp, seq, d), _from_pm(xs, bs),
            jnp.stack(a_p), jnp.stack(a_s),
            jnp.stack(bc_p), jnp.stack(bc_s),
            jnp.stack(br_p), jnp.stack(br_s),
            jnp.stack(ck_p), jnp.stack(cv_p), jnp.stack(ck_s), jnp.stack(cv_s),
            mk32.reshape(mem_shape), mv32.reshape(mem_shape))
```

```python
import functools
import math

import numpy as np
import jax
import jax.numpy as jnp
from jax import lax
from jax.experimental import pallas as pl
from jax.experimental.pallas import tpu as pltpu

F32 = jnp.float32
BF = jnp.bfloat16
SDS = jax.ShapeDtypeStruct

EPS = 1e-6
NEG_INF = -1e30
LANES = 128
SUBLANES = 8
VMEM_LIMIT = 56 * 1024 * 1024

ROW_TILE = 512
GDN_ROW_TILE = 256
GDN_SAMPLE_SEQS_PER_STEP = 8

GDN_HEADS = 8
GDN_DK = 128
GDN_CHUNK = 64
MOBA_HEADS = 8
MOBA_DH = 128
MOBA_BLOCK = 256
MOBA_TOPK = 3
PAGE_SIZE = 128
XA_HEADS = 4


def _params(n_axes):
    return pltpu.CompilerParams(dimension_semantics=("arbitrary",) * n_axes,
                                vmem_limit_bytes=VMEM_LIMIT)


def _const_spec(shape):
    nd = len(shape)
    return pl.BlockSpec(shape, lambda *_: (0,) * nd, pipeline_mode=pl.Buffered(1))


class _Layer:
    def __init__(self, arr, j):
        self.arr, self.j = arr, j

    @property
    def shape(self):
        return self.arr.shape[1:]


def _wspec(w):
    if not isinstance(w, _Layer):
        return _const_spec(w.shape)
    nd = len(w.shape)
    j = w.j
    return pl.BlockSpec((None,) + w.shape, lambda *_: (j,) + (0,) * nd, pipeline_mode=pl.Buffered(1))


def _warr(w):
    return w.arr if isinstance(w, _Layer) else w


def _whole_spec(shape):
    nd = len(shape)
    return pl.BlockSpec(shape, lambda *_: (0,) * nd)


def _dot(a, b):
    return jnp.dot(a, b, preferred_element_type=F32)


def _dot_nt(a, b):
    return lax.dot_general(a, b, (((1,), (1,)), ((), ())), preferred_element_type=F32)


def _dot_tn(a, b):
    return lax.dot_general(a, b, (((0,), (0,)), ((), ())), preferred_element_type=F32)


def _split3(a):
    a0 = a.astype(BF)
    r = a - a0.astype(F32)
    a1 = r.astype(BF)
    a2 = (r - a1.astype(F32)).astype(BF)
    return a0, a1, a2


def _dot_exact_rhs(a, b_bf, dot=_dot):
    a0, a1, a2 = _split3(a)
    return dot(a0, b_bf) + dot(a1, b_bf) + dot(a2, b_bf)


def _dot_hi(a, b, dot=_dot):
    a0, a1, _ = _split3(a)
    b0, b1, _ = _split3(b)
    return dot(a0, b0) + (dot(a0, b1) + dot(a1, b0))


def _rms(x, g):
    ms = jnp.mean(x * x, axis=-1, keepdims=True)
    return x * lax.rsqrt(ms + EPS) * g


def _silu(x):
    return x * jax.nn.sigmoid(x)


def _softplus(x):
    return jnp.maximum(x, 0.0) + jnp.log1p(jnp.exp(-jnp.abs(x)))


def _norm_mm_body(x_ref, g_ref, w_ref, *o_refs):
    h = _rms(x_ref[...], g_ref[...]).astype(BF)
    n = o_refs[0].shape[-1]
    for j, o_ref in enumerate(o_refs):
        o_ref[...] = _dot(h, w_ref[:, j * n:(j + 1) * n])


def norm_mm(x, g, w, n_out, tm, name):
    t, k = x.shape
    n = w.shape[1] // n_out
    return pl.pallas_call(
        _norm_mm_body,
        grid=(t // tm,),
        in_specs=[pl.BlockSpec((tm, k), lambda i: (i, 0)), _const_spec((1, k)), _wspec(w)],
        out_specs=[pl.BlockSpec((tm, n), lambda i: (i, 0))] * n_out,
        out_shape=[SDS((t, n), F32)] * n_out,
        compiler_params=_params(1),
        name=name,
    )(x, g.reshape(1, k), _warr(w))


def _mm_res_body(a_ref, w_ref, r_ref, o_ref):
    o_ref[...] = r_ref[...] + _dot(a_ref[...].astype(BF), w_ref[...])


def mm_res(a, w, res, tm, name):
    t, k = a.shape
    n = w.shape[1]
    return pl.pallas_call(
        _mm_res_body,
        grid=(t // tm,),
        in_specs=[pl.BlockSpec((tm, k), lambda i: (i, 0)), _wspec(w),
                  pl.BlockSpec((tm, n), lambda i: (i, 0))],
        out_specs=pl.BlockSpec((tm, n), lambda i: (i, 0)),
        out_shape=SDS((t, n), F32),
        compiler_params=_params(1),
        name=name,
    )(a, _warr(w), res)


def _memkv_body(x_ref, g_ref, w_ref, k32_ref, v32_ref, kbf_ref, vbf_ref):
    h = _rms(x_ref[...], g_ref[0]).astype(BF)
    d = kbf_ref.shape[-1]
    nh, dh = k32_ref.shape[2:]
    for j, (split_ref, bf_ref) in enumerate(((k32_ref, kbf_ref), (v32_ref, vbf_ref))):
        r = _dot(h, w_ref[0, :, j * d:(j + 1) * d])
        bf_ref[0] = r.astype(BF)
        for hh in range(nh):
            split_ref[0, :, hh, :] = r[:, hh * dh:(hh + 1) * dh]


def memory_kv_all(mem2d, norm_mem, w_kv, nh, tm):
    depth, d, _ = w_kv.shape
    t = mem2d.shape[0]
    flat = pl.BlockSpec((1, tm, d), lambda l, i: (l, i, 0))
    split = pl.BlockSpec((1, tm, nh, d // nh), lambda l, i: (l, i, 0, 0))
    return pl.pallas_call(
        _memkv_body,
        grid=(depth, t // tm),
        in_specs=[pl.BlockSpec((tm, d), lambda l, i: (i, 0)),
                  pl.BlockSpec((1, 1, d), lambda l, i: (l, 0, 0)),
                  pl.BlockSpec((1, d, 2 * d), lambda l, i: (l, 0, 0))],
        out_specs=[split, split, flat, flat],
        out_shape=[SDS((depth, t, nh, d // nh), F32)] * 2 + [SDS((depth, t, d), BF)] * 2,
        compiler_params=_params(2),
        name="memory_kv",
    )(mem2d, norm_mem.reshape(depth, 1, d), w_kv)


def _ffn_body(xp_ref, xs_ref, as_ref, ws_ref, g_ref, wu_ref, wd_ref, gf_ref, op_ref, os_ref, *, ck, final, ntp):
    def tile(x, o_ref):
        h = _rms(x, g_ref[...]).astype(BF)
        dff = wd_ref.shape[0]
        acc = x
        for c0 in range(0, dff, ck):
            gate = _dot(h, wu_ref[:, c0:c0 + ck])
            up = _dot(h, wu_ref[:, dff + c0:dff + c0 + ck])
            a = (_silu(gate) * up).astype(BF)
            acc = acc + _dot(a, wd_ref[c0:c0 + ck, :])
        if final:
            acc = _rms(acc, gf_ref[...])
        o_ref[...] = acc

    i = pl.program_id(0)
    pl.when(i < ntp)(lambda: tile(xp_ref[...], op_ref))
    pl.when(i == ntp)(lambda: tile(xs_ref[...] + _dot(as_ref[...].astype(BF), ws_ref[...]), os_ref))


def ffn(xp, xs, a_s, w_s, g, w_up, w_down, g_final, final, tm, name):
    t, d = xp.shape
    ntp = t // tm
    body = functools.partial(_ffn_body, ck=2 * LANES, final=final, ntp=ntp)
    p_spec = pl.BlockSpec((tm, d), lambda i: (jnp.minimum(i, ntp - 1), 0))
    s_spec = pl.BlockSpec(xs.shape, lambda i: (0, 0))
    return pl.pallas_call(
        body,
        grid=(ntp + 1,),
        in_specs=[p_spec, s_spec, pl.BlockSpec(a_s.shape, lambda i: (0, 0)), _wspec(w_s),
                  _const_spec((1, d)), _wspec(w_up), _wspec(w_down), _const_spec((1, d))],
        out_specs=[p_spec, s_spec],
        out_shape=[SDS((t, d), F32), SDS(xs.shape, F32)],
        compiler_params=_params(1),
        name=name,
    )(xp, xs, a_s, _warr(w_s), g.reshape(1, d), _warr(w_up), _warr(w_down), g_final.reshape(1, d))


def _xattn_p_part(x_ref, g_ref, wq_ref, k_ref, v_ref, wo_ref, o_ref, *, nh, pre_refs=None):
    x = x_ref[...]
    if pre_refs is not None:
        x = x + _dot(pre_refs[0][...].astype(BF), pre_refs[1][...])
    d = x.shape[-1]
    dh = d // nh
    h = _rms(x, g_ref[...]).astype(BF)
    q = (_dot(h, wq_ref[...]) * dh ** -0.5).astype(BF)
    outs = []
    for hh in range(nh):
        cs = slice(hh * dh, (hh + 1) * dh)
        s = _dot_nt(q[:, cs], k_ref[0, :, cs])
        p = jnp.exp(s - jnp.max(s, axis=-1, keepdims=True))
        l = jnp.sum(p, axis=-1, keepdims=True)
        outs.append((_dot(p.astype(BF), v_ref[0, :, cs]) / l).astype(BF))
    o_ref[...] = x + _dot(jnp.concatenate(outs, axis=-1), wo_ref[...])


def _xattn_s_part(qt_ref, k_ref, v_ref, o_ref, *, nb, nh):
    m_len, _, dh = k_ref.shape[2:]
    rows = m_len * nh
    ncol = qt_ref.shape[1]
    rh = lax.broadcasted_iota(jnp.int32, (rows, ncol), 0) % nh
    ch = lax.broadcasted_iota(jnp.int32, (rows, ncol), 1) % nh
    head_bias = jnp.where(rh == ch, 0.0, NEG_INF)
    s = [_dot_nt(k_ref[0, j].reshape(rows, dh).astype(BF), (qt_ref[j] * dh ** -0.5).astype(BF)) + head_bias
         for j in range(nb)]
    p = [jnp.exp(a - jnp.max(a, axis=0, keepdims=True)) for a in s]
    p = [(a / jnp.sum(a, axis=0, keepdims=True)).astype(BF) for a in p]
    for j in range(nb):
        o_ref[j] = _dot_tn(p[j], v_ref[0, j].reshape(rows, dh).astype(BF))


def _xattn_both_body(*refs, nb, has_pre):
    pre_refs = refs[:2] if has_pre else None
    x_ref, g_ref, wq_ref, k_ref, v_ref, wo_ref, qt_ref, ck_ref, cv_ref, o_ref, os_ref = refs[2 if has_pre else 0:]
    _xattn_s_part(qt_ref, ck_ref, cv_ref, os_ref, nb=nb, nh=ck_ref.shape[3])
    _xattn_p_part(x_ref, g_ref, wq_ref, k_ref, v_ref, wo_ref, o_ref, nh=XA_HEADS, pre_refs=pre_refs)


def xattn_both(x, g, w_q, k_bf, v_bf, layer, w_o, qt, cache_k, cache_v, bsz, tm, name, pre=None):
    t, d = x.shape
    nt = t // bsz // tm
    m_len = k_bf.shape[1] // bsz
    bs, ncol, dh = qt.shape
    nh_s = cache_k.shape[3]
    steps = bsz * nt
    assert bs % steps == 0, (bs, steps)
    nb = bs // steps
    kv = lambda a: a.reshape(a.shape[0] * bsz, m_len, d)
    row = lambda b, i: (b * nt + i, 0)
    cache_spec = pl.BlockSpec((1, nb, m_len, nh_s, dh), lambda b, i: (layer, b * nt + i, 0, 0, 0))
    pre_ins, pre_specs = [], []
    if pre is not None:
        pre_ins = [pre[0], _warr(pre[1])]
        pre_specs = [pl.BlockSpec((tm, pre[0].shape[1]), row), _wspec(pre[1])]
    return pl.pallas_call(
        functools.partial(_xattn_both_body, nb=nb, has_pre=pre is not None),
        grid=(bsz, nt),
        in_specs=pre_specs + [
            pl.BlockSpec((tm, d), row), _const_spec((1, d)), _wspec(w_q),
            pl.BlockSpec((1, m_len, d), lambda b, i: (layer * bsz + b, 0, 0)),
            pl.BlockSpec((1, m_len, d), lambda b, i: (layer * bsz + b, 0, 0)),
            _wspec(w_o),
            pl.BlockSpec((nb, ncol, dh), lambda b, i: (b * nt + i, 0, 0)), cache_spec, cache_spec],
        out_specs=[pl.BlockSpec((tm, d), row), pl.BlockSpec((nb, ncol, dh), lambda b, i: (b * nt + i, 0, 0))],
        out_shape=[SDS((t, d), F32), SDS((bs, ncol, dh), F32)],
        compiler_params=_params(2),
        name=name,
    )(*pre_ins, x, g.reshape(1, d), _warr(w_q), kv(k_bf), kv(v_bf), _warr(w_o), qt, cache_k, cache_v)


def _mixa_p_body(x_ref, g_ref, win_ref, wc_ref, wout_ref, o_ref, st_ref, zbuf, *, nt):
    i = pl.program_id(1)
    tm, d = x_ref.shape
    width = wc_ref.shape[0]

    @pl.when(i == 0)
    def _():
        zbuf[0:SUBLANES, :] = jnp.zeros((SUBLANES, d), F32)

    x = x_ref[...]
    h = _rms(x, g_ref[...]).astype(BF)
    gate_b = _dot(h, win_ref[:, 0:d])
    zn = _dot(h, win_ref[:, d:2 * d]) * _dot(h, win_ref[:, 2 * d:3 * d])
    zbuf[SUBLANES:SUBLANES + tm, :] = zn
    conv = wc_ref[width - 1:width, :] * zn
    for j in range(width - 1):
        off = SUBLANES - (width - 1) + j
        conv = conv + wc_ref[j:j + 1, :] * zbuf[off:off + tm, :]
    o_ref[...] = x + _dot((gate_b * conv).astype(BF), wout_ref[...])
    zbuf[0:SUBLANES, :] = zbuf[tm:tm + SUBLANES, :]

    @pl.when(i == nt - 1)
    def _():
        st_ref[0] = zbuf[SUBLANES - (width - 1):SUBLANES, :]


def mixer_a_prompt(x, g, w_in, w_conv, w_out, bsz, tm, name):
    t, d = x.shape
    nt = t // bsz // tm
    width = w_conv.shape[0]
    row = lambda b, i: (b * nt + i, 0)
    return pl.pallas_call(
        functools.partial(_mixa_p_body, nt=nt),
        grid=(bsz, nt),
        in_specs=[pl.BlockSpec((tm, d), row), _const_spec((1, d)), _wspec(w_in),
                  _const_spec(w_conv.shape), _wspec(w_out)],
        out_specs=[pl.BlockSpec((tm, d), row), pl.BlockSpec((1, width - 1, d), lambda b, i: (b, 0, 0))],
        out_shape=[SDS((t, d), F32), SDS((bsz, width - 1, d), F32)],
        scratch_shapes=[pltpu.VMEM((tm + SUBLANES, d), F32)],
        compiler_params=_params(2),
        name=name,
    )(x, g.reshape(1, d), _warr(w_in), w_conv, _warr(w_out))


def _posmajor_conv(state, z, wc_ref, bsz):
    width = wc_ref.shape[0]
    n = z.shape[0]
    zpad = jnp.concatenate([state, z], axis=0)
    conv = wc_ref[0:1, :] * zpad[0:n]
    for j in range(1, width):
        conv = conv + wc_ref[j:j + 1, :] * zpad[j * bsz:j * bsz + n]
    return conv, zpad[n:]


def _mixa_s_body(x_ref, st_ref, g_ref, win_ref, wc_ref, wout_ref, o_ref, nst_ref, *, bsz):
    x = x_ref[...]
    d = x.shape[-1]
    h = _rms(x, g_ref[...]).astype(BF)
    gate_b = _dot(h, win_ref[:, 0:d])
    zn = _dot(h, win_ref[:, d:2 * d]) * _dot(h, win_ref[:, 2 * d:3 * d])
    conv, new_state = _posmajor_conv(st_ref[...], zn, wc_ref, bsz)
    o_ref[...] = x + _dot((gate_b * conv).astype(BF), wout_ref[...])
    nst_ref[...] = new_state


def mixer_a_sample(x, state_pm, g, w_in, w_conv, w_out, bsz, name):
    t, d = x.shape
    return pl.pallas_call(
        functools.partial(_mixa_s_body, bsz=bsz),
        grid=(1,),
        in_specs=[_const_spec(x.shape), _const_spec(state_pm.shape), _const_spec((1, d)),
                  _wspec(w_in), _const_spec(w_conv.shape), _wspec(w_out)],
        out_specs=[_whole_spec(x.shape), _whole_spec(state_pm.shape)],
        out_shape=[SDS(x.shape, F32), SDS(state_pm.shape, F32)],
        compiler_params=_params(1),
        name=name,
    )(x, state_pm, g.reshape(1, d), _warr(w_in), w_conv, _warr(w_out))


GDN_COL_CHUNK = 2 * LANES


def _gdn_qkv_chunk(conv, c0, q_ref, k_ref, v_ref):
    nh, dk = GDN_HEADS, GDN_DK
    hd = nh * dk
    act = _silu(conv)
    section, col = c0 // hd, c0 % hd
    if section == 2:
        v_ref[:, col:col + GDN_COL_CHUNK] = act
        return
    ref = q_ref if section == 0 else k_ref
    for j in range(GDN_COL_CHUNK // dk):
        t = act[:, j * dk:(j + 1) * dk]
        t = t * lax.rsqrt(jnp.sum(t * t, axis=-1, keepdims=True) + EPS)
        ref[:, col + j * dk:col + (j + 1) * dk] = t * dk ** -0.5 if section == 0 else t


def _gdn_gates(ba, alog_ref, dtb_ref, bg_ref):
    nh = GDN_HEADS
    lane = lax.broadcasted_iota(jnp.int32, ba.shape, 1)
    beta = jax.nn.sigmoid(ba)
    g = -jnp.exp(alog_ref[...]) * _softplus(ba + dtb_ref[...])
    bg_ref[...] = jnp.where(lane < nh, beta, jnp.where(lane < 2 * nh, g, 0.0))


def _gdn_proj_p_body(x_ref, g_ref, w_ref, wba_ref, wc_ref, alog_ref, dtb_ref,
                     q_ref, k_ref, v_ref, z_ref, bg_ref, st_ref, cbuf, *, nt):
    i = pl.program_id(1)
    tm = x_ref.shape[0]
    width, cc = wc_ref.shape
    ck = GDN_COL_CHUNK

    @pl.when(i == 0)
    def _():
        cbuf[0:SUBLANES, :] = jnp.zeros((SUBLANES, cc), F32)

    h = _rms(x_ref[...], g_ref[...]).astype(BF)
    for c0 in range(0, cc, ck):
        cols = slice(c0, c0 + ck)
        pre = _dot(h, w_ref[:, cols])
        cbuf[SUBLANES:SUBLANES + tm, cols] = pre
        conv = wc_ref[width - 1:width, cols] * pre
        for j in range(width - 1):
            off = SUBLANES - (width - 1) + j
            conv = conv + wc_ref[j:j + 1, cols] * cbuf[off:off + tm, cols]
        _gdn_qkv_chunk(conv, c0, q_ref, k_ref, v_ref)
    for c0 in range(0, z_ref.shape[1], ck):
        z_ref[:, c0:c0 + ck] = _dot(h, w_ref[:, cc + c0:cc + c0 + ck])
    _gdn_gates(_dot(h, wba_ref[...]), alog_ref, dtb_ref, bg_ref)
    cbuf[0:SUBLANES, :] = cbuf[tm:tm + SUBLANES, :]

    @pl.when(i == nt - 1)
    def _():
        st_ref[0] = cbuf[SUBLANES - (width - 1):SUBLANES, :]


def gdn_proj_prompt(x, g, w_qkvz, w_ba, w_conv, alog, dtb, bsz, tm, name):
    t, d = x.shape
    nt = t // bsz // tm
    width, cc = w_conv.shape
    hd = GDN_HEADS * GDN_DK
    row = lambda b, i: (b * nt + i, 0)
    tile = lambda n: pl.BlockSpec((tm, n), row)
    return pl.pallas_call(
        functools.partial(_gdn_proj_p_body, nt=nt),
        grid=(bsz, nt),
        in_specs=[tile(d), _const_spec((1, d)), _wspec(w_qkvz), _wspec(w_ba),
                  _const_spec(w_conv.shape), _const_spec(alog.shape), _const_spec(dtb.shape)],
        out_specs=[tile(hd), tile(hd), tile(hd), tile(hd), tile(LANES),
                   pl.BlockSpec((1, width - 1, cc), lambda b, i: (b, 0, 0))],
        out_shape=[SDS((t, hd), F32)] * 4 + [SDS((t, LANES), F32), SDS((bsz, width - 1, cc), F32)],
        scratch_shapes=[pltpu.VMEM((tm + SUBLANES, cc), F32)],
        compiler_params=_params(2),
        name=name,
    )(x, g.reshape(1, d), _warr(w_qkvz), _warr(w_ba), w_conv, alog, dtb)


def _gdn_proj_s_body(x_ref, st_ref, g_ref, w_ref, wba_ref, wc_ref, alog_ref, dtb_ref,
                     q_ref, k_ref, v_ref, z_ref, bg_ref, nst_ref, *, bsz):
    cc = wc_ref.shape[1]
    h = _rms(x_ref[...], g_ref[...]).astype(BF)
    qkv = _dot(h, w_ref[:, :cc])
    z_ref[...] = _dot(h, w_ref[:, cc:cc + z_ref.shape[1]])
    ba = _dot(h, wba_ref[...])
    conv, new_state = _posmajor_conv(st_ref[...], qkv, wc_ref, bsz)
    for c0 in range(0, cc, GDN_COL_CHUNK):
        _gdn_qkv_chunk(conv[:, c0:c0 + GDN_COL_CHUNK], c0, q_ref, k_ref, v_ref)
    _gdn_gates(ba, alog_ref, dtb_ref, bg_ref)
    nst_ref[...] = new_state


def gdn_proj_sample(x, state_pm, g, w_qkvz, w_ba, w_conv, alog, dtb, bsz, name):
    t, d = x.shape
    cc = w_conv.shape[1]
    hd = GDN_HEADS * GDN_DK
    ins = (x, state_pm, g.reshape(1, d), w_qkvz, w_ba, w_conv, alog, dtb)
    outs = [SDS((t, hd), F32)] * 4 + [SDS((t, LANES), F32), SDS(state_pm.shape, F32)]
    return pl.pallas_call(
        functools.partial(_gdn_proj_s_body, bsz=bsz),
        grid=(1,),
        in_specs=[_wspec(a) for a in ins],
        out_specs=[_whole_spec(o.shape) for o in outs],
        out_shape=outs,
        compiler_params=_params(1),
        name=name,
    )(*[_warr(a) for a in ins])


def _pad_transpose(a):
    c = a.shape[0]
    if c < LANES:
        a = jnp.concatenate([a, jnp.zeros((LANES - c, a.shape[1]), a.dtype)], axis=0)
    return a.T


def _gdn_scan_body(*refs, c, per_chunk_state, fuse_out):
    if fuse_out:
        x_ref, refs = refs[0], refs[1:]
    (q_ref, k_ref, v_ref, z_ref, bg_ref, tri_ref, s0_ref, gn_ref) = refs[:8]
    refs = refs[8:]
    if fuse_out:
        wout_ref, refs = refs[0], refs[1:]
    o_ref, s_ref, qg_sc, kdec_sc, rk_sc, rv_sc, gam_sc, beta_sc, egl_sc, o_sc = refs

    nh, dk = GDN_HEADS, GDN_DK
    rows, hd = q_ref.shape
    n_chunks = rows // c
    n_double = int(math.log2(c)) - 1

    if not per_chunk_state:
        @pl.when(pl.program_id(1) == 0)
        def _():
            s_ref[...] = s0_ref[...]

    bg = bg_ref[...]
    gam = _dot_exact_rhs(bg, tri_ref[...], dot=lambda a, b: _dot(b, a))
    beta_b = jnp.concatenate([jnp.broadcast_to(bg[:, hh:hh + 1], (rows, dk)) for hh in range(nh)], axis=1)
    gam_b = jnp.concatenate([jnp.broadcast_to(gam[:, nh + hh:nh + hh + 1], (rows, dk)) for hh in range(nh)], axis=1)
    gl_b = jnp.broadcast_to(gam_b.reshape(n_chunks, c, hd)[:, c - 1:c, :], (n_chunks, c, hd)).reshape(rows, hd)
    eg = jnp.exp(gam_b)
    k = k_ref[...]
    qg_sc[...] = q_ref[...] * eg
    rk_sc[...] = beta_b * eg * k
    rv_sc[...] = beta_b * v_ref[...]
    kdec_sc[...] = k * jnp.exp(gl_b - gam_b)
    gam_sc[...] = gam_b
    beta_sc[...] = beta_b
    egl_sc[...] = jnp.exp(gl_b)

    ii = lax.broadcasted_iota(jnp.int32, (c, c), 0)
    jj = lax.broadcasted_iota(jnp.int32, (c, c), 1)
    incl = ii >= jj
    strict = ii > jj

    chains = [(slice(ci * c, (ci + 1) * c), slice(hh * dk, (hh + 1) * dk))
              for ci in range(n_chunks) for hh in range(nh)]
    kq, decay, pm = [], [], []
    for rs, hc in chains:
        kc = k_ref[rs, hc]
        kq.append(_dot_nt(jnp.concatenate([kc, q_ref[rs, hc]], axis=0).astype(BF), kc.astype(BF)))
    for (rs, hc), kq_i in zip(chains, kq):
        gcol = gam_sc[rs, hc]
        diff = gcol[:, :c] - _pad_transpose(gcol)[:c, :c]
        dec = jnp.where(incl, jnp.exp(jnp.where(incl, diff, 0.0)), 0.0)
        decay.append(dec)
        pm.append(jnp.where(strict, -(beta_sc[rs, hc][:, :c] * kq_i[:c] * dec), 0.0))
    mk = pm
    for _ in range(n_double):
        mk = [_dot(a.astype(BF), a.astype(BF)) for a in mk]
        pm = [p + a + _dot(p.astype(BF), a.astype(BF)) for p, a in zip(pm, mk)]
    sol, lhs2 = [], []
    for (rs, hc), p, kq_i, dec in zip(chains, pm, kq, decay):
        rhs = jnp.concatenate([rv_sc[rs, hc], rk_sc[rs, hc]], axis=1)
        sol.append(rhs + _dot(p.astype(BF), rhs.astype(BF)))
        kdt = _pad_transpose(kdec_sc[rs, hc])[:, :c]
        lhs2.append(jnp.concatenate([kq_i[c:] * dec, kdt], axis=0).astype(BF))

    def advance(idxs, states):
        t1 = []
        for idx, s in zip(idxs, states):
            rs, hc = chains[idx]
            t1.append(_dot(jnp.concatenate([sol[idx][:, dk:], qg_sc[rs, hc]], axis=0).astype(BF), s.astype(BF)))
        t2 = [_dot(lhs2[idx], (sol[idx][:, :dk] - t[:c]).astype(BF)) for idx, t in zip(idxs, t1)]
        new_states = []
        for idx, s, ta, tb in zip(idxs, states, t1, t2):
            rs, hc = chains[idx]
            o_sc[rs, hc] = ta[c:] + tb[:c]
            egl = jnp.broadcast_to(egl_sc[rs.start:rs.start + 1, hc], (dk, dk))
            new_states.append(egl * s + tb[c:])
        return new_states

    if per_chunk_state:
        idxs = list(range(len(chains)))
        new_states = advance(idxs, [s0_ref[idx // nh, idx % nh] for idx in idxs])
        for idx, s_new in zip(idxs, new_states):
            s_ref[idx // nh, idx % nh] = s_new
    else:
        state = [s_ref[0, hh] for hh in range(nh)]
        for ci in range(n_chunks):
            state = advance([ci * nh + hh for hh in range(nh)], state)
        for hh in range(nh):
            s_ref[0, hh] = state[hh]

    outs = []
    for hh in range(nh):
        hc = slice(hh * dk, (hh + 1) * dk)
        outs.append(_rms(o_sc[:, hc], gn_ref[...]) * _silu(z_ref[:, hc]))
    yv = jnp.concatenate(outs, axis=-1)
    if fuse_out:
        o_ref[...] = x_ref[...] + _dot(yv.astype(BF), wout_ref[...])
    else:
        o_ref[...] = yv


def _chunk_cumsum_matrix(rows, c):
    r = np.arange(rows)
    return jnp.asarray((r[:, None] >= r[None, :]) & (r[:, None] // c == r[None, :] // c), BF)


def gdn_scan(x, q, k, v, z, bg, s0, g_norm, w_out, *, c, rows, n_seq, per_chunk_state, name):
    t, hd = q.shape
    nt = t // n_seq // rows
    fuse_out = w_out is not None
    tri = _chunk_cumsum_matrix(rows, c)
    ns = rows // c if per_chunk_state else 1
    row = lambda b, i: (b * nt + i, 0)
    tile = lambda n: pl.BlockSpec((rows, n), row)
    st_spec = pl.BlockSpec((ns,) + s0.shape[1:], lambda b, i: (b, 0, 0, 0))
    ins, specs = [], []
    if fuse_out:
        ins.append(x)
        specs.append(tile(x.shape[1]))
    ins += [q, k, v, z, bg, tri, s0, g_norm.reshape(1, GDN_DK)]
    specs += [tile(hd)] * 4 + [tile(LANES), _const_spec(tri.shape), st_spec, _const_spec((1, GDN_DK))]
    d_out = hd
    if fuse_out:
        ins.append(_warr(w_out))
        specs.append(_wspec(w_out))
        d_out = w_out.shape[1]
    return pl.pallas_call(
        functools.partial(_gdn_scan_body, c=c, per_chunk_state=per_chunk_state, fuse_out=fuse_out),
        grid=(n_seq, nt),
        in_specs=specs,
        out_specs=[tile(d_out), st_spec],
        out_shape=[SDS((t, d_out), F32), SDS(s0.shape, F32)],
        scratch_shapes=[pltpu.VMEM((rows, hd), F32)] * 8,
        compiler_params=_params(2),
        name=name,
    )(*ins)


def _topk_mask(sb, valid, axis):
    n = sb.shape[axis]
    idx = lax.broadcasted_iota(jnp.int32, sb.shape, axis)
    cnt = jnp.zeros(sb.shape, F32)
    for m in range(n):
        sm = lax.slice_in_dim(sb, m, m + 1, axis=axis)
        ahead = (sm > sb) | ((sm == sb) & (m < idx))
        cnt = cnt + jnp.where(ahead, 1.0, 0.0)
    return jnp.where((cnt < MOBA_TOPK) & valid, 1.0, 0.0)


def _moba_p_body(q_ref, k_ref, v_ref, oh_ref, o_ref, *, part, parts):
    length, dh = k_ref.shape
    blk = MOBA_BLOCK
    nb = length // blk
    scale = dh ** -0.5
    k = k_ref[...]
    kaug = jnp.concatenate([k.astype(BF), oh_ref[...]], axis=1)
    vb = v_ref[...].astype(BF)
    km = jnp.sum(k.reshape(nb, blk, dh), axis=1) * (1.0 / blk)
    brow = lax.broadcasted_iota(jnp.int32, (nb, blk), 0)
    pad_rows = jnp.zeros((LANES - nb, blk), F32)
    slots = []
    for j in range(nb // parts):
        i = parts * j + part
        rows = pl.ds(pl.multiple_of(i * blk, blk), blk)
        q = q_ref[rows, :]
        sbt = jnp.where(brow < i, _dot_hi(km, q, dot=_dot_nt), NEG_INF)
        selt = _topk_mask(sbt, sbt > NEG_INF / 2, axis=0)
        bias_t = jnp.where((selt > 0.5) | (brow == i), 0.0, NEG_INF)
        bias = jnp.concatenate([bias_t, pad_rows], axis=0).T
        slots.append((i, rows, jnp.concatenate([(q * scale).astype(BF), bias.astype(BF)], axis=1)))
    for j, (i, rows, qaug) in enumerate(slots):
        n_keys = parts * (j + 1) * blk
        s = _dot_nt(qaug, kaug[:n_keys])
        key_pos = lax.broadcasted_iota(jnp.int32, (blk, n_keys), 1)
        query_pos = lax.broadcasted_iota(jnp.int32, (blk, n_keys), 0) + i * blk
        s = jnp.where(key_pos <= query_pos, s, NEG_INF)
        pe = jnp.exp(s - jnp.max(s, axis=-1, keepdims=True))
        l = jnp.sum(pe, axis=-1, keepdims=True)
        o_ref[rows, :] = _dot(pe.astype(BF), vb[:n_keys]) / l


def _lane_group_reduce(row, op, ncol):
    a = jnp.broadcast_to(row, (SUBLANES, LANES))
    shift = ncol
    while shift < LANES:
        a = op(a, pltpu.roll(a, shift, 1))
        shift *= 2
    return a[:1]


def _moba_s_body(pt_ref, qt_ref, hb_ref, kn_ref, vn_ref, *refs, n_pages, ncol):
    del pt_ref
    kp = refs[:n_pages]
    vp = refs[n_pages:2 * n_pages]
    o_ref, s_sc = refs[2 * n_pages:]
    page, nh, dh = kp[0].shape[1:]
    bp = MOBA_BLOCK // page
    nb = n_pages // bp
    prow = page * nh
    pack = LANES // ncol
    q_rep = jnp.concatenate([qt_ref[0]] * pack, axis=0)
    lane = lax.broadcasted_iota(jnp.int32, (1, LANES), 1)
    group = lane // ncol
    qs = q_rep * dh ** -0.5
    rg = lax.broadcasted_iota(jnp.int32, (LANES, pack * dh), 0) // ncol
    cg = lax.broadcasted_iota(jnp.int32, (LANES, pack * dh), 1) // dh
    rhs = jnp.where(rg == cg, jnp.concatenate([qs] * pack, axis=1), 0.0).astype(BF)

    km = []
    for n in range(nb):
        tot = kp[n * bp][0].sum(axis=0)
        for j in range(1, bp):
            tot = tot + kp[n * bp + j][0].sum(axis=0)
        km.append(tot * (1.0 / MOBA_BLOCK))
    km2 = jnp.concatenate(km, axis=0)
    sbm = _dot_hi(km2, q_rep, dot=_dot_nt)
    rh = lax.broadcasted_iota(jnp.int32, sbm.shape, 0) % nh
    ch = lax.broadcasted_iota(jnp.int32, sbm.shape, 1) % ncol % nh
    sb = jnp.sum(jnp.where(rh == ch, sbm, 0.0).reshape(nb, nh, LANES), axis=1)
    sel = _topk_mask(sb, jnp.full(sb.shape, True), axis=0)
    sel_bias = jnp.where(sel > 0.5, 0.0, NEG_INF)

    n_own = kn_ref.shape[1]
    ro = lax.broadcasted_iota(jnp.int32, (n_own, LANES), 0)
    co = lax.broadcasted_iota(jnp.int32, (n_own, LANES), 1)
    own_ok = (ro % nh == co % nh) & (ro // nh <= co // nh) & (co < ncol)
    s_own = jnp.where(own_ok, _dot_nt(kn_ref[0].astype(BF), qs.astype(BF)), NEG_INF)
    mx = jnp.max(s_own, axis=0, keepdims=True)

    n_groups = n_pages // pack
    for j in range(n_groups):
        pages = range(j * pack, (j + 1) * pack)
        k4 = jnp.concatenate([kp[p][0].reshape(prow, dh).astype(BF) for p in pages], axis=1)
        bias = sel_bias[(j * pack) // bp:(j * pack) // bp + 1, :]
        for g in range(1, pack):
            blk = (j * pack + g) // bp
            bias = jnp.where(group >= g, sel_bias[blk:blk + 1, :], bias)
        s = _dot_nt(k4, rhs) + (hb_ref[...] + bias)
        s_sc[j * prow:(j + 1) * prow, :] = s
        mx = jnp.maximum(mx, jnp.max(s, axis=0, keepdims=True))
    mx = _lane_group_reduce(mx, jnp.maximum, ncol)

    pe = jnp.exp(s_own - mx)
    l = jnp.sum(pe, axis=0, keepdims=True)
    acc = _dot_tn(pe.astype(BF), vn_ref[0].astype(BF))[:ncol]
    for j in range(n_groups):
        pages = range(j * pack, (j + 1) * pack)
        pe = jnp.exp(s_sc[j * prow:(j + 1) * prow, :] - mx)
        l = l + jnp.sum(pe, axis=0, keepdims=True)
        v4 = jnp.concatenate([vp[p][0].reshape(prow, dh).astype(BF) for p in pages], axis=1)
        r = _dot_tn(pe.astype(BF), v4)
        for g in range(pack):
            acc = acc + r[g * ncol:(g + 1) * ncol, g * dh:(g + 1) * dh]
    l = _lane_group_reduce(l, jnp.add, ncol)
    l_col = jnp.broadcast_to(l, (LANES, LANES)).T[:ncol, :1]
    o_ref[0] = acc / l_col


def _moba_both_body(pt_ref, q_ref, k_ref, v_ref, oh_ref, qt_ref, hb_ref, kn_ref, vn_ref, *refs,
                    n_pages, ncol, parts):
    pages = refs[:2 * n_pages]
    o_ref, os_ref, s_sc = refs[2 * n_pages:]
    _moba_s_body(pt_ref, qt_ref, hb_ref, kn_ref, vn_ref, *pages, os_ref, s_sc, n_pages=n_pages, ncol=ncol)
    _moba_p_body(q_ref, k_ref, v_ref, oh_ref, o_ref, part=pl.program_id(2), parts=parts)


def moba_both(q, k, v, bsz, qt, kn, vn, pool_k, pool_v, page_table, name):
    t, w = q.shape
    dh = MOBA_DH
    nh = w // dh
    length = t // bsz
    nb = length // MOBA_BLOCK
    bs, ncol, _ = qt.shape
    assert bs % (bsz * nh) == 0, (bs, bsz, nh)
    parts = bs // (bsz * nh)
    assert parts >= 1 and nb % parts == 0, (parts, nb)
    n_pages = page_table.shape[1]
    page, nh_s = pool_k.shape[1:3]
    n_own = kn.shape[1]
    pack = LANES // ncol
    onehot = jnp.asarray(np.arange(length)[:, None] // MOBA_BLOCK == np.arange(dh)[None, :], BF)
    r = np.arange(page * nh_s)[:, None]
    c = np.arange(LANES)[None, :]
    head_bias = jnp.asarray(np.where(r % nh_s == c % ncol % nh_s, 0.0, NEG_INF), F32)

    seq = lambda b, h, p, pt: (b * nh + h) * parts + p
    p_spec = pl.BlockSpec((length, dh), lambda b, h, p, pt: (b, h))

    def page_spec(pg):
        return pl.BlockSpec((1, page, nh_s, dh), lambda b, h, p, pt: (pt[seq(b, h, p, pt), pg], 0, 0, 0))

    s_spec = lambda rows: pl.BlockSpec((1, rows, dh), lambda b, h, p, pt: (seq(b, h, p, pt), 0, 0))
    grid_spec = pltpu.PrefetchScalarGridSpec(
        num_scalar_prefetch=1,
        grid=(bsz, nh, parts),
        in_specs=[p_spec, p_spec, p_spec,
                  pl.BlockSpec(onehot.shape, lambda b, h, p, pt: (0, 0)),
                  s_spec(ncol),
                  pl.BlockSpec(head_bias.shape, lambda b, h, p, pt: (0, 0)),
                  s_spec(n_own), s_spec(n_own)]
                 + [page_spec(pg) for pg in range(n_pages)] * 2,
        out_specs=[p_spec, s_spec(ncol)],
        scratch_shapes=[pltpu.VMEM((n_pages // pack * page * nh_s, LANES), F32)],
    )
    return pl.pallas_call(
        functools.partial(_moba_both_body, n_pages=n_pages, ncol=ncol, parts=parts),
        grid_spec=grid_spec,
        out_shape=[SDS((t, w), F32), SDS((bs, ncol, dh), F32)],
        compiler_params=_params(3),
        name=name,
    )(page_table, q, k, v, onehot, qt, head_bias, kn, vn, *([pool_k] * n_pages), *([pool_v] * n_pages))


def _to_pm(a):
    b, l, c = a.shape
    return jnp.transpose(a, (1, 0, 2)).reshape(l * b, c)


def _from_pm(a, bsz):
    n, c = a.shape
    return jnp.transpose(a.reshape(n // bsz, bsz, c), (1, 0, 2))


def _pad_lanes(vec, offset):
    out = jnp.zeros((1, LANES), F32)
    return lax.dynamic_update_slice(out, vec.reshape(1, -1).astype(F32), (0, offset))


def kernel(x_prompt, x_sample, state_a_conv, state_b_conv, state_b_rec, cache_c_k, cache_c_v, cache_mem_k, cache_mem_v, page_table, mem_prompt, norm_mix, norm_mem, norm_xattn, norm_ffn, norm_final, a_w_in, a_w_conv, a_w_out, b_w_in, b_w_conv, b_a_log, b_dt_bias, b_norm, b_w_out, c_w_qkv, c_w_out, x_w_q, x_w_kv, x_w_o, f_w_up, f_w_down):
    bp, seq, d = x_prompt.shape
    bs, dseq, _ = x_sample.shape
    depth = norm_mix.shape[0]
    n_mixers = 3
    tm = ROW_TILE
    ts = bs * dseq

    bf = lambda w: w.astype(BF)
    a_w_in, a_w_out, b_w_out, c_w_qkv, c_w_out = map(bf, (a_w_in, a_w_out, b_w_out, c_w_qkv, c_w_out))
    x_w_q, x_w_kv, x_w_o, f_w_up, f_w_down = map(bf, (x_w_q, x_w_kv, x_w_o, f_w_up, f_w_down))
    cc = b_w_conv.shape[-1]
    hd = GDN_HEADS * GDN_DK
    b_w_qkvz = bf(b_w_in)
    b_w_ba = bf(jnp.pad(b_w_in[:, :, cc + hd:], ((0, 0), (0, 0), (0, LANES - 2 * GDN_HEADS))))

    xp = x_prompt.reshape(bp * seq, d)
    xs = _to_pm(x_sample)

    mem_len = mem_prompt.shape[1]
    mk32, mv32, mkbf, mvbf = memory_kv_all(mem_prompt.reshape(bp * mem_len, d), norm_mem, x_w_kv, XA_HEADS, tm)

    a_p, a_s, bc_p, bc_s, br_p, br_s = [], [], [], [], [], []
    ck_p, cv_p, ck_s, cv_s = [], [], [], []
    for i in range(depth):
        kind, j = i % n_mixers, i // n_mixers
        tag = f"l{i}"
        pre = None
        if kind == 0:
            xp, st = mixer_a_prompt(xp, norm_mix[i], _Layer(a_w_in, j), a_w_conv[j], _Layer(a_w_out, j), bp, 2 * tm, tag + "_mixa_p")
            a_p.append(st)
            st_pm = _to_pm(state_a_conv[j])
            xs, nst = mixer_a_sample(xs, st_pm, norm_mix[i], _Layer(a_w_in, j), a_w_conv[j], _Layer(a_w_out, j), bs, tag + "_mixa_s")
            a_s.append(_from_pm(nst, bs))
        elif kind == 1:
            alog = _pad_lanes(b_a_log[j], GDN_HEADS)
            dtb = _pad_lanes(b_dt_bias[j], GDN_HEADS)
            q, k, v, z, bg, st = gdn_proj_prompt(xp, norm_mix[i], _Layer(b_w_qkvz, j), _Layer(b_w_ba, j), b_w_conv[j], alog, dtb,
                                                 bp, GDN_ROW_TILE, tag + "_gdnproj_p")
            bc_p.append(st)
            s0 = jnp.zeros((bp, GDN_HEADS, GDN_DK, GDN_DK), F32)
            xp, s_fin = gdn_scan(xp, q, k, v, z, bg, s0, b_norm[j], _Layer(b_w_out, j), c=math.gcd(seq, GDN_CHUNK),
                                 rows=GDN_ROW_TILE, n_seq=bp, per_chunk_state=False, name=tag + "_gdnscan_p")
            br_p.append(s_fin)
            st_pm = _to_pm(state_b_conv[j])
            q, k, v, z, bg, nst = gdn_proj_sample(xs, st_pm, norm_mix[i], _Layer(b_w_qkvz, j), _Layer(b_w_ba, j), b_w_conv[j],
                                                  alog, dtb, bs, tag + "_gdnproj_s")
            bc_s.append(_from_pm(nst, bs))
            cpad = SUBLANES
            padded = lambda a: jnp.pad(_from_pm(a, bs), ((0, 0), (0, cpad - dseq), (0, 0))).reshape(bs * cpad, -1)
            nbatch = GDN_SAMPLE_SEQS_PER_STEP
            y, s_fin = gdn_scan(None, padded(q), padded(k), padded(v), padded(z), padded(bg), state_b_rec[j],
                                b_norm[j], None, c=cpad, rows=nbatch * cpad, n_seq=bs // nbatch,
                                per_chunk_state=True, name=tag + "_gdnscan_s")
            br_s.append(s_fin)
            y = _to_pm(y.reshape(bs, cpad, hd)[:, :dseq])
            xs = mm_res(y, _Layer(b_w_out, j), xs, ts, tag + "_gdnout_s")
        else:
            q, k, v = norm_mm(xp, norm_mix[i], _Layer(c_w_qkv, j), 3, 2 * tm, tag + "_mobaqkv_p")
            ck_p.append(k.reshape(bp, seq, MOBA_HEADS, MOBA_DH))
            cv_p.append(v.reshape(bp, seq, MOBA_HEADS, MOBA_DH))
            qs, ks, vs = norm_mm(xs, norm_mix[i], _Layer(c_w_qkv, j), 3, ts, tag + "_mobaqkv_s")
            kn = _from_pm(ks, bs)
            vn = _from_pm(vs, bs)
            ck_s.append(kn.reshape(bs, dseq, MOBA_HEADS, MOBA_DH))
            cv_s.append(vn.reshape(bs, dseq, MOBA_HEADS, MOBA_DH))
            qt = _from_pm(qs, bs).reshape(bs, dseq * MOBA_HEADS, MOBA_DH)
            rows_kv = lambda a: a.reshape(bs, dseq * MOBA_HEADS, MOBA_DH)
            o, os_ = moba_both(q, k, v, bp, qt, rows_kv(kn), rows_kv(vn), cache_c_k[j], cache_c_v[j], page_table,
                               tag + "_moba")
            pre = (o, _Layer(c_w_out, j))
            xs = mm_res(_to_pm(os_.reshape(bs, dseq, MOBA_HEADS * MOBA_DH)), _Layer(c_w_out, j), xs, ts, tag + "_mobaout_s")

        (qs,) = norm_mm(xs, norm_xattn[i], _Layer(x_w_q, i), 1, ts, tag + "_xattnq_s")
        dh = d // XA_HEADS
        qt = _from_pm(qs, bs).reshape(bs, dseq * XA_HEADS, dh)
        xp, o = xattn_both(xp, norm_xattn[i], _Layer(x_w_q, i), mkbf, mvbf, i, _Layer(x_w_o, i),
                           qt, cache_mem_k, cache_mem_v, bp, tm, tag + "_xattn", pre=pre)

        last = i == depth - 1
        xp, xs = ffn(xp, xs, _to_pm(o.reshape(bs, dseq, d)), _Layer(x_w_o, i), norm_ffn[i],
                     _Layer(f_w_up, i), _Layer(f_w_down, i), norm_final, last, tm, tag + "_ffn")

    mem_shape = (depth, bp, mem_len, XA_HEADS, d // XA_HEADS)
    return (xp.reshape(bp, seq, d), _from_pm(xs, bs),
            jnp.stack(a_p), jnp.stack(a_s),
            jnp.stack(bc_p), jnp.stack(bc_s),
            jnp.stack(br_p), jnp.stack(br_s),
            jnp.stack(ck_p), jnp.stack(cv_p), jnp.stack(ck_s), jnp.stack(cv_s),
            mk32.reshape(mem_shape), mv32.reshape(mem_shape))
```
